```python
import math
import jax
import jax.numpy as jnp
from jax import lax
import numpy as np

D_MODEL = 1024
BATCH = 32
SEQ = 256
DEPTH = 4
DEC_BATCH = 2
DEC_SEQ = 4096
PAST_LEN = 256

GRID_W = 64
D_MIX = D_MODEL
A_HEADS = 4
A_QK = 64
A_V = 2 * A_QK
A_WIDTH = A_HEADS * A_V
B_HEADS = 4
B_DK = 64
B_DV = 64
B_WIDTH = B_HEADS * B_DV
B_QKV = B_HEADS * (2 * B_DK + B_DV)
C_HEADS = 4
C_DK = 64
C_DV = 64
C_KEYS = C_HEADS * C_DK
C_WIDTH = C_HEADS * C_DV
SHORT_CONV = 3
DELTA_CHUNK = 64
HGRN_CHUNK = 16
Q_BLOCK = 128
ROPE_BASE = 10000.0
D_FF = ((8 * D_MODEL // 3 + 255) // 256) * 256
ALPHA = (2 * DEPTH) ** 0.25
BETA_INIT = (8 * DEPTH) ** -0.25
LN_EPS = 1e-5
RMS_EPS = 1e-6
IN_SPLITS = (A_HEADS * 2 * A_QK, A_HEADS * 2 * A_QK, A_WIDTH,
             B_QKV, B_WIDTH, 2 * B_HEADS, 2 * B_HEADS,
             C_KEYS, 2 * C_KEYS, C_WIDTH, C_WIDTH)
D_IN = sum(IN_SPLITS)

kernel_name = 'hybrid_diff_delta_hgrn2_dit_step'


def _split_points():
    return [int(s) for s in np.cumsum(IN_SPLITS)[:-1]]


def layer_norm(x, g, b):
    xf = x.astype(jnp.float32)
    mu = jnp.mean(xf, -1, keepdims=True)
    var = jnp.mean(jnp.square(xf - mu), -1, keepdims=True)
    return ((xf - mu) * lax.rsqrt(var + LN_EPS) * g.astype(jnp.float32) + b.astype(jnp.float32)).astype(x.dtype)


def rms_norm(x, g):
    xf = x.astype(jnp.float32)
    return (xf * lax.rsqrt(jnp.mean(xf * xf, -1, keepdims=True) + RMS_EPS) * g.astype(jnp.float32)).astype(x.dtype)


def l2norm(x):
    return x * lax.rsqrt(jnp.sum(x * x, -1, keepdims=True) + 1e-6)


def short_conv(x, w):
    return lax.conv_general_dilated(x, w[:, None, :].astype(x.dtype), window_strides=(1,),
                                    padding=[(SHORT_CONV // 2, SHORT_CONV // 2)],
                                    dimension_numbers=('NWC', 'WIO', 'NWC'),
                                    feature_group_count=x.shape[-1])


def axial_rope(x):
    L = x.shape[1]
    n_rows = L // GRID_W
    row = jnp.repeat(jnp.arange(n_rows), GRID_W)
    col = jnp.tile(jnp.arange(GRID_W), n_rows)
    half = A_QK // 2
    nf = half // 2
    inv_freq = ROPE_BASE ** (-jnp.arange(nf, dtype=jnp.float32) / nf)
    bshape = (1, L) + (1,) * (x.ndim - 3) + (nf,)
    xf = x.astype(jnp.float32)

    def rotate(xa, pos):
        ang = pos.astype(jnp.float32)[:, None] * inv_freq
        cos = jnp.cos(ang).reshape(bshape)
        sin = jnp.sin(ang).reshape(bshape)
        x1, x2 = xa[..., :nf], xa[..., nf:]
        return jnp.concatenate([x1 * cos - x2 * sin, x2 * cos + x1 * sin], -1)

    return jnp.concatenate([rotate(xf[..., :half], row), rotate(xf[..., half:], col)], -1).astype(x.dtype)


def diff_attention(q1, q2, k1, k2, v, lam):
    B, Lq, H, d = q1.shape
    nb = Lq // Q_BLOCK
    scale = d ** -0.5
    qs = jnp.moveaxis(jnp.stack([q1, q2]).reshape(2, B, nb, Q_BLOCK, H, d), 2, 0)

    def block(qq):
        s1 = jnp.einsum('bqhd,bkhd->bhqk', qq[0], k1, preferred_element_type=jnp.float32) * scale
        s2 = jnp.einsum('bqhd,bkhd->bhqk', qq[1], k2, preferred_element_type=jnp.float32) * scale
        p = jax.nn.softmax(s1, -1) - lam * jax.nn.softmax(s2, -1)
        return jnp.einsum('bhqk,bkhv->bqhv', p.astype(v.dtype), v)

    out = lax.map(block, qs)
    return jnp.moveaxis(out, 0, 1).reshape(B, Lq, H, v.shape[-1])


def gated_delta_chunked(q, k, v, beta, g, s0):
    B, H, L, dk = q.shape
    dv = v.shape[-1]
    C = DELTA_CHUNK
    N = L // C
    q = q.reshape(B, H, N, C, dk)
    k = k.reshape(B, H, N, C, dk)
    v = v.reshape(B, H, N, C, dv)
    beta = beta.reshape(B, H, N, C)
    b = jnp.cumsum(g.reshape(B, H, N, C), -1)
    causal = jnp.tril(jnp.ones((C, C), bool))
    strict = jnp.tril(jnp.ones((C, C), bool), -1)
    decay = jnp.where(causal, jnp.exp(jnp.where(causal, b[..., :, None] - b[..., None, :], 0.0)), 0.0)
    kb = k * beta[..., None]
    m = jnp.where(strict, jnp.einsum('bhntd,bhnsd->bhnts', kb, k) * decay, 0.0)
    a = m + jnp.eye(C, dtype=q.dtype)
    rhs = jnp.concatenate([v * beta[..., None], kb * jnp.exp(b)[..., None]], -1)
    sol = lax.linalg.triangular_solve(a, rhs, left_side=True, lower=True, unit_diagonal=True)
    u, w = sol[..., :dv], sol[..., dv:]
    qk = jnp.where(causal, jnp.einsum('bhntd,bhnsd->bhnts', q, k) * decay, 0.0)
    b_last = b[..., -1:]
    q_dec = q * jnp.exp(b)[..., None]
    k_dec = k * jnp.exp(b_last - b)[..., None]
    g_last = jnp.exp(b_last[..., 0])
    xs = tuple(jnp.moveaxis(t, 2, 0) for t in (u, w, qk, q_dec, k_dec, g_last))

    def step(S, inp):
        u_n, w_n, qk_n, qd_n, kd_n, gl_n = inp
        v_new = u_n - jnp.einsum('bhcd,bhde->bhce', w_n, S)
        o = jnp.einsum('bhcd,bhde->bhce', qd_n, S) + jnp.einsum('bhts,bhse->bhte', qk_n, v_new)
        S = S * gl_n[..., None, None] + jnp.einsum('bhcd,bhce->bhde', kd_n, v_new)
        return S, o

    S, o = lax.scan(step, s0, xs)
    return jnp.moveaxis(o, 0, 2).reshape(B, H, L, dv), S


def gla_chunked(q, k, v, g, s0):
    B, H, L, dk = q.shape
    dv = v.shape[-1]
    C = HGRN_CHUNK
    N = L // C
    q = q.reshape(B, H, N, C, dk)
    k = k.reshape(B, H, N, C, dk)
    g = g.reshape(B, H, N, C, dk)
    v = v.reshape(B, H, N, C, dv)
    b = jnp.cumsum(g, axis=3)
    causal = jnp.tril(jnp.ones((C, C), bool))[:, :, None]
    diff = b[:, :, :, :, None, :] - b[:, :, :, None, :, :]
    dec = jnp.where(causal, jnp.exp(jnp.where(causal, diff, 0.0)), 0.0)
    attn = jnp.einsum('bhntsd,bhnsd->bhnts', q[:, :, :, :, None, :] * dec, k)
    intra = jnp.einsum('bhnts,bhnse->bhnte', attn, v)
    b_last = b[:, :, :, -1:, :]
    q_dec = q * jnp.exp(b)
    k_dec = k * jnp.exp(b_last - b)
    g_last = jnp.exp(b_last[:, :, :, 0, :])
    xs = tuple(jnp.moveaxis(t, 2, 0) for t in (intra, q_dec, k_dec, v, g_last))

    def step(S, inp):
        intra_n, qd_n, kd_n, v_n, gl_n = inp
        o = intra_n + jnp.einsum('bhcd,bhde->bhce', qd_n, S)
        S = S * gl_n[..., None] + jnp.einsum('bhcd,bhce->bhde', kd_n, v_n)
        return S, o

    S, o = lax.scan(step, s0, xs)
    return jnp.moveaxis(o, 0, 2).reshape(B, H, L, dv), S


def _flip(t):
    return jnp.flip(t, axis=2)


def delta_mixer(qkv, gate, beta_raw, a_raw, conv_w, a_log, dt_bias, norm_g, s0):
    B, L, _ = qkv.shape
    f32 = jnp.float32
    qkv_c = jax.nn.silu(short_conv(qkv, conv_w)).astype(f32)
    q, k, v = jnp.split(qkv_c, [B_HEADS * B_DK, 2 * B_HEADS * B_DK], axis=-1)
    q = l2norm(q.reshape(B, L, B_HEADS, B_DK).transpose(0, 2, 1, 3)) * B_DK ** -0.5
    k = l2norm(k.reshape(B, L, B_HEADS, B_DK).transpose(0, 2, 1, 3))
    v = v.reshape(B, L, B_HEADS, B_DV).transpose(0, 2, 1, 3)
    beta = jax.nn.sigmoid(beta_raw.astype(f32)).reshape(B, L, 2, B_HEADS).transpose(2, 0, 3, 1)
    a = a_raw.astype(f32).reshape(B, L, 2, B_HEADS).transpose(2, 0, 3, 1)
    g = -jnp.exp(a_log.astype(f32))[:, None, :, None] * jax.nn.softplus(a + dt_bias.astype(f32)[:, None, :, None])
    s0 = s0.astype(f32)
    o_f, s_f = gated_delta_chunked(q, k, v, beta[0], g[0], s0[:, 0])
    o_b, s_b = gated_delta_chunked(_flip(q), _flip(k), _flip(v), _flip(beta[1]), _flip(g[1]), s0[:, 1])
    o = (o_f + _flip(o_b)).transpose(0, 2, 1, 3)
    o = rms_norm(o, norm_g) * jax.nn.silu(gate.astype(f32).reshape(B, L, B_HEADS, B_DV))
    return o.reshape(B, L, B_WIDTH).astype(qkv.dtype), jnp.stack([s_f, s_b], 1)


def hgrn_mixer(q_raw, f_raw, i_raw, gate, lb, norm_g, s0):
    B, L, _ = q_raw.shape
    f32 = jnp.float32
    q = jax.nn.silu(q_raw.astype(f32)).reshape(B, L, C_HEADS, C_DK).transpose(0, 2, 1, 3)
    v = i_raw.astype(f32).reshape(B, L, C_HEADS, C_DV).transpose(0, 2, 1, 3)
    lbb = lb.astype(f32)[:, None, None, :]
    forget = lbb + (1.0 - lbb) * jax.nn.sigmoid(f_raw.astype(f32).reshape(B, L, 2, C_KEYS).transpose(2, 0, 1, 3))
    key = (1.0 - forget).reshape(2, B, L, C_HEADS, C_DK).transpose(0, 1, 3, 2, 4)
    g = jnp.log(forget).reshape(2, B, L, C_HEADS, C_DK).transpose(0, 1, 3, 2, 4)
    s0 = s0.astype(f32)
    o_f, s_f = gla_chunked(q, key[0], v, g[0], s0[:, 0])
    o_b, s_b = gla_chunked(_flip(q), _flip(key[1]), _flip(v), _flip(g[1]), s0[:, 1])
    o = (o_f + _flip(o_b)).transpose(0, 2, 1, 3)
    o = rms_norm(o, norm_g) * jax.nn.silu(gate.astype(f32).reshape(B, L, C_HEADS, C_DV))
    return o.reshape(B, L, C_WIDTH).astype(q_raw.dtype), jnp.stack([s_f, s_b], 1)


def token_mixers(z, conv_w, delta_a_log, delta_dt_bias, delta_norm, lb, hgrn_norm, lam, lam_scale, diff_norm, ctx):
    B, L, _ = z.shape
    aq, ak, av, bqkv, bg, bbeta, ba, cq, cf, ci, cg = jnp.split(z, _split_points(), axis=-1)
    aq = aq.reshape(B, L, A_HEADS, 2, A_QK)
    ak = ak.reshape(B, L, A_HEADS, 2, A_QK)
    v = av.reshape(B, L, A_HEADS, A_V)
    if ctx is None:
        k_all, v_all = ak, v
        s0_d = jnp.zeros((B, 2, B_HEADS, B_DK, B_DV), jnp.float32)
        s0_h = jnp.zeros((B, 2, C_HEADS, C_DK, C_DV), jnp.float32)
    else:
        ctx_k, ctx_v, s0_d, s0_h = ctx
        aq = axial_rope(aq)
        k_lat = axial_rope(ak)
        k_all = jnp.concatenate([k_lat, ctx_k.reshape(B, -1, A_HEADS, 2, A_QK).astype(ak.dtype)], 1)
        v_all = jnp.concatenate([v, ctx_v.astype(v.dtype)], 1)
    o_a = diff_attention(aq[..., 0, :], aq[..., 1, :], k_all[..., 0, :], k_all[..., 1, :], v_all, lam)
    o_a = (rms_norm(o_a, diff_norm) * lam_scale).reshape(B, L, A_WIDTH)
    o_b, s_d = delta_mixer(bqkv, bg, bbeta, ba, conv_w, delta_a_log, delta_dt_bias, delta_norm, s0_d)
    o_c, s_h = hgrn_mixer(cq, cf, ci, cg, lb, hgrn_norm, s0_h)
    o = jnp.concatenate([o_a.astype(z.dtype), o_b, o_c], -1)
    return o, ak.reshape(B, L, A_HEADS, 2 * A_QK), v, s_d, s_h


def trunk_layer(x, mod, w_in, w_out, conv_w, delta_a_log, delta_dt_bias, delta_norm, lb, hgrn_norm,
                lam, lam_scale, diff_norm, ln_g, ln_b, w_ffn_in, w_ffn_out, ctx):
    shift_m, scale_m, gate_m, shift_f, scale_f, gate_f = jnp.split(mod.astype(x.dtype), 6, axis=-1)
    z = (x * (1 + scale_m) + shift_m) @ w_in
    o, k_c, v_c, s_d, s_h = token_mixers(z, conv_w, delta_a_log, delta_dt_bias, delta_norm, lb, hgrn_norm,
                                         lam, lam_scale, diff_norm, ctx)
    x = layer_norm(ALPHA * x + gate_m * (o @ w_out), ln_g[0], ln_b[0])
    gt, up = jnp.split((x * (1 + scale_f) + shift_f) @ w_ffn_in, 2, axis=-1)
    x = layer_norm(ALPHA * x + gate_f * ((jax.nn.silu(gt) * up) @ w_ffn_out), ln_g[1], ln_b[1])
    return x, k_c, v_c, s_d, s_h


def setup_inputs(seed: int = 0) -> dict:
    key = jax.random.key(seed)
    ks = jax.random.split(key, 24)
    f32 = jnp.float32

    def nrm(k, shape, s):
        return jax.random.normal(k, shape, f32) * s

    dt = jnp.exp(jax.random.uniform(ks[13], (DEPTH, 2, B_HEADS), f32, math.log(1e-3), math.log(1e-1)))
    return {
        'x_prompt': nrm(ks[0], (BATCH, SEQ, D_MODEL), 1.0),
        'x_sample': nrm(ks[1], (DEC_BATCH, DEC_SEQ, D_MODEL), 1.0),
        'cache_attn_k': nrm(ks[2], (DEC_BATCH, DEPTH, PAST_LEN, A_HEADS, 2 * A_QK), 1.0),
        'cache_attn_v': nrm(ks[3], (DEC_BATCH, DEPTH, PAST_LEN, A_HEADS, A_V), 1.0),
        'state_delta': nrm(ks[4], (DEC_BATCH, DEPTH, 2, B_HEADS, B_DK, B_DV), 0.1),
        'state_hgrn': nrm(ks[5], (DEC_BATCH, DEPTH, 2, C_HEADS, C_DK, C_DV), 0.5),
        'c': nrm(ks[6], (DEC_BATCH, D_MODEL), 1.0),
        'c_ctx': nrm(ks[7], (D_MODEL,), 1.0),
        'w_mod': nrm(ks[8], (DEPTH, D_MODEL, 6 * D_MODEL), D_MODEL ** -0.5),
        'b_mod': nrm(ks[9], (DEPTH, 6 * D_MODEL), 0.02),
        'w_in': nrm(ks[10], (DEPTH, D_MODEL, D_IN), D_MODEL ** -0.5),
        'conv_w': nrm(ks[11], (DEPTH, SHORT_CONV, B_QKV), SHORT_CONV ** -0.5),
        'delta_a_log': jnp.log(jax.random.uniform(ks[12], (DEPTH, 2, B_HEADS), f32, 1.0, 16.0)),
        'delta_dt_bias': dt + jnp.log(-jnp.expm1(-dt)),
        'delta_norm': 1.0 + nrm(ks[14], (DEPTH, B_DV), 0.02),
        'hgrn_lb': 1.0 + nrm(ks[15], (2, DEPTH, C_KEYS), 0.1),
        'hgrn_norm': 1.0 + nrm(ks[16], (DEPTH, C_DV), 0.02),
        'diff_lambda': nrm(ks[17], (DEPTH, 4, A_QK), 0.1),
        'diff_norm': 1.0 + nrm(ks[18], (DEPTH, A_V), 0.02),
        'w_out': nrm(ks[19], (DEPTH, D_MIX, D_MODEL), BETA_INIT * D_MIX ** -0.5),
        'ln_g': 1.0 + nrm(ks[20], (DEPTH, 2, D_MODEL), 0.02),
        'ln_b': nrm(ks[21], (DEPTH, 2, D_MODEL), 0.02),
        'w_ffn_in': nrm(ks[22], (DEPTH, D_MODEL, 2 * D_FF), D_MODEL ** -0.5),
        'w_ffn_out': nrm(ks[23], (DEPTH, D_FF, D_MODEL), BETA_INIT * D_FF ** -0.5),
    }


def reference(x_prompt, x_sample, cache_attn_k, cache_attn_v, state_delta, state_hgrn, c, c_ctx,
              w_mod, b_mod, w_in, conv_w, delta_a_log, delta_dt_bias, delta_norm, hgrn_lb, hgrn_norm,
              diff_lambda, diff_norm, w_out, ln_g, ln_b, w_ffn_in, w_ffn_out):
    f32 = jnp.float32
    lb_soft = jax.nn.softmax(hgrn_lb.astype(f32), axis=1)
    lb_all = jnp.cumsum(lb_soft, axis=1) - lb_soft[:, :1]
    c_silu = jax.nn.silu(c)
    cctx_silu = jax.nn.silu(c_ctx)
    xp, xs = x_prompt, x_sample
    new_k, new_v, new_sd, new_sh = [], [], [], []
    for l in range(DEPTH):
        lam_init = 0.8 - 0.6 * math.exp(-0.3 * l)
        dl = diff_lambda[l].astype(f32)
        lam = jnp.exp(jnp.sum(dl[0] * dl[1])) - jnp.exp(jnp.sum(dl[2] * dl[3])) + lam_init
        mod_p = (cctx_silu @ w_mod[l] + b_mod[l])[None, None, :]
        mod_s = (c_silu @ w_mod[l] + b_mod[l])[:, None, :]
        shared = (w_in[l], w_out[l], conv_w[l], delta_a_log[l], delta_dt_bias[l], delta_norm[l], lb_all[:, l],
                  hgrn_norm[l], lam, 1.0 - lam_init, diff_norm[l], ln_g[l], ln_b[l], w_ffn_in[l], w_ffn_out[l])
        xp, k_c, v_c, s_d, s_h = trunk_layer(xp, mod_p, *shared, None)
        xs, _, _, _, _ = trunk_layer(xs, mod_s, *shared,
                                     (cache_attn_k[:, l], cache_attn_v[:, l], state_delta[:, l], state_hgrn[:, l]))
        new_k.append(k_c)
        new_v.append(v_c)
        new_sd.append(s_d)
        new_sh.append(s_h)
    return (xp, xs, jnp.stack(new_k, 1), jnp.stack(new_v, 1), jnp.stack(new_sd, 1), jnp.stack(new_sh, 1))
```

```python
import functools
import math

import numpy as np
import jax
import jax.numpy as jnp
from jax import lax
from jax.experimental import pallas as pl
from jax.experimental.pallas import tpu as pltpu

F32 = jnp.float32
BF16 = jnp.bfloat16

A_HEADS = 4
A_QK = 64
A_V = 2 * A_QK
B_HEADS = 4
B_DK = 64
B_DV = 64
C_HEADS = 4
C_DK = 64
C_DV = 64
GRID_W = 64
ROPE_BASE = 10000.0
LN_EPS = 1e-5
RMS_EPS = 1e-6
L2_EPS = 1e-6

LANES = 128
TOK = 128
DELTA_CHUNK = 64
VMEM_LIMIT = 60 * 1024 * 1024


def _cparams(*sem):
    return pltpu.CompilerParams(dimension_semantics=sem, vmem_limit_bytes=VMEM_LIMIT)


def _dot(a, b):
    return jnp.dot(a, b, preferred_element_type=F32)


def _dot_nt(a, b):
    return lax.dot_general(a, b, (((1,), (1,)), ((), ())), preferred_element_type=F32)


def _bdot(a, b):
    return _dot(a.astype(BF16), b.astype(BF16))


def _bdot_nt(a, b):
    return _dot_nt(a.astype(BF16), b.astype(BF16))


def _split3(x):
    hi = x.astype(BF16)
    r = x - hi.astype(F32)
    mid = r.astype(BF16)
    lo = (r - mid.astype(F32)).astype(BF16)
    return hi, mid, lo


def _dot01(w01, x):
    hi, mid, lo = _split3(x)
    return (_dot(w01, lo) + _dot(w01, mid)) + _dot(w01, hi)


def _dot01_nt(x, w01):
    hi, mid, lo = _split3(x)
    return (_dot_nt(lo, w01) + _dot_nt(mid, w01)) + _dot_nt(hi, w01)


def _sigmoid(x):
    return 1.0 / (1.0 + jnp.exp(-x))


def _silu(x):
    return x * _sigmoid(x)


def _softplus(x):
    return jnp.maximum(x, 0.0) + jnp.log(1.0 + jnp.exp(-jnp.abs(x)))


def _lane_iota(shape):
    return lax.broadcasted_iota(jnp.int32, shape, len(shape) - 1)


def _row_iota(shape):
    return lax.broadcasted_iota(jnp.int32, shape, len(shape) - 2)


def _layer_norm(x, g, b):
    mu = jnp.mean(x, axis=-1, keepdims=True)
    xc = x - mu
    var = jnp.mean(xc * xc, axis=-1, keepdims=True)
    return xc * lax.rsqrt(var + LN_EPS) * g + b


def _mod_kernel(c_ref, w_ref, b_ref, o_ref):
    cs = _silu(c_ref[...])
    o_ref[0] = _bdot(cs, w_ref[0]) + b_ref[0]


def _modulation(cond, w_mod, b_mod):
    depth, d, n = w_mod.shape
    rows = cond.shape[0]
    tn = 1536 if n % 1536 == 0 else n
    return pl.pallas_call(
        _mod_kernel,
        grid=(depth, n // tn),
        in_specs=[
            pl.BlockSpec((rows, d), lambda l, j: (0, 0)),
            pl.BlockSpec((1, d, tn), lambda l, j: (l, 0, j)),
            pl.BlockSpec((1, 1, tn), lambda l, j: (l, 0, j)),
        ],
        out_specs=pl.BlockSpec((1, rows, tn), lambda l, j: (l, 0, j)),
        out_shape=jax.ShapeDtypeStruct((depth, rows, n), F32),
        compiler_params=_cparams("parallel", "parallel"),
        name="modulation",
    )(cond, w_mod, b_mod.reshape(depth, 1, n))


N_A = 3 * A_HEADS * A_V
N_B = 4 * B_HEADS * B_DK
N_S = 2 * LANES
N_C = 5 * C_HEADS * C_DK
N_Z = N_A + N_B + N_S + N_C


def _in_proj_kernel(x_ref, mod_ref, w_ref, *rest, d_model, rope):
    if rope:
        rc_ref, rs1_ref, rs2_ref, za_ref, zb_ref, zs_ref, zc_ref = rest
    else:
        za_ref, zb_ref, zs_ref, zc_ref = rest
    m = mod_ref[0]
    shift = m[:, 0:d_model]
    scale = m[:, d_model:2 * d_model]
    h = (x_ref[...] * (1.0 + scale) + shift).astype(BF16)
    if rope:
        rc, rs1, rs2 = rc_ref[...], rs1_ref[...], rs2_ref[...]
        n_rot = 2 * A_HEADS
        for g in range(N_A // LANES):
            blk = _dot(h, w_ref[:, g * LANES:(g + 1) * LANES])
            if g < n_rot:
                up = pltpu.roll(blk, LANES - 16, axis=1)
                dn = pltpu.roll(blk, 16, axis=1)
                blk = blk * rc + up * rs1 + dn * rs2
            za_ref[:, g * LANES:(g + 1) * LANES] = blk
    else:
        za_ref[...] = _dot(h, w_ref[:, 0:N_A])
    zb_ref[...] = _dot(h, w_ref[:, N_A:N_A + N_B])
    zs_ref[...] = _dot(h, w_ref[:, N_A + N_B:N_A + N_B + N_S])
    zc_ref[...] = _dot(h, w_ref[:, N_A + N_B + N_S:N_Z])


def _in_proj(x, mod, w, seq_len, rope_tabs):
    n, d = x.shape
    tm = min(512, seq_len)
    tiles_per_seq = seq_len // tm
    rows_per_mod = n // mod.shape[0]
    rope = rope_tabs is not None
    in_specs = [
        pl.BlockSpec((tm, d), lambda i: (i, 0)),
        pl.BlockSpec((1, 1, mod.shape[2]), lambda i: ((i * tm) // rows_per_mod, 0, 0)),
        pl.BlockSpec((d, N_Z), lambda i: (0, 0)),
    ]
    args = [x, mod, w]
    if rope:
        in_specs += [pl.BlockSpec((tm, LANES), lambda i: (i % tiles_per_seq, 0))] * 3
        args += list(rope_tabs)
    outs = [N_A, N_B, N_S, N_C]
    return pl.pallas_call(
        functools.partial(_in_proj_kernel, d_model=d, rope=rope),
        grid=(n // tm,),
        in_specs=in_specs,
        out_specs=[pl.BlockSpec((tm, w_), lambda i: (i, 0)) for w_ in outs],
        out_shape=[jax.ShapeDtypeStruct((n, w_), F32) for w_ in outs],
        compiler_params=_cparams("parallel"),
        name="in_proj_rope" if rope else "in_proj",
    )(*args)


def _rope_tables(seq_len):
    half = A_QK // 2
    nf = half // 2
    pos = jnp.arange(seq_len)
    row = (pos // GRID_W).astype(F32)
    col = (pos % GRID_W).astype(F32)
    inv_freq = ROPE_BASE ** (-jnp.arange(nf, dtype=F32) / nf)
    ang_r = row[:, None] * inv_freq
    ang_c = col[:, None] * inv_freq
    cos64 = jnp.concatenate([jnp.cos(ang_r), jnp.cos(ang_r), jnp.cos(ang_c), jnp.cos(ang_c)], -1)
    zero = jnp.zeros_like(ang_r)
    s1_64 = jnp.concatenate([-jnp.sin(ang_r), zero, -jnp.sin(ang_c), zero], -1)
    s2_64 = jnp.concatenate([zero, jnp.sin(ang_r), zero, jnp.sin(ang_c)], -1)
    rep = LANES // A_QK
    return tuple(jnp.tile(t, (1, rep)) for t in (cos64, s1_64, s2_64))


def _attn_kernel(q_ref, k_ref, v_ref, *rest, lam_init, has_ctx, tq):
    if has_ctx:
        ck_ref, cv_ref, dl_ref, nrm_ref, o_ref = rest
    else:
        dl_ref, nrm_ref, o_ref = rest
    dl = dl_ref[0]
    lam = (jnp.exp(jnp.sum(dl[0:1] * dl[1:2], axis=1, keepdims=True))
           - jnp.exp(jnp.sum(dl[2:3] * dl[3:4], axis=1, keepdims=True)) + lam_init)
    q = q_ref[...] * (A_QK ** -0.5)
    lane = _lane_iota(q.shape)
    qq = jnp.concatenate([jnp.where(lane < A_QK, q, 0.0), jnp.where(lane >= A_QK, q, 0.0)], axis=0).astype(BF16)
    s = _dot_nt(qq, k_ref[...].astype(BF16))
    m = jnp.max(s, axis=-1, keepdims=True)
    if has_ctx:
        sc = _dot_nt(qq, ck_ref[0, 0].astype(BF16))
        m = jnp.maximum(m, jnp.max(sc, axis=-1, keepdims=True))
    e = jnp.exp(s - m)
    den = jnp.sum(e, axis=-1, keepdims=True)
    if has_ctx:
        ec = jnp.exp(sc - m)
        den = den + jnp.sum(ec, axis=-1, keepdims=True)
    r = 1.0 / den
    r1 = r[:tq]
    r2 = lam * r[tq:]
    p = e[:tq] * r1 - e[tq:] * r2
    o = _dot(p.astype(BF16), v_ref[...].astype(BF16))
    if has_ctx:
        pc = ec[:tq] * r1 - ec[tq:] * r2
        o = o + _dot(pc.astype(BF16), cv_ref[0, 0].astype(BF16))
    o = o * lax.rsqrt(jnp.mean(o * o, axis=-1, keepdims=True) + RMS_EPS) * nrm_ref[0]
    o_ref[...] = o * (1.0 - lam_init)


def _attention(za, seq_len, layer, diff_lambda, diff_norm, ctx_kv):
    n = za.shape[0]
    n_seq = n // seq_len
    tq = min(128, seq_len)
    nq = seq_len // tq
    lam_init = 0.8 - 0.6 * math.exp(-0.3 * layer)
    has_ctx = ctx_kv is not None
    in_specs = [
        pl.BlockSpec((tq, LANES), lambda b, h, i: (b * nq + i, h)),
        pl.BlockSpec((seq_len, LANES), lambda b, h, i: (b, A_HEADS + h)),
        pl.BlockSpec((seq_len, LANES), lambda b, h, i: (b, 2 * A_HEADS + h)),
    ]
    args = [za, za, za]
    if has_ctx:
        ck, cv = ctx_kv
        past = ck.shape[2]
        spec = pl.BlockSpec((1, 1, past, LANES), lambda b, h, i: (b, layer, 0, h))
        in_specs += [spec, spec]
        args += [ck, cv]
    in_specs += [
        pl.BlockSpec((1, 4, A_QK), lambda b, h, i: (layer, 0, 0)),
        pl.BlockSpec((1, 1, A_V), lambda b, h, i: (layer, 0, 0)),
    ]
    args += [diff_lambda, diff_norm.reshape(diff_norm.shape[0], 1, A_V)]
    return pl.pallas_call(
        functools.partial(_attn_kernel, lam_init=lam_init, has_ctx=has_ctx, tq=tq),
        grid=(n_seq, A_HEADS, nq),
        in_specs=in_specs,
        out_specs=pl.BlockSpec((tq, LANES), lambda b, h, i: (b * nq + i, h)),
        out_shape=jax.ShapeDtypeStruct((n, A_HEADS * A_V), F32),
        compiler_params=_cparams("parallel", "parallel", "arbitrary"),
        name="diff_attn_ctx" if has_ctx else "diff_attn",
    )(*args)


def _head_sum(x):
    lane = _lane_iota(x.shape)
    lo = lane < 64
    s0 = jnp.sum(jnp.where(lo, x, 0.0), axis=-1, keepdims=True)
    s1 = jnp.sum(jnp.where(lo, 0.0, x), axis=-1, keepdims=True)
    return jnp.where(lo, s0, s1)


def _pair_blockdiag(a, b):
    z = jnp.zeros_like(a)
    return jnp.concatenate([jnp.concatenate([a, z], axis=1), jnp.concatenate([z, b], axis=1)], axis=0)


def _gated_norm_store(oacc_s, gt_ref, norm_row, o_ref, n_groups):
    def body(j, carry):
        r0 = pl.multiple_of(j * TOK, TOK)
        o = oacc_s[pl.ds(r0, TOK), :]
        ms = _head_sum(o * o) * (1.0 / 64.0)
        o = o * lax.rsqrt(ms + RMS_EPS) * norm_row
        o_ref[pl.ds(r0, TOK), :] = o * _silu(gt_ref[pl.ds(r0, TOK), :])
        return carry
    lax.fori_loop(0, n_groups, body, 0)


def _delta_consts():
    t = np.arange(TOK)[:, None]
    s = np.arange(TOK)[None, :]

    def same(n):
        return (t // n) == (s // n)

    masks, cums = [], []
    for d in range(2):
        before = (s <= t) if d == 0 else (s >= t)
        strict = (s < t) if d == 0 else (s > t)
        causal = same(DELTA_CHUNK) & before
        st = same(DELTA_CHUNK) & strict
        masks.append(np.stack([
            causal, st, st & same(8), st & same(16) & ~same(8), st & same(32) & ~same(16),
            st & same(64) & ~same(32), t == s, (t // 64) == (s // 64)]))
        cums.append(np.concatenate([causal, same(DELTA_CHUNK) & ~before, same(DELTA_CHUNK)], 0))
    return (jnp.asarray(np.stack(masks), F32), jnp.asarray(np.stack(cums), BF16))


def _pdot(a, b, passes):
    if passes == 1:
        return _bdot(a, b)
    ah = a.astype(BF16)
    al = (a - ah.astype(F32)).astype(BF16)
    bh = b.astype(BF16)
    bl = (b - bh.astype(F32)).astype(BF16)
    return (_dot(al, bh) + _dot(ah, bl)) + _dot(ah, bh)


def _unit_tri_inverse(m, msk, passes):
    t1 = -(m * msk[2])
    x = msk[6] + t1
    t2 = _pdot(t1, t1, passes)
    x = x + _pdot(x, t2, passes)
    t4 = _pdot(t2, t2, passes)
    x = x + _pdot(x, t4, passes)
    for lvl in (3, 4, 5):
        y = _pdot(m * msk[lvl], x, passes)
        x = x - _pdot(x, y, passes)
    return x


def _delta_kernel(q_ref, k_ref, v_ref, gt_ref, sc_ref, sr_ref, cwq_ref, cwk_ref, cwv_ref, prow_ref, pcol_ref,
                  msk_ref, cum_ref, *rest, seq_len, has_s0, passes):
    if has_s0:
        s0_ref, o_ref, sout_ref, u_s, w_s, qd_s, qkm_s, kdt_s, gl_s, oacc_s = rest
    else:
        o_ref, sout_ref, u_s, w_s, qd_s, qkm_s, kdt_s, gl_s, oacc_s = rest
    n_groups = seq_len // TOK
    prow = prow_ref[0, 0]
    pcol = pcol_ref[0, 0]

    def conv_silu(ref, cw_ref, j, r0):
        cw = cw_ref[0]
        x = ref[pl.ds(r0, TOK), :]
        rp = pl.multiple_of(jnp.maximum(r0 - 8, 0), 8)
        rn = pl.multiple_of(jnp.minimum(r0 + TOK, seq_len - 8), 8)
        prev = jnp.where(j > 0, ref[pl.ds(rp, 8), :][7:8], 0.0)
        nxt = jnp.where(j < n_groups - 1, ref[pl.ds(rn, 8), :][0:1], 0.0)
        row = _row_iota(x.shape)
        dn = jnp.where(row == 0, prev, pltpu.roll(x, 1, axis=0))
        up = jnp.where(row == TOK - 1, nxt, pltpu.roll(x, TOK - 1, axis=0))
        return _silu(cw[0:1] * dn + cw[1:2] * x + cw[2:3] * up)

    def prep(j, carry):
        r0 = pl.multiple_of(j * TOK, TOK)
        qs = conv_silu(q_ref, cwq_ref, j, r0)
        ks = conv_silu(k_ref, cwk_ref, j, r0)
        v = conv_silu(v_ref, cwv_ref, j, r0)
        q = qs * lax.rsqrt(_head_sum(qs * qs) + L2_EPS) * (B_DK ** -0.5)
        k = ks * lax.rsqrt(_head_sum(ks * ks) + L2_EPS)
        lane = _lane_iota(q.shape)
        lo = lane < 64
        kb16 = k.astype(BF16)
        k_h = (jnp.where(lo, k, 0.0).astype(BF16), jnp.where(lo, 0.0, k).astype(BF16))
        q_h = (jnp.where(lo, q, 0.0).astype(BF16), jnp.where(lo, 0.0, q).astype(BF16))
        kk = [_dot_nt(k_h[hh], kb16) for hh in range(2)]
        qk = [_dot_nt(q_h[hh], kb16) for hh in range(2)]
        xs = sc_ref[pl.ds(r0, TOK), :]
        gcol = jnp.where(lane < 4, _sigmoid(xs), -jnp.exp(prow[0:1]) * _softplus(xs + prow[1:2]))
        xr = sr_ref[0, j]
        rowi = _row_iota(xr.shape)
        grow = jnp.where(rowi < 4, _sigmoid(xr), -jnp.exp(pcol[:, 0:1]) * _softplus(xr + pcol[:, 1:2]))
        for d in range(2):
            msk = msk_ref[d]
            cs = _dot01(cum_ref[d], gcol)
            br = _dot01_nt(grow, cum_ref[d, 0:TOK])
            sol = None
            qkm, eq, ek, gl = [], [], [], []
            for hh in range(2):
                cb, cg = d * 2 + hh, 4 + d * 2 + hh
                beta = gcol[:, cb:cb + 1]
                bcol = cs[0:TOK, cg:cg + 1]
                dec = jnp.exp(jnp.minimum(bcol - br[cg:cg + 1, :], 0.0)) * msk[0]
                x = _unit_tri_inverse(kk[hh] * beta * dec * msk[1], msk, passes)
                eb = jnp.exp(bcol)
                sel = lo if hh == 0 else jnp.logical_not(lo)
                rhs = jnp.concatenate([jnp.where(sel, v * beta, 0.0), jnp.where(sel, k * (beta * eb), 0.0)], axis=1)
                part = _pdot(x, rhs, passes)
                sol = part if sol is None else sol + part
                qkm.append((qk[hh] * dec).astype(BF16))
                eq.append(eb)
                ek.append(jnp.exp(cs[TOK:2 * TOK, cg:cg + 1]))
                gl.append(jnp.exp(cs[2 * TOK:3 * TOK, cg:cg + 1]))
            u_s[d, pl.ds(r0, TOK), :] = sol[:, 0:LANES]
            w_s[d, pl.ds(r0, TOK), :] = sol[:, LANES:2 * LANES].astype(BF16)
            qd_s[d, pl.ds(r0, TOK), :] = (q * jnp.where(lo, eq[0], eq[1])).astype(BF16)
            qkm_s[d, pl.ds(r0, TOK), :] = jnp.concatenate(qkm, axis=1)
            kd = k * jnp.where(lo, ek[0], ek[1])
            kdt_s[d, pl.ds(r0, TOK), :] = kd.T.astype(BF16)
            glf = jnp.where(lo, gl[0], gl[1])
            gl_s[d, pl.ds(pl.multiple_of(j * 16, 16), 16), :] = jnp.concatenate([glf[0:8], glf[64:72]], axis=0)
        oacc_s[pl.ds(r0, TOK), :] = jnp.zeros((TOK, LANES), F32)
        return carry

    lax.fori_loop(0, n_groups, prep, 0)

    bd = msk_ref[0, 7]
    lane64 = _lane_iota((DELTA_CHUNK, LANES)) < 64
    zpad = jnp.zeros((DELTA_CHUNK, LANES), BF16)

    def chunk_step(d, s, j, c):
        r0 = pl.multiple_of(j * TOK, TOK)
        rc = pl.multiple_of(j * TOK + c * DELTA_CHUNK, DELTA_CHUNK)
        sb = s.astype(BF16)
        vn = u_s[d, pl.ds(rc, DELTA_CHUNK), :] - _dot(w_s[d, pl.ds(rc, DELTA_CHUNK), :], sb)
        v0 = jnp.where(lane64, vn, 0.0).astype(BF16)
        v1 = jnp.where(lane64, 0.0, vn).astype(BF16)
        vnb = vn.astype(BF16)
        if c == 0:
            vext = jnp.concatenate([vnb, zpad], axis=0)
            v2 = jnp.concatenate([v0, zpad, v1, zpad], axis=0)
        else:
            vext = jnp.concatenate([zpad, vnb], axis=0)
            v2 = jnp.concatenate([zpad, v0, zpad, v1], axis=0)
        o = _dot(qd_s[d, pl.ds(rc, DELTA_CHUNK), :], sb) + _dot(qkm_s[d, pl.ds(rc, DELTA_CHUNK), :], v2)
        glr = gl_s[d, pl.ds(pl.multiple_of(j * 16 + c * 8, 8), 8), :][0:1]
        s = s * glr + _dot(kdt_s[d, pl.ds(r0, TOK), :], vext) * bd
        oacc_s[pl.ds(rc, DELTA_CHUNK), :] += o
        return s

    def scan(i, carry):
        sf, sb_ = carry
        sf = chunk_step(0, sf, i, 0)
        sf = chunk_step(0, sf, i, 1)
        jb = n_groups - 1 - i
        sb_ = chunk_step(1, sb_, jb, 1)
        sb_ = chunk_step(1, sb_, jb, 0)
        return sf, sb_

    if has_s0:
        init = tuple(_pair_blockdiag(s0_ref[0, 0, d, 0], s0_ref[0, 0, d, 1]) for d in range(2))
    else:
        init = (jnp.zeros((LANES, LANES), F32),) * 2
    fin = lax.fori_loop(0, n_groups, scan, init)
    for d in range(2):
        sout_ref[0, d, 0] = fin[d][0:64, 0:64]
        sout_ref[0, d, 1] = fin[d][64:128, 64:128]
    _gated_norm_store(oacc_s, gt_ref, prow[2:3], o_ref, n_groups)


def _delta_mixer(zb, zs, seq_len, layer, conv_w, prow, pcol, consts, s0, passes):
    n = zb.shape[0]
    n_seq = n // seq_len
    n_groups = seq_len // TOK
    pairs = B_HEADS // 2
    has_s0 = s0 is not None
    zs_rows = zs.reshape(n // TOK, TOK, pairs, LANES)[..., :8].transpose(2, 0, 3, 1)
    msk, cum = consts
    blk = lambda c0: pl.BlockSpec((seq_len, LANES), lambda b, p, c0=c0: (b, c0 + p))
    cwb = lambda c0: pl.BlockSpec((1, 3, LANES), lambda b, p, c0=c0: (layer, 0, c0 + p))
    in_specs = [
        blk(0), blk(pairs), blk(2 * pairs), blk(3 * pairs),
        pl.BlockSpec((seq_len, LANES), lambda b, p: (b, p)),
        pl.BlockSpec((1, n_groups, 8, TOK), lambda b, p: (p, b, 0, 0)),
        cwb(0), cwb(pairs), cwb(2 * pairs),
        pl.BlockSpec((1, 1, 8, LANES), lambda b, p: (layer, p, 0, 0)),
        pl.BlockSpec((1, 1, 8, LANES), lambda b, p: (layer, p, 0, 0)),
        pl.BlockSpec(msk.shape, lambda b, p: (0, 0, 0, 0)),
        pl.BlockSpec(cum.shape, lambda b, p: (0, 0, 0)),
    ]
    args = [zb, zb, zb, zb, zs, zs_rows, conv_w, conv_w, conv_w, prow, pcol, msk, cum]
    if has_s0:
        in_specs.append(pl.BlockSpec((1, 1, 2, 2, B_DK, B_DV), lambda b, p: (b, layer, 0, p, 0, 0)))
        args.append(s0)
    return pl.pallas_call(
        functools.partial(_delta_kernel, seq_len=seq_len, has_s0=has_s0, passes=passes),
        grid=(n_seq, pairs),
        in_specs=in_specs,
        out_specs=[
            pl.BlockSpec((seq_len, LANES), lambda b, p: (b, p)),
            pl.BlockSpec((1, 2, 2, B_DK, B_DV), lambda b, p: (b, 0, p, 0, 0)),
        ],
        out_shape=[
            jax.ShapeDtypeStruct((n, B_HEADS * B_DV), F32),
            jax.ShapeDtypeStruct((n_seq, 2, B_HEADS, B_DK, B_DV), F32),
        ],
        scratch_shapes=[
            pltpu.VMEM((2, seq_len, LANES), F32),
            pltpu.VMEM((2, seq_len, LANES), BF16),
            pltpu.VMEM((2, seq_len, LANES), BF16),
            pltpu.VMEM((2, seq_len, 2 * LANES), BF16),
            pltpu.VMEM((2, seq_len, LANES), BF16),
            pltpu.VMEM((2, n_groups * 16, LANES), F32),
            pltpu.VMEM((seq_len, LANES), F32),
        ],
        compiler_params=_cparams("parallel", "parallel"),
        name="delta_s0" if has_s0 else "delta",
    )(*args)


HGRN_LEVELS = 7
HGRN_WROWS = (HGRN_LEVELS + 2) * TOK + 8


def _hgrn_consts():
    t = np.arange(TOK)
    masks = [t[:, None] == t[None, :]]
    for lvl in range(1, HGRN_LEVELS + 1):
        n = 1 << lvl
        masks.append((t[:, None] // n) == (t[None, :] // n))
    ws = []
    for d in range(2):
        tau = t if d == 0 else TOK - 1 - t
        tt, ti = tau[:, None], tau[None, :]
        blocks = []
        for lvl in range(1, HGRN_LEVELS + 1):
            n = 1 << lvl
            piv = (tau - tau % n + n // 2 - 1)[:, None]
            upper = ((tau % n) >= n // 2)[:, None]
            blocks.append(np.where(upper, (ti > piv) & (ti <= tt), (ti > tt) & (ti <= piv)))
        blocks.append(ti <= tt)
        blocks.append(ti > tt)
        blocks.append(np.ones((8, TOK), bool))
        ws.append(np.concatenate(blocks, 0))
    return jnp.asarray(np.stack(masks), F32), jnp.asarray(np.stack(ws), BF16)


def _hgrn_kernel(q_ref, ff_ref, fb_ref, i_ref, gt_ref, lb_ref, nrm_ref, msk_ref, w_ref, *rest,
                 seq_len, layer, has_s0):
    if has_s0:
        s0_ref, o_ref, sout_ref, oacc_s = rest
    else:
        o_ref, sout_ref, oacc_s = rest
    n_groups = seq_len // TOK
    lbs = []
    for d in range(2):
        x = lb_ref[0, d]
        e = jnp.exp(x - jnp.max(x, axis=0, keepdims=True))
        sm = e / jnp.sum(e, axis=0, keepdims=True)
        if layer > 0:
            lbs.append(jnp.sum(sm[1:layer + 1], axis=0, keepdims=True))
        else:
            lbs.append(jnp.zeros((1, LANES), F32))

    oacc_s[...] = jnp.zeros(oacc_s.shape, F32)
    bd = msk_ref[6]

    def stack_heads(x, lo):
        return jnp.concatenate([jnp.where(lo, x, 0.0), jnp.where(lo, 0.0, x)], axis=0).astype(BF16)

    def tile_step(d, st, j):
        r0 = pl.multiple_of(j * TOK, TOK)
        q = _silu(q_ref[pl.ds(r0, TOK), :])
        v = i_ref[pl.ds(r0, TOK), :]
        f = (ff_ref if d == 0 else fb_ref)[pl.ds(r0, TOK), :]
        lbv = lbs[d]
        forget = lbv + (1.0 - lbv) * _sigmoid(f)
        key = 1.0 - forget
        ex = _dot01(w_ref[d], jnp.log(forget))
        row = _row_iota((TOK, LANES))
        tau = row if d == 0 else TOK - 1 - row
        lo = _lane_iota((TOK, LANES)) < 64
        m0 = msk_ref[0]
        acc = _dot_nt(stack_heads(q, lo), key.astype(BF16)) * jnp.concatenate([m0, m0], axis=0)
        for lvl in range(1, HGRN_LEVELS + 1):
            e = jnp.exp(ex[(lvl - 1) * TOK:lvl * TOK])
            up = (tau & (1 << (lvl - 1))) != 0
            qt = jnp.where(up, q * e, 0.0)
            kt = jnp.where(up, 0.0, key * e)
            a = _dot_nt(stack_heads(qt, lo), kt.astype(BF16))
            if lvl < HGRN_LEVELS:
                ml = msk_ref[lvl]
                a = a * jnp.concatenate([ml, ml], axis=0)
            acc = acc + a
        intra = (_dot(acc[0:TOK].astype(BF16), jnp.where(lo, v, 0.0).astype(BF16))
                 + _dot(acc[TOK:2 * TOK].astype(BF16), jnp.where(lo, 0.0, v).astype(BF16)))
        base = HGRN_LEVELS * TOK
        qd = (q * jnp.exp(ex[base:base + TOK])).astype(BF16)
        kd = (key * jnp.exp(ex[base + TOK:base + 2 * TOK])).astype(BF16)
        gl = jnp.exp(ex[base + 2 * TOK:base + 2 * TOK + 1])
        o = intra + _dot_nt(qd, st.astype(BF16))
        st = st * gl + _dot(v.T.astype(BF16), kd) * bd
        oacc_s[pl.ds(r0, TOK), :] += o
        return st

    def scan(i, carry):
        return tile_step(0, carry[0], i), tile_step(1, carry[1], n_groups - 1 - i)

    if has_s0:
        init = tuple(_pair_blockdiag(s0_ref[0, 0, d, 0], s0_ref[0, 0, d, 1]) for d in range(2))
    else:
        init = (jnp.zeros((LANES, LANES), F32),) * 2
    fin = lax.fori_loop(0, n_groups, scan, init)
    for d in range(2):
        sout_ref[0, d, 0] = fin[d][0:64, 0:64]
        sout_ref[0, d, 1] = fin[d][64:128, 64:128]
    _gated_norm_store(oacc_s, gt_ref, nrm_ref[0], o_ref, n_groups)


def _hgrn_mixer(zc, seq_len, layer, lb, nrm, consts, s0t):
    n = zc.shape[0]
    n_seq = n // seq_len
    pairs = C_HEADS // 2
    has_s0 = s0t is not None
    msk, wst = consts
    blk = lambda c0: pl.BlockSpec((seq_len, LANES), lambda b, p, c0=c0: (b, c0 + p))
    in_specs = [
        blk(0), blk(pairs), blk(2 * pairs), blk(3 * pairs), blk(4 * pairs),
        pl.BlockSpec((1,) + lb.shape[1:], lambda b, p: (p, 0, 0, 0)),
        pl.BlockSpec((1, 1, LANES), lambda b, p: (layer, 0, 0)),
        pl.BlockSpec(msk.shape, lambda b, p: (0, 0, 0)),
        pl.BlockSpec(wst.shape, lambda b, p: (0, 0, 0)),
    ]
    args = [zc, zc, zc, zc, zc, lb, nrm, msk, wst]
    if has_s0:
        in_specs.append(pl.BlockSpec((1, 1, 2, 2, C_DV, C_DK), lambda b, p: (b, layer, 0, p, 0, 0)))
        args.append(s0t)
    return pl.pallas_call(
        functools.partial(_hgrn_kernel, seq_len=seq_len, layer=layer, has_s0=has_s0),
        grid=(n_seq, pairs),
        in_specs=in_specs,
        out_specs=[
            pl.BlockSpec((seq_len, LANES), lambda b, p: (b, p)),
            pl.BlockSpec((1, 2, 2, C_DV, C_DK), lambda b, p: (b, 0, p, 0, 0)),
        ],
        out_shape=[
            jax.ShapeDtypeStruct((n, C_HEADS * C_DV), F32),
            jax.ShapeDtypeStruct((n_seq, 2, C_HEADS, C_DV, C_DK), F32),
        ],
        scratch_shapes=[pltpu.VMEM((seq_len, LANES), F32)],
        compiler_params=_cparams("parallel", "parallel"),
        name="hgrn_s0" if has_s0 else "hgrn",
    )(*args)


FFN_CHUNK = 256


def _tail_kernel(x_ref, oa_ref, ob_ref, oc_ref, mod_ref, wo_ref, lng_ref, lnb_ref, wg_ref, wu_ref, wd_ref,
                 o_ref, x1_s, h_s, acc_s, *, d_model, alpha, n_k):
    kk = pl.program_id(1)
    m = mod_ref[0]
    na, nb = oa_ref.shape[1], ob_ref.shape[1]

    @pl.when(kk == 0)
    def _():
        y = (_bdot(oa_ref[...], wo_ref[0:na]) + _bdot(ob_ref[...], wo_ref[na:na + nb])
             + _bdot(oc_ref[...], wo_ref[na + nb:]))
        x1 = _layer_norm(alpha * x_ref[...] + m[:, 2 * d_model:3 * d_model] * y, lng_ref[0, 0:1], lnb_ref[0, 0:1])
        x1_s[...] = x1
        h_s[...] = (x1 * (1.0 + m[:, 4 * d_model:5 * d_model]) + m[:, 3 * d_model:4 * d_model]).astype(BF16)
        acc_s[...] = jnp.zeros(acc_s.shape, F32)

    h = h_s[...]
    gt = _dot(h, wg_ref[...])
    up = _dot(h, wu_ref[...])
    acc_s[...] += _dot((_silu(gt) * up).astype(BF16), wd_ref[...])

    @pl.when(kk == n_k - 1)
    def _():
        o_ref[...] = _layer_norm(alpha * x1_s[...] + m[:, 5 * d_model:6 * d_model] * acc_s[...],
                                 lng_ref[0, 1:2], lnb_ref[0, 1:2])


def _tail(x, oa, ob, oc, mod, layer, w_out, ln_g, ln_b, w_ffn_in, w_ffn_out, alpha):
    n, d = x.shape
    tm = 512
    d_ff = w_ffn_out.shape[1]
    n_k = d_ff // FFN_CHUNK
    rows_per_mod = n // mod.shape[0]
    row = lambda w_: pl.BlockSpec((tm, w_), lambda i, k: (i, 0))
    return pl.pallas_call(
        functools.partial(_tail_kernel, d_model=d, alpha=alpha, n_k=n_k),
        grid=(n // tm, n_k),
        in_specs=[
            row(d), row(oa.shape[1]), row(ob.shape[1]), row(oc.shape[1]),
            pl.BlockSpec((1, 1, mod.shape[2]), lambda i, k: ((i * tm) // rows_per_mod, 0, 0)),
            pl.BlockSpec((None,) + w_out.shape[1:], lambda i, k: (layer, 0, 0)),
            pl.BlockSpec((1, 2, d), lambda i, k: (layer, 0, 0)),
            pl.BlockSpec((1, 2, d), lambda i, k: (layer, 0, 0)),
            pl.BlockSpec((None, d, FFN_CHUNK), lambda i, k: (layer, 0, k)),
            pl.BlockSpec((None, d, FFN_CHUNK), lambda i, k: (layer, 0, n_k + k)),
            pl.BlockSpec((None, FFN_CHUNK, d), lambda i, k: (layer, k, 0)),
        ],
        out_specs=pl.BlockSpec((tm, d), lambda i, k: (i, 0)),
        out_shape=jax.ShapeDtypeStruct((n, d), F32),
        scratch_shapes=[pltpu.VMEM((tm, d), F32), pltpu.VMEM((tm, d), BF16), pltpu.VMEM((tm, d), F32)],
        compiler_params=_cparams("parallel", "arbitrary"),
        name="out_proj_ffn",
    )(x, oa, ob, oc, mod, w_out, ln_g, ln_b, w_ffn_in, w_ffn_in, w_ffn_out)


def _pair_scalar_order(p):
    return [d * B_HEADS + 2 * p + hh for d in range(2) for hh in range(2)]


def _layout_w_in(w_in):
    depth, d, _ = w_in.shape
    a_end = N_A
    b_end = a_end + N_B
    beta0 = b_end
    dec0 = beta0 + 2 * B_HEADS
    c0 = dec0 + 2 * B_HEADS
    parts = [w_in[:, :, :b_end]]
    for p in range(B_HEADS // 2):
        order = _pair_scalar_order(p)
        cols = [beta0 + i for i in order] + [dec0 + i for i in order]
        parts.append(w_in[:, :, np.asarray(cols)])
        parts.append(jnp.zeros((depth, d, LANES - len(cols)), w_in.dtype))
    parts.append(w_in[:, :, c0:])
    return jnp.concatenate(parts, axis=-1).astype(BF16)


def _delta_params(delta_a_log, delta_dt_bias, delta_norm):
    depth = delta_a_log.shape[0]
    pairs = B_HEADS // 2
    al = delta_a_log.reshape(depth, 2 * B_HEADS)
    dt = delta_dt_bias.reshape(depth, 2 * B_HEADS)
    prow = jnp.zeros((depth, pairs, 8, LANES), F32)
    pcol = jnp.zeros((depth, pairs, 8, LANES), F32)
    for p in range(pairs):
        order = np.asarray(_pair_scalar_order(p))
        prow = prow.at[:, p, 0, 4:8].set(al[:, order]).at[:, p, 1, 4:8].set(dt[:, order])
        pcol = pcol.at[:, p, 4:8, 0].set(al[:, order]).at[:, p, 4:8, 1].set(dt[:, order])
    prow = prow.at[:, :, 2, :].set(jnp.tile(delta_norm, (1, 2))[:, None, :])
    return prow, pcol


def kernel(x_prompt, x_sample, cache_attn_k, cache_attn_v, state_delta, state_hgrn, c, c_ctx, w_mod, b_mod, w_in, conv_w, delta_a_log, delta_dt_bias, delta_norm, hgrn_lb, hgrn_norm, diff_lambda, diff_norm, w_out, ln_g, ln_b, w_ffn_in, w_ffn_out):
    depth, d, _ = w_in.shape
    bp, lp, _ = x_prompt.shape
    bs, ls, _ = x_sample.shape
    alpha = (2 * depth) ** 0.25
    passes = 3

    cond_rows = 8 * ((1 + bs + 7) // 8)
    cond = jnp.concatenate([c_ctx[None], c, jnp.zeros((cond_rows - 1 - bs, d), F32)], axis=0)
    mod = _modulation(cond, w_mod, b_mod)

    w_in_l = _layout_w_in(w_in)
    w_out_b = w_out.astype(BF16)
    w_ffn_in_b = w_ffn_in.astype(BF16)
    w_ffn_out_b = w_ffn_out.astype(BF16)
    prow, pcol = _delta_params(delta_a_log, delta_dt_bias, delta_norm)
    lb = hgrn_lb.reshape(2, depth, C_HEADS // 2, LANES).transpose(2, 0, 1, 3)
    hnorm = jnp.tile(hgrn_norm, (1, 2)).reshape(depth, 1, LANES)
    rope = _rope_tables(ls)
    dconsts = _delta_consts()
    hconsts = _hgrn_consts()
    past = cache_attn_k.shape[2]
    ck = cache_attn_k.reshape(bs, depth, past, A_HEADS * 2 * A_QK)
    cv = cache_attn_v.reshape(bs, depth, past, A_HEADS * A_V)
    sh0 = jnp.swapaxes(state_hgrn, -1, -2)

    xp = x_prompt.reshape(bp * lp, d)
    xs = x_sample.reshape(bs * ls, d)
    new_k, new_v, new_sd, new_sh = [], [], [], []
    for l in range(depth):
        mod_p = mod[l, 0:1][:, None, :]
        mod_s = mod[l, 1:1 + bs][:, None, :]

        za, zb, zs, zc = _in_proj(xp, mod_p, w_in_l[l], lp, None)
        oa = _attention(za, lp, l, diff_lambda, diff_norm, None)
        ob, sd = _delta_mixer(zb, zs, lp, l, conv_w, prow, pcol, dconsts, None, passes)
        oc, sh = _hgrn_mixer(zc, lp, l, lb, hnorm, hconsts, None)
        xp = _tail(xp, oa, ob, oc, mod_p, l, w_out_b, ln_g, ln_b, w_ffn_in_b, w_ffn_out_b, alpha)
        new_k.append(za[:, A_HEADS * A_V:2 * A_HEADS * A_V].reshape(bp, lp, A_HEADS, 2 * A_QK))
        new_v.append(za[:, 2 * A_HEADS * A_V:].reshape(bp, lp, A_HEADS, A_V))
        new_sd.append(sd)
        new_sh.append(jnp.swapaxes(sh, -1, -2))

        za, zb, zs, zc = _in_proj(xs, mod_s, w_in_l[l], ls, rope)
        oa = _attention(za, ls, l, diff_lambda, diff_norm, (ck, cv))
        ob, _ = _delta_mixer(zb, zs, ls, l, conv_w, prow, pcol, dconsts, state_delta, passes)
        oc, _ = _hgrn_mixer(zc, ls, l, lb, hnorm, hconsts, sh0)
        xs = _tail(xs, oa, ob, oc, mod_s, l, w_out_b, ln_g, ln_b, w_ffn_in_b, w_ffn_out_b, alpha)

    return (xp.reshape(bp, lp, d), xs.reshape(bs, ls, d), jnp.stack(new_k, 1), jnp.stack(new_v, 1),
            jnp.stack(new_sd, 1), jnp.stack(new_sh, 1))
```

```python
import functools
import math

import numpy as np
import jax
import jax.numpy as jnp
from jax import lax
from jax.experimental import pallas as pl
from jax.experimental.pallas import tpu as pltpu

F32 = jnp.float32
BF16 = jnp.bfloat16

A_HEADS = 4
A_QK = 64
A_V = 2 * A_QK
B_HEADS = 4
B_DK = 64
B_DV = 64
C_HEADS = 4
C_DK = 64
C_DV = 64
GRID_W = 64
ROPE_BASE = 10000.0
LN_EPS = 1e-5
RMS_EPS = 1e-6
L2_EPS = 1e-6

LANES = 128
TOK = 128
DELTA_CHUNK = 64
VMEM_LIMIT = 60 * 1024 * 1024


def _cparams(*sem):
    return pltpu.CompilerParams(dimension_semantics=sem, vmem_limit_bytes=VMEM_LIMIT)


def _dot(a, b):
    return jnp.dot(a, b, preferred_element_type=F32)


def _dot_nt(a, b):
    return lax.dot_general(a, b, (((1,), (1,)), ((), ())), preferred_element_type=F32)


def _bdot(a, b):
    return _dot(a.astype(BF16), b.astype(BF16))


def _split3(x):
    hi = x.astype(BF16)
    r = x - hi.astype(F32)
    mid = r.astype(BF16)
    lo = (r - mid.astype(F32)).astype(BF16)
    return hi, mid, lo


def _dot01(w01, x):
    hi, mid, lo = _split3(x)
    return (_dot(w01, lo) + _dot(w01, mid)) + _dot(w01, hi)


def _dot01_nt(x, w01):
    hi, mid, lo = _split3(x)
    return (_dot_nt(lo, w01) + _dot_nt(mid, w01)) + _dot_nt(hi, w01)


def _sigmoid(x):
    return 1.0 / (1.0 + jnp.exp(-x))


def _silu(x):
    return x * _sigmoid(x)


def _softplus(x):
    return jnp.maximum(x, 0.0) + jnp.log(1.0 + jnp.exp(-jnp.abs(x)))


def _lane_iota(shape):
    return lax.broadcasted_iota(jnp.int32, shape, len(shape) - 1)


def _row_iota(shape):
    return lax.broadcasted_iota(jnp.int32, shape, len(shape) - 2)


def _layer_norm(x, g, b):
    mu = jnp.mean(x, axis=-1, keepdims=True)
    xc = x - mu
    var = jnp.mean(xc * xc, axis=-1, keepdims=True)
    return xc * lax.rsqrt(var + LN_EPS) * g + b


def _mod_kernel(c_ref, w_ref, b_ref, o_ref):
    cs = _silu(c_ref[...])
    o_ref[0] = _bdot(cs, w_ref[0]) + b_ref[0]


def _modulation(cond, w_mod, b_mod):
    depth, d, n = w_mod.shape
    rows = cond.shape[0]
    tn = 1536 if n % 1536 == 0 else n
    return pl.pallas_call(
        _mod_kernel,
        grid=(depth, n // tn),
        in_specs=[
            pl.BlockSpec((rows, d), lambda l, j: (0, 0)),
            pl.BlockSpec((1, d, tn), lambda l, j: (l, 0, j)),
            pl.BlockSpec((1, 1, tn), lambda l, j: (l, 0, j)),
        ],
        out_specs=pl.BlockSpec((1, rows, tn), lambda l, j: (l, 0, j)),
        out_shape=jax.ShapeDtypeStruct((depth, rows, n), F32),
        compiler_params=_cparams("parallel", "parallel"),
        name="modulation",
    )(cond, w_mod, b_mod.reshape(depth, 1, n))


N_A = 3 * A_HEADS * A_V
N_B = 4 * B_HEADS * B_DK
N_S = 2 * LANES
N_C = 5 * C_HEADS * C_DK
N_Z = N_A + N_B + N_S + N_C


def _in_proj_kernel(x_ref, mod_ref, w_ref, *rest, d_model, rope):
    if rope:
        rc_ref, rs1_ref, rs2_ref, za_ref, zb_ref, zs_ref, zc_ref = rest
    else:
        za_ref, zb_ref, zs_ref, zc_ref = rest
    m = mod_ref[0]
    shift = m[:, 0:d_model]
    scale = m[:, d_model:2 * d_model]
    h = (x_ref[...] * (1.0 + scale) + shift).astype(BF16)
    if rope:
        rc, rs1, rs2 = rc_ref[...], rs1_ref[...], rs2_ref[...]
        n_rot = 2 * A_HEADS
        for g in range(N_A // LANES):
            blk = _dot(h, w_ref[:, g * LANES:(g + 1) * LANES])
            if g < n_rot:
                up = pltpu.roll(blk, LANES - 16, axis=1)
                dn = pltpu.roll(blk, 16, axis=1)
                blk = blk * rc + up * rs1 + dn * rs2
            za_ref[:, g * LANES:(g + 1) * LANES] = blk
    else:
        za_ref[...] = _dot(h, w_ref[:, 0:N_A])
    zb_ref[...] = _dot(h, w_ref[:, N_A:N_A + N_B])
    zs_ref[...] = _dot(h, w_ref[:, N_A + N_B:N_A + N_B + N_S])
    zc_ref[...] = _dot(h, w_ref[:, N_A + N_B + N_S:N_Z])


def _in_proj(x, mod, w, seq_len, rope_tabs):
    n, d = x.shape
    tm = min(512, seq_len)
    tiles_per_seq = seq_len // tm
    rows_per_mod = n // mod.shape[0]
    rope = rope_tabs is not None
    in_specs = [
        pl.BlockSpec((tm, d), lambda i: (i, 0)),
        pl.BlockSpec((1, 1, mod.shape[2]), lambda i: ((i * tm) // rows_per_mod, 0, 0)),
        pl.BlockSpec((d, N_Z), lambda i: (0, 0)),
    ]
    args = [x, mod, w]
    if rope:
        in_specs += [pl.BlockSpec((tm, LANES), lambda i: (i % tiles_per_seq, 0))] * 3
        args += list(rope_tabs)
    outs = [N_A, N_B, N_S, N_C]
    return pl.pallas_call(
        functools.partial(_in_proj_kernel, d_model=d, rope=rope),
        grid=(n // tm,),
        in_specs=in_specs,
        out_specs=[pl.BlockSpec((tm, w_), lambda i: (i, 0)) for w_ in outs],
        out_shape=[jax.ShapeDtypeStruct((n, w_), F32) for w_ in outs],
        compiler_params=_cparams("parallel"),
        name="in_proj_rope" if rope else "in_proj",
    )(*args)


def _rope_tables(seq_len):
    half = A_QK // 2
    nf = half // 2
    pos = jnp.arange(seq_len)
    row = (pos // GRID_W).astype(F32)
    col = (pos % GRID_W).astype(F32)
    inv_freq = ROPE_BASE ** (-jnp.arange(nf, dtype=F32) / nf)
    ang_r = row[:, None] * inv_freq
    ang_c = col[:, None] * inv_freq
    cos64 = jnp.concatenate([jnp.cos(ang_r), jnp.cos(ang_r), jnp.cos(ang_c), jnp.cos(ang_c)], -1)
    zero = jnp.zeros_like(ang_r)
    s1_64 = jnp.concatenate([-jnp.sin(ang_r), zero, -jnp.sin(ang_c), zero], -1)
    s2_64 = jnp.concatenate([zero, jnp.sin(ang_r), zero, jnp.sin(ang_c)], -1)
    rep = LANES // A_QK
    return tuple(jnp.tile(t, (1, rep)) for t in (cos64, s1_64, s2_64))


def _attn_kernel(q_ref, k_ref, v_ref, *rest, lam_init, has_ctx, tq):
    if has_ctx:
        ck_ref, cv_ref, dl_ref, nrm_ref, o_ref = rest
    else:
        dl_ref, nrm_ref, o_ref = rest
    dl = dl_ref[0]
    lam = (jnp.exp(jnp.sum(dl[0:1] * dl[1:2], axis=1, keepdims=True))
           - jnp.exp(jnp.sum(dl[2:3] * dl[3:4], axis=1, keepdims=True)) + lam_init)
    q = q_ref[...] * (A_QK ** -0.5)
    lane = _lane_iota(q.shape)
    qq = jnp.concatenate([jnp.where(lane < A_QK, q, 0.0), jnp.where(lane >= A_QK, q, 0.0)], axis=0).astype(BF16)
    s = _dot_nt(qq, k_ref[...].astype(BF16))
    m = jnp.max(s, axis=-1, keepdims=True)
    if has_ctx:
        sc = _dot_nt(qq, ck_ref[0, 0].astype(BF16))
        m = jnp.maximum(m, jnp.max(sc, axis=-1, keepdims=True))
    e = jnp.exp(s - m)
    den = jnp.sum(e, axis=-1, keepdims=True)
    if has_ctx:
        ec = jnp.exp(sc - m)
        den = den + jnp.sum(ec, axis=-1, keepdims=True)
    r = 1.0 / den
    r1 = r[:tq]
    r2 = lam * r[tq:]
    p = e[:tq] * r1 - e[tq:] * r2
    o = _dot(p.astype(BF16), v_ref[...].astype(BF16))
    if has_ctx:
        pc = ec[:tq] * r1 - ec[tq:] * r2
        o = o + _dot(pc.astype(BF16), cv_ref[0, 0].astype(BF16))
    o = o * lax.rsqrt(jnp.mean(o * o, axis=-1, keepdims=True) + RMS_EPS) * nrm_ref[0]
    o_ref[...] = o * (1.0 - lam_init)


def _attention(za, seq_len, layer, diff_lambda, diff_norm, ctx_kv):
    n = za.shape[0]
    n_seq = n // seq_len
    tq = min(128, seq_len)
    nq = seq_len // tq
    lam_init = 0.8 - 0.6 * math.exp(-0.3 * layer)
    has_ctx = ctx_kv is not None
    in_specs = [
        pl.BlockSpec((tq, LANES), lambda b, h, i: (b * nq + i, h)),
        pl.BlockSpec((seq_len, LANES), lambda b, h, i: (b, A_HEADS + h)),
        pl.BlockSpec((seq_len, LANES), lambda b, h, i: (b, 2 * A_HEADS + h)),
    ]
    args = [za, za, za]
    if has_ctx:
        ck, cv = ctx_kv
        past = ck.shape[2]
        spec = pl.BlockSpec((1, 1, past, LANES), lambda b, h, i: (b, layer, 0, h))
        in_specs += [spec, spec]
        args += [ck, cv]
    in_specs += [
        pl.BlockSpec((1, 4, A_QK), lambda b, h, i: (layer, 0, 0)),
        pl.BlockSpec((1, 1, A_V), lambda b, h, i: (layer, 0, 0)),
    ]
    args += [diff_lambda, diff_norm.reshape(diff_norm.shape[0], 1, A_V)]
    return pl.pallas_call(
        functools.partial(_attn_kernel, lam_init=lam_init, has_ctx=has_ctx, tq=tq),
        grid=(n_seq, A_HEADS, nq),
        in_specs=in_specs,
        out_specs=pl.BlockSpec((tq, LANES), lambda b, h, i: (b * nq + i, h)),
        out_shape=jax.ShapeDtypeStruct((n, A_HEADS * A_V), F32),
        compiler_params=_cparams("parallel", "parallel", "arbitrary"),
        name="diff_attn_ctx" if has_ctx else "diff_attn",
    )(*args)


def _head_sum(x):
    lane = _lane_iota(x.shape)
    lo = lane < 64
    s0 = jnp.sum(jnp.where(lo, x, 0.0), axis=-1, keepdims=True)
    s1 = jnp.sum(jnp.where(lo, 0.0, x), axis=-1, keepdims=True)
    return jnp.where(lo, s0, s1)


def _pair_blockdiag(a, b):
    z = jnp.zeros_like(a)
    return jnp.concatenate([jnp.concatenate([a, z], axis=1), jnp.concatenate([z, b], axis=1)], axis=0)


def _gated_norm_store(oacc_s, gt_ref, norm_row, o_ref, n_groups):
    def body(j, carry):
        r0 = pl.multiple_of(j * TOK, TOK)
        o = oacc_s[pl.ds(r0, TOK), :]
        ms = _head_sum(o * o) * (1.0 / 64.0)
        o = o * lax.rsqrt(ms + RMS_EPS) * norm_row
        o_ref[pl.ds(r0, TOK), :] = o * _silu(gt_ref[pl.ds(r0, TOK), :])
        return carry
    lax.fori_loop(0, n_groups, body, 0)


DELTA_MERGES = (4, 8, 16, 32, 64)
M_CAUSAL, M_STRICT, M_PAIR, M_MERGE0 = 0, 1, 2, 3
M_EYE = M_MERGE0 + len(DELTA_MERGES)
M_HEADS = M_EYE + 1


def _delta_consts():
    t = np.arange(TOK)[:, None]
    s = np.arange(TOK)[None, :]

    def same(n):
        return (t // n) == (s // n)

    masks, cums = [], []
    for d in range(2):
        before = (s <= t) if d == 0 else (s >= t)
        strict = (s < t) if d == 0 else (s > t)
        causal = same(DELTA_CHUNK) & before
        st = same(DELTA_CHUNK) & strict
        rows = [causal, st, st & same(2)]
        rows += [st & same(n) & ~same(n // 2) for n in DELTA_MERGES]
        rows += [t == s, same(64)]
        masks.append(np.stack(rows))
        cums.append(np.concatenate([causal, same(DELTA_CHUNK) & ~before, same(DELTA_CHUNK)], 0))
    return (jnp.asarray(np.stack(masks), F32), jnp.asarray(np.stack(cums), BF16))


def _delta_prep_kernel(z_ref, zp_ref, zn_ref, sc_ref, sr_ref, cw_ref, prow_ref, pcol_ref, msk_ref, cum_ref,
                       u_ref, w_ref, qd_ref, qkm_ref, kdt_ref, gl_ref, *, blocks_per_seq, g_tiles):
    jloc = pl.program_id(0) % blocks_per_seq
    rb = g_tiles * TOK
    nqk = B_HEADS * B_DK
    x = z_ref[...]
    cw = cw_ref[0]
    prev = jnp.where(jloc > 0, zp_ref[7:8, :], 0.0)
    nxt = jnp.where(jloc < blocks_per_seq - 1, zn_ref[0:1, :], 0.0)
    row = _row_iota(x.shape)
    dn = jnp.where(row == 0, prev, pltpu.roll(x, 1, axis=0))
    up = jnp.where(row == rb - 1, nxt, pltpu.roll(x, rb - 1, axis=0))
    y = _silu(cw[0:1] * dn + cw[1:2] * x + cw[2:3] * up)

    lo = _lane_iota((TOK, LANES)) < 64
    hi = jnp.logical_not(lo)
    units = [(g, p) for g in range(g_tiles) for p in range(B_HEADS // 2)]
    chains = [(g, p, d, hh) for (g, p) in units for d in range(2) for hh in range(2)]

    q, k, v, kk, qk, gcol, grow = {}, {}, {}, {}, {}, {}, {}
    for (g, p) in units:
        rs = slice(g * TOK, (g + 1) * TOK)
        qs = y[rs, p * LANES:(p + 1) * LANES]
        ks = y[rs, nqk + p * LANES:nqk + (p + 1) * LANES]
        v[g, p] = y[rs, 2 * nqk + p * LANES:2 * nqk + (p + 1) * LANES]
        q[g, p] = qs * lax.rsqrt(_head_sum(qs * qs) + L2_EPS) * (B_DK ** -0.5)
        k[g, p] = ks * lax.rsqrt(_head_sum(ks * ks) + L2_EPS)
    for (g, p) in units:
        k16 = k[g, p].astype(BF16)
        for hh, sel in enumerate((lo, hi)):
            kk[g, p, hh] = _dot_nt(jnp.where(sel, k[g, p], 0.0).astype(BF16), k16)
            qk[g, p, hh] = _dot_nt(jnp.where(sel, q[g, p], 0.0).astype(BF16), k16)
    for (g, p) in units:
        prow = prow_ref[0, p]
        pcol = pcol_ref[0, p]
        xs = sc_ref[g * TOK:(g + 1) * TOK, p * LANES:(p + 1) * LANES]
        lane = _lane_iota(xs.shape)
        gcol[g, p] = jnp.where(lane < 4, _sigmoid(xs), -jnp.exp(prow[0:1]) * _softplus(xs + prow[1:2]))
        xr = sr_ref[p, g]
        rowi = _row_iota(xr.shape)
        grow[g, p] = jnp.where(rowi < 4, _sigmoid(xr), -jnp.exp(pcol[:, 0:1]) * _softplus(xr + pcol[:, 1:2]))
    cs, br = {}, {}
    for (g, p) in units:
        for d in range(2):
            cs[g, p, d] = _dot01(cum_ref[d], gcol[g, p])
            br[g, p, d] = _dot01_nt(grow[g, p], cum_ref[d, 0:TOK])
    beta, bcol, dec, m, x = {}, {}, {}, {}, {}
    for ch in chains:
        g, p, d, hh = ch
        cb, cg = d * 2 + hh, 4 + d * 2 + hh
        beta[ch] = gcol[g, p][:, cb:cb + 1]
        bcol[ch] = cs[g, p, d][0:TOK, cg:cg + 1]
        dec[ch] = jnp.exp(jnp.minimum(bcol[ch] - br[g, p, d][cg:cg + 1, :], 0.0)) * msk_ref[d, M_CAUSAL]
        m[ch] = kk[g, p, hh] * beta[ch] * dec[ch] * msk_ref[d, M_STRICT]
        x[ch] = msk_ref[d, M_EYE] - m[ch] * msk_ref[d, M_PAIR]
    for lvl in range(len(DELTA_MERGES)):
        yv = {ch: _bdot(m[ch] * msk_ref[ch[2], M_MERGE0 + lvl], x[ch]) for ch in chains}
        x = {ch: x[ch] - _bdot(x[ch], yv[ch]) for ch in chains}
    for (g, p) in units:
        rs = slice(g * TOK, (g + 1) * TOK)
        for d in range(2):
            sol = None
            eq, ek, gl, qkm = [], [], [], []
            for hh, sel in enumerate((lo, hi)):
                ch = (g, p, d, hh)
                cg = 4 + d * 2 + hh
                eb = jnp.exp(bcol[ch])
                rhs = jnp.concatenate([jnp.where(sel, v[g, p] * beta[ch], 0.0),
                                       jnp.where(sel, k[g, p] * (beta[ch] * eb), 0.0)], axis=1)
                part = _bdot(x[ch], rhs)
                sol = part if sol is None else sol + part
                qkm.append((qk[g, p, hh] * dec[ch]).astype(BF16))
                eq.append(eb)
                ek.append(jnp.exp(cs[g, p, d][TOK:2 * TOK, cg:cg + 1]))
                gl.append(jnp.exp(cs[g, p, d][2 * TOK:3 * TOK, cg:cg + 1]))
            u_ref[d, p, rs, :] = sol[:, 0:LANES]
            w_ref[d, p, rs, :] = sol[:, LANES:2 * LANES].astype(BF16)
            qd_ref[d, p, rs, :] = (q[g, p] * jnp.where(lo, eq[0], eq[1])).astype(BF16)
            qkm_ref[d, p, rs, :] = jnp.concatenate(qkm, axis=1)
            kd = k[g, p] * jnp.where(lo, ek[0], ek[1])
            kdt_ref[d, p, rs, :] = kd.T.astype(BF16)
            glf = jnp.where(lo, gl[0], gl[1])
            gl_ref[d, p, g * 16:(g + 1) * 16, :] = jnp.concatenate([glf[0:8], glf[64:72]], axis=0)


def _delta_prep(zb, zs, seq_len, layer, conv_w, prow, pcol, consts):
    n = zb.shape[0]
    pairs = B_HEADS // 2
    g_tiles = 2
    rb = g_tiles * TOK
    blocks_per_seq = seq_len // rb
    n_blocks = n // rb
    nconv = conv_w.shape[2]
    msk, cum = consts
    zs_rows = zs.reshape(n // TOK, TOK, pairs, LANES)[..., :8].transpose(2, 0, 3, 1)
    last8 = n // 8 - 1
    out_w = [LANES, LANES, LANES, 2 * LANES, LANES]
    out_t = [F32, BF16, BF16, BF16, BF16]
    return pl.pallas_call(
        functools.partial(_delta_prep_kernel, blocks_per_seq=blocks_per_seq, g_tiles=g_tiles),
        grid=(n_blocks,),
        in_specs=[
            pl.BlockSpec((rb, nconv), lambda i: (i, 0)),
            pl.BlockSpec((8, nconv), lambda i: (jnp.maximum(i * (rb // 8) - 1, 0), 0)),
            pl.BlockSpec((8, nconv), lambda i: (jnp.minimum((i + 1) * (rb // 8), last8), 0)),
            pl.BlockSpec((rb, pairs * LANES), lambda i: (i, 0)),
            pl.BlockSpec((pairs, g_tiles, 8, TOK), lambda i: (0, i, 0, 0)),
            pl.BlockSpec((1, 3, nconv), lambda i: (layer, 0, 0)),
            pl.BlockSpec((1, pairs, 8, LANES), lambda i: (layer, 0, 0, 0)),
            pl.BlockSpec((1, pairs, 8, LANES), lambda i: (layer, 0, 0, 0)),
            pl.BlockSpec(msk.shape, lambda i: (0, 0, 0, 0)),
            pl.BlockSpec(cum.shape, lambda i: (0, 0, 0)),
        ],
        out_specs=[pl.BlockSpec((2, pairs, rb, w_), lambda i: (0, 0, i, 0)) for w_ in out_w]
        + [pl.BlockSpec((2, pairs, g_tiles * 16, LANES), lambda i: (0, 0, i, 0))],
        out_shape=[jax.ShapeDtypeStruct((2, pairs, n, w_), t_) for w_, t_ in zip(out_w, out_t)]
        + [jax.ShapeDtypeStruct((2, pairs, n // TOK * 16, LANES), F32)],
        compiler_params=_cparams("parallel"),
        name="delta_prep",
    )(zb, zb, zb, zs, zs_rows, conv_w, prow, pcol, msk, cum)


def _delta_scan_kernel(*refs, sb, tb, has_s0):
    fwd, bwd = refs[0:6], refs[6:12]
    hm_ref = refs[12]
    if has_s0:
        s0_ref, of_ref, ob_ref, sout_ref, s_s = refs[13:]
    else:
        of_ref, ob_ref, sout_ref, s_s = refs[13:]
    t = pl.program_id(1)
    n_t = pl.num_programs(1)
    n_g = tb // TOK
    pairs = B_HEADS // 2
    chains = [(d, sq, p) for d in range(2) for sq in range(sb) for p in range(pairs)]

    @pl.when(t == 0)
    def _():
        for (d, sq, p) in chains:
            if has_s0:
                s_s[d, sq, p] = _pair_blockdiag(s0_ref[sq, 0, d, 2 * p], s0_ref[sq, 0, d, 2 * p + 1])
            else:
                s_s[d, sq, p] = jnp.zeros((LANES, LANES), F32)

    lane64 = _lane_iota((DELTA_CHUNK, LANES)) < 64
    zpad = jnp.zeros((DELTA_CHUNK, LANES), BF16)

    def body(gi, carry):
        for step in range(2):
            tmp = {}
            for ch in chains:
                d, sq, p = ch
                u_r, w_r = (fwd, bwd)[d][0:2]
                g = gi if d == 0 else n_g - 1 - gi
                c = step if d == 0 else 1 - step
                rc = pl.multiple_of(g * TOK + c * DELTA_CHUNK, DELTA_CHUNK)
                s = s_s[d, sq, p]
                s16 = s.astype(BF16)
                vn = u_r[0, p, sq, pl.ds(rc, DELTA_CHUNK), :] - _dot(w_r[0, p, sq, pl.ds(rc, DELTA_CHUNK), :], s16)
                tmp[ch] = (s, s16, vn, g, c, rc)
            for ch in chains:
                d, sq, p = ch
                _, _, qd_r, qkm_r, kdt_r, gl_r = (fwd, bwd)[d]
                o_r = (of_ref, ob_ref)[d]
                s, s16, vn, g, c, rc = tmp[ch]
                r0 = pl.multiple_of(g * TOK, TOK)
                v0 = jnp.where(lane64, vn, 0.0).astype(BF16)
                v1 = jnp.where(lane64, 0.0, vn).astype(BF16)
                vnb = vn.astype(BF16)
                if c == 0:
                    vext = jnp.concatenate([vnb, zpad], axis=0)
                    v2 = jnp.concatenate([v0, zpad, v1, zpad], axis=0)
                else:
                    vext = jnp.concatenate([zpad, vnb], axis=0)
                    v2 = jnp.concatenate([zpad, v0, zpad, v1], axis=0)
                o = (_dot(qd_r[0, p, sq, pl.ds(rc, DELTA_CHUNK), :], s16)
                     + _dot(qkm_r[0, p, sq, pl.ds(rc, DELTA_CHUNK), :], v2))
                o_r[sq, pl.ds(rc, DELTA_CHUNK), p * LANES:(p + 1) * LANES] = o
                glr = gl_r[0, p, sq, pl.ds(pl.multiple_of(g * 16 + c * 8, 8), 8), :][0:1]
                s_s[d, sq, p] = s * glr + _dot(kdt_r[0, p, sq, pl.ds(r0, TOK), :], vext) * hm_ref[...]
        return carry

    lax.fori_loop(0, n_g, body, 0)

    @pl.when(t == n_t - 1)
    def _():
        for (d, sq, p) in chains:
            s = s_s[d, sq, p]
            sout_ref[sq, d, 2 * p] = s[0:64, 0:64]
            sout_ref[sq, d, 2 * p + 1] = s[64:128, 64:128]


def _delta_scan(prep, n_seq, seq_len, layer, head_mask, s0):
    pairs = B_HEADS // 2
    has_s0 = s0 is not None
    sb = n_seq if n_seq <= 2 else 8
    tb = min(seq_len, 512)
    n_t = seq_len // tb
    arrs = [a.reshape(2, pairs, n_seq, a.shape[2] // n_seq, a.shape[3]) for a in prep]

    def specs(d):
        tmap = (lambda s, t: (d, 0, s, t, 0)) if d == 0 else (lambda s, t: (d, 0, s, n_t - 1 - t, 0))
        return [pl.BlockSpec((1, pairs, sb, a.shape[3] // n_t, a.shape[4]), tmap) for a in arrs]

    in_specs = specs(0) + specs(1) + [pl.BlockSpec((TOK, LANES), lambda s, t: (0, 0))]
    args = arrs + arrs + [head_mask]
    if has_s0:
        in_specs.append(pl.BlockSpec((sb, 1, 2, B_HEADS, B_DK, B_DV), lambda s, t: (s, layer, 0, 0, 0, 0)))
        args.append(s0)
    width = B_HEADS * B_DV
    o_f, o_b, s_out = pl.pallas_call(
        functools.partial(_delta_scan_kernel, sb=sb, tb=tb, has_s0=has_s0),
        grid=(n_seq // sb, n_t),
        in_specs=in_specs,
        out_specs=[
            pl.BlockSpec((sb, tb, width), lambda s, t: (s, t, 0)),
            pl.BlockSpec((sb, tb, width), lambda s, t: (s, n_t - 1 - t, 0)),
            pl.BlockSpec((sb, 2, B_HEADS, B_DK, B_DV), lambda s, t: (s, 0, 0, 0, 0)),
        ],
        out_shape=[
            jax.ShapeDtypeStruct((n_seq, seq_len, width), F32),
            jax.ShapeDtypeStruct((n_seq, seq_len, width), F32),
            jax.ShapeDtypeStruct((n_seq, 2, B_HEADS, B_DK, B_DV), F32),
        ],
        scratch_shapes=[pltpu.VMEM((2, sb, pairs, LANES, LANES), F32)],
        compiler_params=_cparams("parallel", "arbitrary"),
        name="delta_scan_s0" if has_s0 else "delta_scan",
    )(*args)
    n = n_seq * seq_len
    return o_f.reshape(n, width), o_b.reshape(n, width), s_out


HGRN_LEVELS = 7
HGRN_WROWS = (HGRN_LEVELS + 2) * TOK + 8


def _hgrn_consts():
    t = np.arange(TOK)
    masks = [t[:, None] == t[None, :]]
    for lvl in range(1, HGRN_LEVELS + 1):
        n = 1 << lvl
        masks.append((t[:, None] // n) == (t[None, :] // n))
    ws = []
    for d in range(2):
        tau = t if d == 0 else TOK - 1 - t
        tt, ti = tau[:, None], tau[None, :]
        blocks = []
        for lvl in range(1, HGRN_LEVELS + 1):
            n = 1 << lvl
            piv = (tau - tau % n + n // 2 - 1)[:, None]
            upper = ((tau % n) >= n // 2)[:, None]
            blocks.append(np.where(upper, (ti > piv) & (ti <= tt), (ti > tt) & (ti <= piv)))
        blocks.append(ti <= tt)
        blocks.append(ti > tt)
        blocks.append(np.ones((8, TOK), bool))
        ws.append(np.concatenate(blocks, 0))
    return jnp.asarray(np.stack(masks), F32), jnp.asarray(np.stack(ws), BF16)


def _hgrn_kernel(q_ref, ff_ref, fb_ref, i_ref, gt_ref, lb_ref, nrm_ref, msk_ref, w_ref, *rest,
                 seq_len, layer, has_s0):
    if has_s0:
        s0_ref, o_ref, sout_ref, oacc_s = rest
    else:
        o_ref, sout_ref, oacc_s = rest
    n_groups = seq_len // TOK
    lbs = []
    for d in range(2):
        x = lb_ref[0, d]
        e = jnp.exp(x - jnp.max(x, axis=0, keepdims=True))
        sm = e / jnp.sum(e, axis=0, keepdims=True)
        if layer > 0:
            lbs.append(jnp.sum(sm[1:layer + 1], axis=0, keepdims=True))
        else:
            lbs.append(jnp.zeros((1, LANES), F32))

    oacc_s[...] = jnp.zeros(oacc_s.shape, F32)
    bd = msk_ref[6]

    def stack_heads(x, lo):
        return jnp.concatenate([jnp.where(lo, x, 0.0), jnp.where(lo, 0.0, x)], axis=0).astype(BF16)

    def tile_step(d, st, j):
        r0 = pl.multiple_of(j * TOK, TOK)
        q = _silu(q_ref[pl.ds(r0, TOK), :])
        v = i_ref[pl.ds(r0, TOK), :]
        f = (ff_ref if d == 0 else fb_ref)[pl.ds(r0, TOK), :]
        lbv = lbs[d]
        forget = lbv + (1.0 - lbv) * _sigmoid(f)
        key = 1.0 - forget
        ex = _dot01(w_ref[d], jnp.log(forget))
        row = _row_iota((TOK, LANES))
        tau = row if d == 0 else TOK - 1 - row
        lo = _lane_iota((TOK, LANES)) < 64
        m0 = msk_ref[0]
        acc = _dot_nt(stack_heads(q, lo), key.astype(BF16)) * jnp.concatenate([m0, m0], axis=0)
        for lvl in range(1, HGRN_LEVELS + 1):
            e = jnp.exp(ex[(lvl - 1) * TOK:lvl * TOK])
            up = (tau & (1 << (lvl - 1))) != 0
            qt = jnp.where(up, q * e, 0.0)
            kt = jnp.where(up, 0.0, key * e)
            a = _dot_nt(stack_heads(qt, lo), kt.astype(BF16))
            if lvl < HGRN_LEVELS:
                ml = msk_ref[lvl]
                a = a * jnp.concatenate([ml, ml], axis=0)
            acc = acc + a
        intra = (_dot(acc[0:TOK].astype(BF16), jnp.where(lo, v, 0.0).astype(BF16))
                 + _dot(acc[TOK:2 * TOK].astype(BF16), jnp.where(lo, 0.0, v).astype(BF16)))
        base = HGRN_LEVELS * TOK
        qd = (q * jnp.exp(ex[base:base + TOK])).astype(BF16)
        kd = (key * jnp.exp(ex[base + TOK:base + 2 * TOK])).astype(BF16)
        gl = jnp.exp(ex[base + 2 * TOK:base + 2 * TOK + 1])
        o = intra + _dot_nt(qd, st.astype(BF16))
        st = st * gl + _dot(v.T.astype(BF16), kd) * bd
        oacc_s[pl.ds(r0, TOK), :] += o
        return st

    def scan(i, carry):
        return tile_step(0, carry[0], i), tile_step(1, carry[1], n_groups - 1 - i)

    if has_s0:
        init = tuple(_pair_blockdiag(s0_ref[0, 0, d, 0], s0_ref[0, 0, d, 1]) for d in range(2))
    else:
        init = (jnp.zeros((LANES, LANES), F32),) * 2
    fin = lax.fori_loop(0, n_groups, scan, init)
    for d in range(2):
        sout_ref[0, d, 0] = fin[d][0:64, 0:64]
        sout_ref[0, d, 1] = fin[d][64:128, 64:128]
    _gated_norm_store(oacc_s, gt_ref, nrm_ref[0], o_ref, n_groups)


def _hgrn_mixer(zc, seq_len, layer, lb, nrm, consts, s0t):
    n = zc.shape[0]
    n_seq = n // seq_len
    pairs = C_HEADS // 2
    has_s0 = s0t is not None
    msk, wst = consts
    blk = lambda c0: pl.BlockSpec((seq_len, LANES), lambda b, p, c0=c0: (b, c0 + p))
    in_specs = [
        blk(0), blk(pairs), blk(2 * pairs), blk(3 * pairs), blk(4 * pairs),
        pl.BlockSpec((1,) + lb.shape[1:], lambda b, p: (p, 0, 0, 0)),
        pl.BlockSpec((1, 1, LANES), lambda b, p: (layer, 0, 0)),
        pl.BlockSpec(msk.shape, lambda b, p: (0, 0, 0)),
        pl.BlockSpec(wst.shape, lambda b, p: (0, 0, 0)),
    ]
    args = [zc, zc, zc, zc, zc, lb, nrm, msk, wst]
    if has_s0:
        in_specs.append(pl.BlockSpec((1, 1, 2, 2, C_DV, C_DK), lambda b, p: (b, layer, 0, p, 0, 0)))
        args.append(s0t)
    return pl.pallas_call(
        functools.partial(_hgrn_kernel, seq_len=seq_len, layer=layer, has_s0=has_s0),
        grid=(n_seq, pairs),
        in_specs=in_specs,
        out_specs=[
            pl.BlockSpec((seq_len, LANES), lambda b, p: (b, p)),
            pl.BlockSpec((1, 2, 2, C_DV, C_DK), lambda b, p: (b, 0, p, 0, 0)),
        ],
        out_shape=[
            jax.ShapeDtypeStruct((n, C_HEADS * C_DV), F32),
            jax.ShapeDtypeStruct((n_seq, 2, C_HEADS, C_DV, C_DK), F32),
        ],
        scratch_shapes=[pltpu.VMEM((seq_len, LANES), F32)],
        compiler_params=_cparams("parallel", "parallel"),
        name="hgrn_s0" if has_s0 else "hgrn",
    )(*args)


FFN_SPLIT = 2


def _tail_kernel(x_ref, oa_ref, obf_ref, obb_ref, bg_ref, oc_ref, mod_ref, bn_ref, wo_ref, lng_ref, lnb_ref,
                 wi_ref, wd_ref, o_ref, *, d_model, d_ff, alpha):
    m = mod_ref[0]
    na, nb = oa_ref.shape[1], obf_ref.shape[1]
    ob = obf_ref[...] + obb_ref[...]
    halves = []
    for hp in range(nb // LANES):
        blk = ob[:, hp * LANES:(hp + 1) * LANES]
        halves.append(blk * lax.rsqrt(_head_sum(blk * blk) * (1.0 / B_DV) + RMS_EPS))
    ob = jnp.concatenate(halves, axis=1) * bn_ref[0] * _silu(bg_ref[...])
    y = (_bdot(oa_ref[...], wo_ref[0:na]) + _bdot(ob, wo_ref[na:na + nb]) + _bdot(oc_ref[...], wo_ref[na + nb:]))
    x1 = _layer_norm(alpha * x_ref[...] + m[:, 2 * d_model:3 * d_model] * y, lng_ref[0, 0:1], lnb_ref[0, 0:1])
    h = (x1 * (1.0 + m[:, 4 * d_model:5 * d_model]) + m[:, 3 * d_model:4 * d_model]).astype(BF16)
    ck = d_ff // FFN_SPLIT
    acc = None
    for c0 in range(0, d_ff, ck):
        gt = _dot(h, wi_ref[:, c0:c0 + ck])
        up = _dot(h, wi_ref[:, d_ff + c0:d_ff + c0 + ck])
        part = _dot((_silu(gt) * up).astype(BF16), wd_ref[c0:c0 + ck, :])
        acc = part if acc is None else acc + part
    o_ref[...] = _layer_norm(alpha * x1 + m[:, 5 * d_model:6 * d_model] * acc, lng_ref[0, 1:2], lnb_ref[0, 1:2])


def _tail(x, oa, obf, obb, zb, oc, mod, layer, bnorm, w_out, ln_g, ln_b, w_ffn_in, w_ffn_out, alpha):
    n, d = x.shape
    tm = 512
    d_ff = w_ffn_out.shape[1]
    rows_per_mod = n // mod.shape[0]
    nb = obf.shape[1]
    row = lambda w_: pl.BlockSpec((tm, w_), lambda i: (i, 0))
    resident = lambda a: pl.BlockSpec((None,) + a.shape[1:], lambda i: (layer, 0, 0), pipeline_mode=pl.Buffered(1))
    return pl.pallas_call(
        functools.partial(_tail_kernel, d_model=d, d_ff=d_ff, alpha=alpha),
        grid=(n // tm,),
        in_specs=[
            row(d), row(oa.shape[1]), row(nb), row(nb),
            pl.BlockSpec((tm, nb), lambda i: (i, zb.shape[1] // nb - 1)),
            row(oc.shape[1]),
            pl.BlockSpec((1, 1, mod.shape[2]), lambda i: ((i * tm) // rows_per_mod, 0, 0)),
            pl.BlockSpec((1, 1, nb), lambda i: (layer, 0, 0)),
            resident(w_out),
            pl.BlockSpec((1, 2, d), lambda i: (layer, 0, 0)),
            pl.BlockSpec((1, 2, d), lambda i: (layer, 0, 0)),
            resident(w_ffn_in), resident(w_ffn_out),
        ],
        out_specs=pl.BlockSpec((tm, d), lambda i: (i, 0)),
        out_shape=jax.ShapeDtypeStruct((n, d), F32),
        compiler_params=_cparams("parallel"),
        name="out_proj_ffn",
    )(x, oa, obf, obb, zb, oc, mod, bnorm, w_out, ln_g, ln_b, w_ffn_in, w_ffn_out)


def _pair_scalar_order(p):
    return [d * B_HEADS + 2 * p + hh for d in range(2) for hh in range(2)]


def _layout_w_in(w_in):
    depth, d, _ = w_in.shape
    a_end = N_A
    b_end = a_end + N_B
    beta0 = b_end
    dec0 = beta0 + 2 * B_HEADS
    c0 = dec0 + 2 * B_HEADS
    parts = [w_in[:, :, :b_end]]
    for p in range(B_HEADS // 2):
        order = _pair_scalar_order(p)
        cols = [beta0 + i for i in order] + [dec0 + i for i in order]
        parts.append(w_in[:, :, np.asarray(cols)])
        parts.append(jnp.zeros((depth, d, LANES - len(cols)), w_in.dtype))
    parts.append(w_in[:, :, c0:])
    return jnp.concatenate(parts, axis=-1).astype(BF16)


def _delta_params(delta_a_log, delta_dt_bias):
    depth = delta_a_log.shape[0]
    pairs = B_HEADS // 2
    al = delta_a_log.reshape(depth, 2 * B_HEADS)
    dt = delta_dt_bias.reshape(depth, 2 * B_HEADS)
    prow = jnp.zeros((depth, pairs, 8, LANES), F32)
    pcol = jnp.zeros((depth, pairs, 8, LANES), F32)
    for p in range(pairs):
        order = np.asarray(_pair_scalar_order(p))
        prow = prow.at[:, p, 0, 4:8].set(al[:, order]).at[:, p, 1, 4:8].set(dt[:, order])
        pcol = pcol.at[:, p, 4:8, 0].set(al[:, order]).at[:, p, 4:8, 1].set(dt[:, order])
    return prow, pcol


def kernel(x_prompt, x_sample, cache_attn_k, cache_attn_v, state_delta, state_hgrn, c, c_ctx, w_mod, b_mod, w_in, conv_w, delta_a_log, delta_dt_bias, delta_norm, hgrn_lb, hgrn_norm, diff_lambda, diff_norm, w_out, ln_g, ln_b, w_ffn_in, w_ffn_out):
    depth, d, _ = w_in.shape
    bp, lp, _ = x_prompt.shape
    bs, ls, _ = x_sample.shape
    alpha = (2 * depth) ** 0.25

    cond_rows = 8 * ((1 + bs + 7) // 8)
    cond = jnp.concatenate([c_ctx[None], c, jnp.zeros((cond_rows - 1 - bs, d), F32)], axis=0)
    mod = _modulation(cond, w_mod, b_mod)

    w_in_l = _layout_w_in(w_in)
    w_out_b = w_out.astype(BF16)
    w_ffn_in_b = w_ffn_in.astype(BF16)
    w_ffn_out_b = w_ffn_out.astype(BF16)
    prow, pcol = _delta_params(delta_a_log, delta_dt_bias)
    bnorm = jnp.tile(delta_norm, (1, B_HEADS)).reshape(depth, 1, B_HEADS * B_DV)
    lb = hgrn_lb.reshape(2, depth, C_HEADS // 2, LANES).transpose(2, 0, 1, 3)
    hnorm = jnp.tile(hgrn_norm, (1, 2)).reshape(depth, 1, LANES)
    rope = _rope_tables(ls)
    dconsts = _delta_consts()
    head_mask = dconsts[0][0, M_HEADS]
    hconsts = _hgrn_consts()
    past = cache_attn_k.shape[2]
    ck = cache_attn_k.reshape(bs, depth, past, A_HEADS * 2 * A_QK)
    cv = cache_attn_v.reshape(bs, depth, past, A_HEADS * A_V)
    sh0 = jnp.swapaxes(state_hgrn, -1, -2)

    xp = x_prompt.reshape(bp * lp, d)
    xs = x_sample.reshape(bs * ls, d)
    new_k, new_v, new_sd, new_sh = [], [], [], []
    for l in range(depth):
        mod_p = mod[l, 0:1][:, None, :]
        mod_s = mod[l, 1:1 + bs][:, None, :]

        za, zb, zs, zc = _in_proj(xp, mod_p, w_in_l[l], lp, None)
        oa = _attention(za, lp, l, diff_lambda, diff_norm, None)
        prep = _delta_prep(zb, zs, lp, l, conv_w, prow, pcol, dconsts)
        obf, obb, sd = _delta_scan(prep, bp, lp, l, head_mask, None)
        oc, sh = _hgrn_mixer(zc, lp, l, lb, hnorm, hconsts, None)
        xp = _tail(xp, oa, obf, obb, zb, oc, mod_p, l, bnorm, w_out_b, ln_g, ln_b, w_ffn_in_b, w_ffn_out_b, alpha)
        new_k.append(za[:, A_HEADS * A_V:2 * A_HEADS * A_V].reshape(bp, lp, A_HEADS, 2 * A_QK))
        new_v.append(za[:, 2 * A_HEADS * A_V:].reshape(bp, lp, A_HEADS, A_V))
        new_sd.append(sd)
        new_sh.append(jnp.swapaxes(sh, -1, -2))

        za, zb, zs, zc = _in_proj(xs, mod_s, w_in_l[l], ls, rope)
        oa = _attention(za, ls, l, diff_lambda, diff_norm, (ck, cv))
        prep = _delta_prep(zb, zs, ls, l, conv_w, prow, pcol, dconsts)
        obf, obb, _ = _delta_scan(prep, bs, ls, l, head_mask, state_delta)
        oc, _ = _hgrn_mixer(zc, ls, l, lb, hnorm, hconsts, sh0)
        xs = _tail(xs, oa, obf, obb, zb, oc, mod_s, l, bnorm, w_out_b, ln_g, ln_b, w_ffn_in_b, w_ffn_out_b, alpha)

    return (xp.reshape(bp, lp, d), xs.reshape(bs, ls, d), jnp.stack(new_k, 1), jnp.stack(new_v, 1),
            jnp.stack(new_sd, 1), jnp.stack(new_sh, 1))
```

```python
import functools
import math

import numpy as np
import jax
import jax.numpy as jnp
from jax import lax
from jax.experimental import pallas as pl
from jax.experimental.pallas import tpu as pltpu

F32 = jnp.float32
BF16 = jnp.bfloat16

A_HEADS = 4
A_QK = 64
A_V = 2 * A_QK
B_HEADS = 4
B_DK = 64
B_DV = 64
C_HEADS = 4
C_DK = 64
C_DV = 64
GRID_W = 64
ROPE_BASE = 10000.0
LN_EPS = 1e-5
RMS_EPS = 1e-6
L2_EPS = 1e-6

LANES = 128
TOK = 128
DELTA_CHUNK = 64
VMEM_LIMIT = 60 * 1024 * 1024


def _cparams(*sem):
    return pltpu.CompilerParams(dimension_semantics=sem, vmem_limit_bytes=VMEM_LIMIT)


def _dot(a, b):
    return jnp.dot(a, b, preferred_element_type=F32)


def _dot_nt(a, b):
    return lax.dot_general(a, b, (((1,), (1,)), ((), ())), preferred_element_type=F32)


def _bdot(a, b):
    return _dot(a.astype(BF16), b.astype(BF16))


def _split3(x):
    hi = x.astype(BF16)
    r = x - hi.astype(F32)
    mid = r.astype(BF16)
    lo = (r - mid.astype(F32)).astype(BF16)
    return hi, mid, lo


def _dot01(w01, x, terms=3):
    hi, mid, lo = _split3(x)
    if terms == 2:
        return _dot(w01, mid) + _dot(w01, hi)
    return (_dot(w01, lo) + _dot(w01, mid)) + _dot(w01, hi)


def _dot01_nt(x, w01):
    hi, mid, lo = _split3(x)
    return (_dot_nt(lo, w01) + _dot_nt(mid, w01)) + _dot_nt(hi, w01)


def _sigmoid(x):
    return 1.0 / (1.0 + jnp.exp(-x))


def _silu(x):
    return x * _sigmoid(x)


def _softplus(x):
    return jnp.maximum(x, 0.0) + jnp.log(1.0 + jnp.exp(-jnp.abs(x)))


def _lane_iota(shape):
    return lax.broadcasted_iota(jnp.int32, shape, len(shape) - 1)


def _row_iota(shape):
    return lax.broadcasted_iota(jnp.int32, shape, len(shape) - 2)


def _layer_norm(x, g, b):
    mu = jnp.mean(x, axis=-1, keepdims=True)
    xc = x - mu
    var = jnp.mean(xc * xc, axis=-1, keepdims=True)
    return xc * lax.rsqrt(var + LN_EPS) * g + b


def _mod_kernel(c_ref, w_ref, b_ref, o_ref):
    cs = _silu(c_ref[...])
    o_ref[0] = _bdot(cs, w_ref[0]) + b_ref[0]


def _modulation(cond, w_mod, b_mod):
    depth, d, n = w_mod.shape
    rows = cond.shape[0]
    tn = 1536 if n % 1536 == 0 else n
    return pl.pallas_call(
        _mod_kernel,
        grid=(depth, n // tn),
        in_specs=[
            pl.BlockSpec((rows, d), lambda l, j: (0, 0)),
            pl.BlockSpec((1, d, tn), lambda l, j: (l, 0, j)),
            pl.BlockSpec((1, 1, tn), lambda l, j: (l, 0, j)),
        ],
        out_specs=pl.BlockSpec((1, rows, tn), lambda l, j: (l, 0, j)),
        out_shape=jax.ShapeDtypeStruct((depth, rows, n), F32),
        compiler_params=_cparams("parallel", "parallel"),
        name="modulation",
    )(cond, w_mod, b_mod.reshape(depth, 1, n))


N_A = 3 * A_HEADS * A_V
N_B = 4 * B_HEADS * B_DK
N_S = 2 * LANES
N_C = 5 * C_HEADS * C_DK
N_Z = N_A + N_B + N_S + N_C


N_QK = A_HEADS * 2 * A_QK
Q_SCALE = (A_QK ** -0.5) * math.log2(math.e)


def _in_proj_kernel(x_ref, mod_ref, w_ref, *rest, d_model, rope):
    if rope:
        rc_ref, rs1_ref, rs2_ref, za_ref, zb_ref, zs_ref, zc_ref = rest
    else:
        _, _, zq_ref, ko_ref, vo_ref, zb_ref, zs_ref, zc_ref = rest
    m = mod_ref[0]
    shift = m[:, 0:d_model]
    scale = m[:, d_model:2 * d_model]
    h = (x_ref[...] * (1.0 + scale) + shift).astype(BF16)
    if rope:
        rc, rs1, rs2 = rc_ref[...], rs1_ref[...], rs2_ref[...]
        for g in range(N_A // LANES):
            blk = _dot(h, w_ref[:, g * LANES:(g + 1) * LANES])
            if g < 2 * N_QK // LANES:
                up = pltpu.roll(blk, LANES - 16, axis=1)
                dn = pltpu.roll(blk, 16, axis=1)
                blk = blk * rc + up * rs1 + dn * rs2
            if g < N_QK // LANES:
                blk = blk * Q_SCALE
            za_ref[:, g * LANES:(g + 1) * LANES] = blk.astype(za_ref.dtype)
    else:
        zq_ref[...] = _dot(h, w_ref[:, 0:N_QK])
        ko_ref[0, 0] = _dot(h, w_ref[:, N_QK:2 * N_QK])
        vo_ref[0, 0] = _dot(h, w_ref[:, 2 * N_QK:N_A])
    zb_ref[...] = _dot(h, w_ref[:, N_A:N_A + N_B])
    zs_ref[...] = _dot(h, w_ref[:, N_A + N_B:N_A + N_B + N_S])
    zc_ref[...] = _dot(h, w_ref[:, N_A + N_B + N_S:N_Z])


def _in_proj(x, mod, w, seq_len, rope_tabs=None, kv_out=None, layer=0):
    n, d = x.shape
    rope = rope_tabs is not None
    tm = min(512, seq_len) if rope else seq_len
    tiles_per_seq = seq_len // tm
    rows_per_mod = n // mod.shape[0]
    in_specs = [
        pl.BlockSpec((tm, d), lambda i: (i, 0)),
        pl.BlockSpec((1, 1, mod.shape[2]), lambda i: ((i * tm) // rows_per_mod, 0, 0)),
        pl.BlockSpec((d, N_Z), lambda i: (0, 0)),
    ]
    args = [x, mod, w]
    row = lambda w_: pl.BlockSpec((tm, w_), lambda i: (i, 0))
    tail_w = [N_B, N_S, N_C]
    if rope:
        in_specs += [pl.BlockSpec((tm, LANES), lambda i: (i % tiles_per_seq, 0))] * 3
        args += list(rope_tabs)
        out_specs = [row(N_A)] + [row(w_) for w_ in tail_w]
        out_shape = [jax.ShapeDtypeStruct((n, N_A), BF16)] + [jax.ShapeDtypeStruct((n, w_), F32) for w_ in tail_w]
        aliases = {}
    else:
        k_all, v_all = kv_out
        in_specs += [pl.BlockSpec(memory_space=pl.ANY)] * 2
        args += [k_all, v_all]
        kv_spec = lambda a: pl.BlockSpec((1, 1, tm, a.shape[3]), lambda i: (i, layer, 0, 0))
        out_specs = [row(N_QK), kv_spec(k_all), kv_spec(v_all)] + [row(w_) for w_ in tail_w]
        out_shape = ([jax.ShapeDtypeStruct((n, N_QK), F32), jax.ShapeDtypeStruct(k_all.shape, F32),
                      jax.ShapeDtypeStruct(v_all.shape, F32)]
                     + [jax.ShapeDtypeStruct((n, w_), F32) for w_ in tail_w])
        aliases = {3: 1, 4: 2}
    return pl.pallas_call(
        functools.partial(_in_proj_kernel, d_model=d, rope=rope),
        grid=(n // tm,),
        in_specs=in_specs,
        out_specs=out_specs,
        out_shape=out_shape,
        input_output_aliases=aliases,
        compiler_params=_cparams("parallel"),
        name="in_proj_rope" if rope else "in_proj",
    )(*args)


def _rope_tables(seq_len):
    half = A_QK // 2
    nf = half // 2
    pos = jnp.arange(seq_len)
    row = (pos // GRID_W).astype(F32)
    col = (pos % GRID_W).astype(F32)
    inv_freq = ROPE_BASE ** (-jnp.arange(nf, dtype=F32) / nf)
    ang_r = row[:, None] * inv_freq
    ang_c = col[:, None] * inv_freq
    cos64 = jnp.concatenate([jnp.cos(ang_r), jnp.cos(ang_r), jnp.cos(ang_c), jnp.cos(ang_c)], -1)
    zero = jnp.zeros_like(ang_r)
    s1_64 = jnp.concatenate([-jnp.sin(ang_r), zero, -jnp.sin(ang_c), zero], -1)
    s2_64 = jnp.concatenate([zero, jnp.sin(ang_r), zero, jnp.sin(ang_c)], -1)
    rep = LANES // A_QK
    return tuple(jnp.tile(t, (1, rep)) for t in (cos64, s1_64, s2_64))


def _attn_kernel(q_ref, k_ref, v_ref, *rest, lam_init, latent, sb, tq, seq_len):
    if latent:
        ck_ref, cv_ref, dl_ref, nrm_ref, o_ref = rest
    else:
        dl_ref, nrm_ref, o_ref = rest
    dl = dl_ref[0]
    lam = (jnp.exp(jnp.sum(dl[0:1] * dl[1:2], axis=1, keepdims=True))
           - jnp.exp(jnp.sum(dl[2:3] * dl[3:4], axis=1, keepdims=True)) + lam_init)
    lo = _lane_iota((tq, LANES)) < A_QK
    for sq in range(sb):
        for h in range(A_HEADS):
            hs = slice(h * LANES, (h + 1) * LANES)
            q = q_ref[sq * tq:(sq + 1) * tq, hs]
            if latent:
                k, v = k_ref[:, hs], v_ref[:, hs]
                zero = jnp.zeros_like(q)
            else:
                q = q * Q_SCALE
                k, v = k_ref[sq, 0, :, hs].astype(BF16), v_ref[sq, 0, :, hs].astype(BF16)
                zero = 0.0
            qq = jnp.concatenate([jnp.where(lo, q, zero), jnp.where(lo, zero, q)], axis=0).astype(BF16)
            s = _dot_nt(qq, k)
            m = jnp.max(s, axis=-1, keepdims=True)
            if latent:
                sc = _dot_nt(qq, ck_ref[0, 0, :, hs].astype(BF16))
                m = jnp.maximum(m, jnp.max(sc, axis=-1, keepdims=True))
            e = jnp.exp2(s - m)
            den = jnp.sum(e, axis=-1, keepdims=True)
            o2 = _dot(e.astype(BF16), v)
            if latent:
                ec = jnp.exp2(sc - m)
                den = den + jnp.sum(ec, axis=-1, keepdims=True)
                o2 = o2 + _dot(ec.astype(BF16), cv_ref[0, 0, :, hs].astype(BF16))
            r = 1.0 / den
            o = o2[:tq] * r[:tq] - o2[tq:] * (lam * r[tq:])
            o = o * lax.rsqrt(jnp.mean(o * o, axis=-1, keepdims=True) + RMS_EPS) * nrm_ref[0]
            o_ref[sq * tq:(sq + 1) * tq, hs] = o * (1.0 - lam_init)


def _attention(q_arr, seq_len, layer, diff_lambda, diff_norm, kv_all=None, ctx_kv=None):
    n = q_arr.shape[0]
    n_seq = n // seq_len
    latent = ctx_kv is not None
    lam_init = 0.8 - 0.6 * math.exp(-0.3 * layer)
    if latent:
        sb, tq = 1, min(128, seq_len)
        nq = seq_len // tq
        col = lambda c: pl.BlockSpec((seq_len, N_QK), lambda s, i: (s, c))
        ck, cv = ctx_kv
        cspec = lambda a: pl.BlockSpec((1, 1) + a.shape[2:], lambda s, i: (s, layer, 0, 0))
        in_specs = [pl.BlockSpec((tq, N_QK), lambda s, i: (s * nq + i, 0)), col(1), col(2), cspec(ck), cspec(cv)]
        args = [q_arr, q_arr, q_arr, ck, cv]
    else:
        sb, tq, nq = min(4, n_seq), seq_len, 1
        k_all, v_all = kv_all
        kspec = lambda a: pl.BlockSpec((sb, 1) + a.shape[2:], lambda s, i: (s, layer, 0, 0))
        in_specs = [pl.BlockSpec((sb * tq, N_QK), lambda s, i: (s, 0)), kspec(k_all), kspec(v_all)]
        args = [q_arr, k_all, v_all]
    in_specs += [
        pl.BlockSpec((1, 4, A_QK), lambda s, i: (layer, 0, 0)),
        pl.BlockSpec((1, 1, A_V), lambda s, i: (layer, 0, 0)),
    ]
    args += [diff_lambda, diff_norm.reshape(diff_norm.shape[0], 1, A_V)]
    return pl.pallas_call(
        functools.partial(_attn_kernel, lam_init=lam_init, latent=latent, sb=sb, tq=tq, seq_len=seq_len),
        grid=(n_seq // sb, nq),
        in_specs=in_specs,
        out_specs=pl.BlockSpec((sb * tq, A_HEADS * A_V), lambda s, i: (s * nq + i, 0)),
        out_shape=jax.ShapeDtypeStruct((n, A_HEADS * A_V), F32),
        compiler_params=_cparams("parallel", "arbitrary"),
        name="diff_attn_latent" if latent else "diff_attn",
    )(*args)


def _head_sum(x):
    lane = _lane_iota(x.shape)
    lo = lane < 64
    s0 = jnp.sum(jnp.where(lo, x, 0.0), axis=-1, keepdims=True)
    s1 = jnp.sum(jnp.where(lo, 0.0, x), axis=-1, keepdims=True)
    return jnp.where(lo, s0, s1)


def _pair_blockdiag(a, b):
    z = jnp.zeros_like(a)
    return jnp.concatenate([jnp.concatenate([a, z], axis=1), jnp.concatenate([z, b], axis=1)], axis=0)


def _gated_head_norm(o, norm_row, gate):
    parts = []
    for hp in range(o.shape[1] // LANES):
        blk = o[:, hp * LANES:(hp + 1) * LANES]
        parts.append(blk * lax.rsqrt(_head_sum(blk * blk) * (1.0 / 64.0) + RMS_EPS))
    return jnp.concatenate(parts, axis=1) * norm_row * _silu(gate)


DELTA_MERGES = (4, 8, 16, 32, 64)
M_CAUSAL, M_STRICT, M_PAIR, M_MERGE0 = 0, 1, 2, 3
M_EYE = M_MERGE0 + len(DELTA_MERGES)
M_HEADS = M_EYE + 1


def _delta_consts():
    t = np.arange(TOK)[:, None]
    s = np.arange(TOK)[None, :]

    def same(n):
        return (t // n) == (s // n)

    masks, cums = [], []
    for d in range(2):
        before = (s <= t) if d == 0 else (s >= t)
        strict = (s < t) if d == 0 else (s > t)
        causal = same(DELTA_CHUNK) & before
        st = same(DELTA_CHUNK) & strict
        rows = [causal, st, st & same(2)]
        rows += [st & same(n) & ~same(n // 2) for n in DELTA_MERGES]
        rows += [t == s, same(64)]
        masks.append(np.stack(rows))
        cums.append(np.concatenate([causal, same(DELTA_CHUNK) & ~before, same(DELTA_CHUNK)], 0))
    return (jnp.asarray(np.stack(masks), F32), jnp.asarray(np.stack(cums), BF16))


def _delta_prep_kernel(z_ref, zp_ref, zn_ref, sc_ref, sr_ref, cw_ref, prow_ref, pcol_ref, msk_ref, cum_ref,
                       u_ref, w_ref, qd_ref, qkm_ref, kdt_ref, gl_ref, *, blocks_per_seq, g_tiles):
    jloc = pl.program_id(0) % blocks_per_seq
    rb = g_tiles * TOK
    nqk = B_HEADS * B_DK
    x = z_ref[...]
    cw = cw_ref[0]
    prev = jnp.where(jloc > 0, zp_ref[7:8, :], 0.0)
    nxt = jnp.where(jloc < blocks_per_seq - 1, zn_ref[0:1, :], 0.0)
    row = _row_iota(x.shape)
    dn = jnp.where(row == 0, prev, pltpu.roll(x, 1, axis=0))
    up = jnp.where(row == rb - 1, nxt, pltpu.roll(x, rb - 1, axis=0))
    y = _silu(cw[0:1] * dn + cw[1:2] * x + cw[2:3] * up)

    lo = _lane_iota((TOK, LANES)) < 64
    hi = jnp.logical_not(lo)
    units = [(g, p) for g in range(g_tiles) for p in range(B_HEADS // 2)]
    chains = [(g, p, d, hh) for (g, p) in units for d in range(2) for hh in range(2)]

    q, k, v, kk, qk, gcol, grow = {}, {}, {}, {}, {}, {}, {}
    for (g, p) in units:
        rs = slice(g * TOK, (g + 1) * TOK)
        qs = y[rs, p * LANES:(p + 1) * LANES]
        ks = y[rs, nqk + p * LANES:nqk + (p + 1) * LANES]
        v[g, p] = y[rs, 2 * nqk + p * LANES:2 * nqk + (p + 1) * LANES]
        q[g, p] = qs * lax.rsqrt(_head_sum(qs * qs) + L2_EPS) * (B_DK ** -0.5)
        k[g, p] = ks * lax.rsqrt(_head_sum(ks * ks) + L2_EPS)
    for (g, p) in units:
        k16 = k[g, p].astype(BF16)
        for hh, sel in enumerate((lo, hi)):
            kk[g, p, hh] = _dot_nt(jnp.where(sel, k[g, p], 0.0).astype(BF16), k16)
            qk[g, p, hh] = _dot_nt(jnp.where(sel, q[g, p], 0.0).astype(BF16), k16)
    for (g, p) in units:
        prow = prow_ref[0, p]
        pcol = pcol_ref[0, p]
        xs = sc_ref[g * TOK:(g + 1) * TOK, p * LANES:(p + 1) * LANES]
        lane = _lane_iota(xs.shape)
        gcol[g, p] = jnp.where(lane < 4, _sigmoid(xs), -jnp.exp(prow[0:1]) * _softplus(xs + prow[1:2]))
        xr = sr_ref[p, g]
        rowi = _row_iota(xr.shape)
        grow[g, p] = jnp.where(rowi < 4, _sigmoid(xr), -jnp.exp(pcol[:, 0:1]) * _softplus(xr + pcol[:, 1:2]))
    cs, br = {}, {}
    for (g, p) in units:
        for d in range(2):
            cs[g, p, d] = _dot01(cum_ref[d], gcol[g, p])
            br[g, p, d] = _dot01_nt(grow[g, p], cum_ref[d, 0:TOK])
    beta, bcol, dec, m, x = {}, {}, {}, {}, {}
    for ch in chains:
        g, p, d, hh = ch
        cb, cg = d * 2 + hh, 4 + d * 2 + hh
        beta[ch] = gcol[g, p][:, cb:cb + 1]
        bcol[ch] = cs[g, p, d][0:TOK, cg:cg + 1]
        dec[ch] = jnp.exp(jnp.minimum(bcol[ch] - br[g, p, d][cg:cg + 1, :], 0.0)) * msk_ref[d, M_CAUSAL]
        m[ch] = kk[g, p, hh] * beta[ch] * dec[ch] * msk_ref[d, M_STRICT]
        x[ch] = msk_ref[d, M_EYE] - m[ch] * msk_ref[d, M_PAIR]
    for lvl in range(len(DELTA_MERGES)):
        yv = {ch: _bdot(m[ch] * msk_ref[ch[2], M_MERGE0 + lvl], x[ch]) for ch in chains}
        x = {ch: x[ch] - _bdot(x[ch], yv[ch]) for ch in chains}
    for (g, p) in units:
        rs = slice(g * TOK, (g + 1) * TOK)
        for d in range(2):
            sol = None
            eq, ek, gl, qkm = [], [], [], []
            for hh, sel in enumerate((lo, hi)):
                ch = (g, p, d, hh)
                cg = 4 + d * 2 + hh
                eb = jnp.exp(bcol[ch])
                rhs = jnp.concatenate([jnp.where(sel, v[g, p] * beta[ch], 0.0),
                                       jnp.where(sel, k[g, p] * (beta[ch] * eb), 0.0)], axis=1)
                part = _bdot(x[ch], rhs)
                sol = part if sol is None else sol + part
                qkm.append((qk[g, p, hh] * dec[ch]).astype(BF16))
                eq.append(eb)
                ek.append(jnp.exp(cs[g, p, d][TOK:2 * TOK, cg:cg + 1]))
                gl.append(jnp.exp(cs[g, p, d][2 * TOK:3 * TOK, cg:cg + 1]))
            u_ref[d, p, rs, :] = sol[:, 0:LANES]
            w_ref[d, p, rs, :] = sol[:, LANES:2 * LANES].astype(BF16)
            qd_ref[d, p, rs, :] = (q[g, p] * jnp.where(lo, eq[0], eq[1])).astype(BF16)
            qkm_ref[d, p, rs, :] = jnp.concatenate(qkm, axis=1)
            kd = k[g, p] * jnp.where(lo, ek[0], ek[1])
            kdt_ref[d, p, rs, :] = kd.T.astype(BF16)
            glf = jnp.where(lo, gl[0], gl[1])
            gl_ref[d, p, g * 16:(g + 1) * 16, :] = jnp.concatenate([glf[0:8], glf[64:72]], axis=0)


def _delta_prep(zb, zs, seq_len, layer, conv_w, prow, pcol, consts):
    n = zb.shape[0]
    pairs = B_HEADS // 2
    g_tiles = 2
    rb = g_tiles * TOK
    blocks_per_seq = seq_len // rb
    n_blocks = n // rb
    nconv = conv_w.shape[2]
    msk, cum = consts
    zs_rows = zs.reshape(n // TOK, TOK, pairs, LANES)[..., :8].transpose(2, 0, 3, 1)
    last8 = n // 8 - 1
    out_w = [LANES, LANES, LANES, 2 * LANES, LANES]
    out_t = [F32, BF16, BF16, BF16, BF16]
    return pl.pallas_call(
        functools.partial(_delta_prep_kernel, blocks_per_seq=blocks_per_seq, g_tiles=g_tiles),
        grid=(n_blocks,),
        in_specs=[
            pl.BlockSpec((rb, nconv), lambda i: (i, 0)),
            pl.BlockSpec((8, nconv), lambda i: (jnp.maximum(i * (rb // 8) - 1, 0), 0)),
            pl.BlockSpec((8, nconv), lambda i: (jnp.minimum((i + 1) * (rb // 8), last8), 0)),
            pl.BlockSpec((rb, pairs * LANES), lambda i: (i, 0)),
            pl.BlockSpec((pairs, g_tiles, 8, TOK), lambda i: (0, i, 0, 0)),
            pl.BlockSpec((1, 3, nconv), lambda i: (layer, 0, 0)),
            pl.BlockSpec((1, pairs, 8, LANES), lambda i: (layer, 0, 0, 0)),
            pl.BlockSpec((1, pairs, 8, LANES), lambda i: (layer, 0, 0, 0)),
            pl.BlockSpec(msk.shape, lambda i: (0, 0, 0, 0)),
            pl.BlockSpec(cum.shape, lambda i: (0, 0, 0)),
        ],
        out_specs=[pl.BlockSpec((2, pairs, rb, w_), lambda i: (0, 0, i, 0)) for w_ in out_w]
        + [pl.BlockSpec((2, pairs, g_tiles * 16, LANES), lambda i: (0, 0, i, 0))],
        out_shape=[jax.ShapeDtypeStruct((2, pairs, n, w_), t_) for w_, t_ in zip(out_w, out_t)]
        + [jax.ShapeDtypeStruct((2, pairs, n // TOK * 16, LANES), F32)],
        compiler_params=_cparams("parallel"),
        name="delta_prep",
    )(zb, zb, zb, zs, zs_rows, conv_w, prow, pcol, msk, cum)


def _delta_scan_kernel(*refs, sb, tb, has_s0):
    fwd, bwd = refs[0:6], refs[6:12]
    hm_ref = refs[12]
    if has_s0:
        s0_ref, of_ref, ob_ref, sout_ref, s_s = refs[13:]
    else:
        of_ref, ob_ref, sout_ref, s_s = refs[13:]
    t = pl.program_id(1)
    n_t = pl.num_programs(1)
    n_g = tb // TOK
    pairs = B_HEADS // 2
    chains = [(d, sq, p) for d in range(2) for sq in range(sb) for p in range(pairs)]

    @pl.when(t == 0)
    def _():
        for (d, sq, p) in chains:
            if has_s0:
                s_s[d, sq, p] = _pair_blockdiag(s0_ref[sq, 0, d, 2 * p], s0_ref[sq, 0, d, 2 * p + 1])
            else:
                s_s[d, sq, p] = jnp.zeros((LANES, LANES), F32)

    lane64 = _lane_iota((DELTA_CHUNK, LANES)) < 64
    zpad = jnp.zeros((DELTA_CHUNK, LANES), BF16)

    def body(gi, carry):
        for step in range(2):
            tmp = {}
            for ch in chains:
                d, sq, p = ch
                u_r, w_r = (fwd, bwd)[d][0:2]
                g = gi if d == 0 else n_g - 1 - gi
                c = step if d == 0 else 1 - step
                rc = pl.multiple_of(g * TOK + c * DELTA_CHUNK, DELTA_CHUNK)
                s = s_s[d, sq, p]
                s16 = s.astype(BF16)
                vn = u_r[0, p, sq, pl.ds(rc, DELTA_CHUNK), :] - _dot(w_r[0, p, sq, pl.ds(rc, DELTA_CHUNK), :], s16)
                tmp[ch] = (s, s16, vn, g, c, rc)
            for ch in chains:
                d, sq, p = ch
                _, _, qd_r, qkm_r, kdt_r, gl_r = (fwd, bwd)[d]
                o_r = (of_ref, ob_ref)[d]
                s, s16, vn, g, c, rc = tmp[ch]
                r0 = pl.multiple_of(g * TOK, TOK)
                v0 = jnp.where(lane64, vn, 0.0).astype(BF16)
                v1 = jnp.where(lane64, 0.0, vn).astype(BF16)
                vnb = vn.astype(BF16)
                if c == 0:
                    vext = jnp.concatenate([vnb, zpad], axis=0)
                    v2 = jnp.concatenate([v0, zpad, v1, zpad], axis=0)
                else:
                    vext = jnp.concatenate([zpad, vnb], axis=0)
                    v2 = jnp.concatenate([zpad, v0, zpad, v1], axis=0)
                o = (_dot(qd_r[0, p, sq, pl.ds(rc, DELTA_CHUNK), :], s16)
                     + _dot(qkm_r[0, p, sq, pl.ds(rc, DELTA_CHUNK), :], v2))
                o_r[sq, pl.ds(rc, DELTA_CHUNK), p * LANES:(p + 1) * LANES] = o
                glr = gl_r[0, p, sq, pl.ds(pl.multiple_of(g * 16 + c * 8, 8), 8), :][0:1]
                s_s[d, sq, p] = s * glr + _dot(kdt_r[0, p, sq, pl.ds(r0, TOK), :], vext) * hm_ref[...]
        return carry

    lax.fori_loop(0, n_g, body, 0)

    @pl.when(t == n_t - 1)
    def _():
        for (d, sq, p) in chains:
            s = s_s[d, sq, p]
            sout_ref[sq, d, 2 * p] = s[0:64, 0:64]
            sout_ref[sq, d, 2 * p + 1] = s[64:128, 64:128]


def _delta_scan(prep, n_seq, seq_len, layer, head_mask, s0):
    pairs = B_HEADS // 2
    has_s0 = s0 is not None
    sb = n_seq if n_seq <= 2 else 8
    tb = min(seq_len, 512)
    n_t = seq_len // tb
    arrs = [a.reshape(2, pairs, n_seq, a.shape[2] // n_seq, a.shape[3]) for a in prep]

    def specs(d):
        tmap = (lambda s, t: (d, 0, s, t, 0)) if d == 0 else (lambda s, t: (d, 0, s, n_t - 1 - t, 0))
        return [pl.BlockSpec((1, pairs, sb, a.shape[3] // n_t, a.shape[4]), tmap) for a in arrs]

    in_specs = specs(0) + specs(1) + [pl.BlockSpec((TOK, LANES), lambda s, t: (0, 0))]
    args = arrs + arrs + [head_mask]
    if has_s0:
        in_specs.append(pl.BlockSpec((sb, 1, 2, B_HEADS, B_DK, B_DV), lambda s, t: (s, layer, 0, 0, 0, 0)))
        args.append(s0)
    width = B_HEADS * B_DV
    o_f, o_b, s_out = pl.pallas_call(
        functools.partial(_delta_scan_kernel, sb=sb, tb=tb, has_s0=has_s0),
        grid=(n_seq // sb, n_t),
        in_specs=in_specs,
        out_specs=[
            pl.BlockSpec((sb, tb, width), lambda s, t: (s, t, 0)),
            pl.BlockSpec((sb, tb, width), lambda s, t: (s, n_t - 1 - t, 0)),
            pl.BlockSpec((sb, 2, B_HEADS, B_DK, B_DV), lambda s, t: (s, 0, 0, 0, 0)),
        ],
        out_shape=[
            jax.ShapeDtypeStruct((n_seq, seq_len, width), F32),
            jax.ShapeDtypeStruct((n_seq, seq_len, width), F32),
            jax.ShapeDtypeStruct((n_seq, 2, B_HEADS, B_DK, B_DV), F32),
        ],
        scratch_shapes=[pltpu.VMEM((2, sb, pairs, LANES, LANES), F32)],
        compiler_params=_cparams("parallel", "arbitrary"),
        name="delta_scan_s0" if has_s0 else "delta_scan",
    )(*args)
    n = n_seq * seq_len
    return o_f.reshape(n, width), o_b.reshape(n, width), s_out


HGRN_LEVELS = 7
HGRN_WROWS = (HGRN_LEVELS + 2) * TOK + 8


def _hgrn_consts():
    t = np.arange(TOK)
    masks = [t[:, None] == t[None, :]]
    for lvl in range(1, HGRN_LEVELS + 1):
        n = 1 << lvl
        masks.append((t[:, None] // n) == (t[None, :] // n))
    ws = []
    for d in range(2):
        tau = t if d == 0 else TOK - 1 - t
        tt, ti = tau[:, None], tau[None, :]
        blocks = []
        for lvl in range(1, HGRN_LEVELS + 1):
            n = 1 << lvl
            piv = (tau - tau % n + n // 2 - 1)[:, None]
            upper = ((tau % n) >= n // 2)[:, None]
            blocks.append(np.where(upper, (ti > piv) & (ti <= tt), (ti > tt) & (ti <= piv)))
        blocks.append(ti <= tt)
        blocks.append(ti > tt)
        blocks.append(np.ones((8, TOK), bool))
        ws.append(np.concatenate(blocks, 0))
    return jnp.asarray(np.stack(masks), F32), jnp.asarray(np.stack(ws), BF16)


def _hgrn_prep_kernel(z_ref, lb_ref, msk_ref, w_ref, intra_ref, qd_ref, kv_ref, gl_ref, *, layer, g_tiles):
    pairs = C_HEADS // 2
    nk = C_HEADS * C_DK
    lo = _lane_iota((TOK, LANES)) < 64
    row = _row_iota((TOK, LANES))
    units = [(g, p) for g in range(g_tiles) for p in range(pairs)]
    chains = [(g, p, d) for (g, p) in units for d in range(2)]

    lbs = {}
    for p in range(pairs):
        for d in range(2):
            x = lb_ref[p, d]
            e = jnp.exp(x - jnp.max(x, axis=0, keepdims=True))
            sm = e / jnp.sum(e, axis=0, keepdims=True)
            if layer > 0:
                lbs[p, d] = jnp.sum(sm[1:layer + 1], axis=0, keepdims=True)
            else:
                lbs[p, d] = jnp.zeros((1, LANES), F32)

    def stack_heads(x):
        return jnp.concatenate([jnp.where(lo, x, 0.0), jnp.where(lo, 0.0, x)], axis=0).astype(BF16)

    def both_heads(mask):
        return jnp.concatenate([mask, mask], axis=1)

    q, vt, key, ex, acc = {}, {}, {}, {}, {}
    for (g, p) in units:
        rs = slice(g * TOK, (g + 1) * TOK)
        q[g, p] = _silu(z_ref[rs, p * LANES:(p + 1) * LANES])
        vt[g, p] = z_ref[rs, 3 * nk + p * LANES:3 * nk + (p + 1) * LANES].T
    for g in range(g_tiles):
        for d in range(2):
            lg = []
            for p in range(pairs):
                f = z_ref[g * TOK:(g + 1) * TOK, (1 + d) * nk + p * LANES:(1 + d) * nk + (p + 1) * LANES]
                forget = lbs[p, d] + (1.0 - lbs[p, d]) * _sigmoid(f)
                key[g, p, d] = 1.0 - forget
                lg.append(jnp.log(forget))
            spans = _dot01(w_ref[d], jnp.concatenate(lg, axis=1), terms=2)
            for p in range(pairs):
                ex[g, p, d] = spans[:, p * LANES:(p + 1) * LANES]
    for ch in chains:
        acc[ch] = _dot_nt(key[ch].astype(BF16), stack_heads(q[ch[0], ch[1]])) * both_heads(msk_ref[0])
    for lvl in range(1, HGRN_LEVELS + 1):
        for ch in chains:
            g, p, d = ch
            tau = row if d == 0 else TOK - 1 - row
            e = jnp.exp(ex[ch][(lvl - 1) * TOK:lvl * TOK])
            up = (tau & (1 << (lvl - 1))) != 0
            qt = jnp.where(up, q[g, p] * e, 0.0)
            kt = jnp.where(up, 0.0, key[ch] * e)
            a = _dot_nt(kt.astype(BF16), stack_heads(qt))
            if lvl < HGRN_LEVELS:
                a = a * both_heads(msk_ref[lvl])
            acc[ch] = acc[ch] + a
    base = HGRN_LEVELS * TOK
    for ch in chains:
        g, p, d = ch
        rs = slice(g * TOK, (g + 1) * TOK)
        vv = vt[g, p]
        intra_t = (_dot(jnp.where(row < 64, vv, 0.0).astype(BF16), acc[ch][:, 0:LANES].astype(BF16))
                   + _dot(jnp.where(row < 64, 0.0, vv).astype(BF16), acc[ch][:, LANES:2 * LANES].astype(BF16)))
        intra_ref[d, p, rs, :] = intra_t.T
        qd_ref[d, p, rs, :] = (q[g, p] * jnp.exp(ex[ch][base:base + TOK])).astype(BF16)
        kd = (key[ch] * jnp.exp(ex[ch][base + TOK:base + 2 * TOK])).astype(BF16)
        kv_ref[d, p, rs, :] = _dot(vv.astype(BF16), kd) * msk_ref[6]
        gl_ref[d, p, g * 8:(g + 1) * 8, :] = jnp.exp(ex[ch][base + 2 * TOK:base + 2 * TOK + 8])


def _hgrn_prep(zc, layer, lb, consts):
    n = zc.shape[0]
    pairs = C_HEADS // 2
    g_tiles = 2
    rb = g_tiles * TOK
    msk, wst = consts
    n_in = 4 * C_HEADS * C_DK
    return pl.pallas_call(
        functools.partial(_hgrn_prep_kernel, layer=layer, g_tiles=g_tiles),
        grid=(n // rb,),
        in_specs=[
            pl.BlockSpec((rb, n_in), lambda i: (i, 0)),
            pl.BlockSpec(lb.shape, lambda i: (0, 0, 0, 0)),
            pl.BlockSpec(msk.shape, lambda i: (0, 0, 0)),
            pl.BlockSpec(wst.shape, lambda i: (0, 0, 0)),
        ],
        out_specs=[pl.BlockSpec((2, pairs, rb, LANES), lambda i: (0, 0, i, 0))] * 3
        + [pl.BlockSpec((2, pairs, g_tiles * 8, LANES), lambda i: (0, 0, i, 0))],
        out_shape=[jax.ShapeDtypeStruct((2, pairs, n, LANES), t_) for t_ in (F32, BF16, F32)]
        + [jax.ShapeDtypeStruct((2, pairs, n // TOK * 8, LANES), F32)],
        compiler_params=_cparams("parallel"),
        name="hgrn_prep",
    )(zc, lb, msk, wst)


def _hgrn_scan_kernel(*refs, sb, tb, has_s0):
    fwd, bwd = refs[0:4], refs[4:8]
    if has_s0:
        s0_ref, of_ref, ob_ref, sout_ref, s_s = refs[8:]
    else:
        of_ref, ob_ref, sout_ref, s_s = refs[8:]
    t = pl.program_id(1)
    n_t = pl.num_programs(1)
    n_g = tb // TOK
    pairs = C_HEADS // 2
    chains = [(d, sq, p) for d in range(2) for sq in range(sb) for p in range(pairs)]

    @pl.when(t == 0)
    def _():
        for (d, sq, p) in chains:
            if has_s0:
                s_s[d, sq, p] = _pair_blockdiag(s0_ref[sq, 0, d, 2 * p], s0_ref[sq, 0, d, 2 * p + 1])
            else:
                s_s[d, sq, p] = jnp.zeros((LANES, LANES), F32)

    def body(gi, carry):
        for (d, sq, p) in chains:
            intra_r, qd_r, kv_r, gl_r = (fwd, bwd)[d]
            o_r = (of_ref, ob_ref)[d]
            g = gi if d == 0 else n_g - 1 - gi
            r0 = pl.multiple_of(g * TOK, TOK)
            st = s_s[d, sq, p]
            o = intra_r[0, p, sq, pl.ds(r0, TOK), :] + _dot_nt(qd_r[0, p, sq, pl.ds(r0, TOK), :], st.astype(BF16))
            o_r[sq, pl.ds(r0, TOK), p * LANES:(p + 1) * LANES] = o
            glr = gl_r[0, p, sq, pl.ds(pl.multiple_of(g * 8, 8), 8), :][0:1]
            s_s[d, sq, p] = st * glr + kv_r[0, p, sq, pl.ds(r0, TOK), :]
        return carry

    lax.fori_loop(0, n_g, body, 0)

    @pl.when(t == n_t - 1)
    def _():
        for (d, sq, p) in chains:
            s = s_s[d, sq, p]
            sout_ref[sq, d, 2 * p] = s[0:64, 0:64]
            sout_ref[sq, d, 2 * p + 1] = s[64:128, 64:128]


def _hgrn_scan(prep, n_seq, seq_len, layer, s0t):
    pairs = C_HEADS // 2
    has_s0 = s0t is not None
    sb = n_seq if n_seq <= 2 else 8
    tb = min(seq_len, 512)
    n_t = seq_len // tb
    arrs = [a.reshape(2, pairs, n_seq, a.shape[2] // n_seq, a.shape[3]) for a in prep]

    def specs(d):
        tmap = (lambda s, t: (d, 0, s, t, 0)) if d == 0 else (lambda s, t: (d, 0, s, n_t - 1 - t, 0))
        return [pl.BlockSpec((1, pairs, sb, a.shape[3] // n_t, a.shape[4]), tmap) for a in arrs]

    in_specs = specs(0) + specs(1)
    args = arrs + arrs
    if has_s0:
        in_specs.append(pl.BlockSpec((sb, 1, 2, C_HEADS, C_DV, C_DK), lambda s, t: (s, layer, 0, 0, 0, 0)))
        args.append(s0t)
    width = C_HEADS * C_DV
    o_f, o_b, s_out = pl.pallas_call(
        functools.partial(_hgrn_scan_kernel, sb=sb, tb=tb, has_s0=has_s0),
        grid=(n_seq // sb, n_t),
        in_specs=in_specs,
        out_specs=[
            pl.BlockSpec((sb, tb, width), lambda s, t: (s, t, 0)),
            pl.BlockSpec((sb, tb, width), lambda s, t: (s, n_t - 1 - t, 0)),
            pl.BlockSpec((sb, 2, C_HEADS, C_DV, C_DK), lambda s, t: (s, 0, 0, 0, 0)),
        ],
        out_shape=[
            jax.ShapeDtypeStruct((n_seq, seq_len, width), F32),
            jax.ShapeDtypeStruct((n_seq, seq_len, width), F32),
            jax.ShapeDtypeStruct((n_seq, 2, C_HEADS, C_DV, C_DK), F32),
        ],
        scratch_shapes=[pltpu.VMEM((2, sb, pairs, LANES, LANES), F32)],
        compiler_params=_cparams("parallel", "arbitrary"),
        name="hgrn_scan_s0" if has_s0 else "hgrn_scan",
    )(*args)
    n = n_seq * seq_len
    return o_f.reshape(n, width), o_b.reshape(n, width), s_out


FFN_SPLIT = 2


def _tail_kernel(x_ref, oa_ref, obf_ref, obb_ref, bg_ref, ocf_ref, ocb_ref, cg_ref, mod_ref, bn_ref, cn_ref,
                 wo_ref, lng_ref, lnb_ref, wi_ref, wd_ref, o_ref, *, d_model, d_ff, alpha):
    m = mod_ref[0]
    na, nb = oa_ref.shape[1], obf_ref.shape[1]
    ob = _gated_head_norm(obf_ref[...] + obb_ref[...], bn_ref[0], bg_ref[...])
    oc = _gated_head_norm(ocf_ref[...] + ocb_ref[...], cn_ref[0], cg_ref[...])
    y = _bdot(oa_ref[...], wo_ref[0:na]) + _bdot(ob, wo_ref[na:na + nb]) + _bdot(oc, wo_ref[na + nb:])
    x1 = _layer_norm(alpha * x_ref[...] + m[:, 2 * d_model:3 * d_model] * y, lng_ref[0, 0:1], lnb_ref[0, 0:1])
    h = (x1 * (1.0 + m[:, 4 * d_model:5 * d_model]) + m[:, 3 * d_model:4 * d_model]).astype(BF16)
    ck = d_ff // FFN_SPLIT
    acc = None
    for c0 in range(0, d_ff, ck):
        gt = _dot(h, wi_ref[:, c0:c0 + ck])
        up = _dot(h, wi_ref[:, d_ff + c0:d_ff + c0 + ck])
        part = _dot((_silu(gt) * up).astype(BF16), wd_ref[c0:c0 + ck, :])
        acc = part if acc is None else acc + part
    o_ref[...] = _layer_norm(alpha * x1 + m[:, 5 * d_model:6 * d_model] * acc, lng_ref[0, 1:2], lnb_ref[0, 1:2])


def _tail(x, oa, obf, obb, zb, ocf, ocb, zc, mod, layer, bnorm, cnorm, w_out, ln_g, ln_b, w_ffn_in, w_ffn_out,
          alpha):
    n, d = x.shape
    tm = 512
    d_ff = w_ffn_out.shape[1]
    rows_per_mod = n // mod.shape[0]
    nb, nc = obf.shape[1], ocf.shape[1]
    row = lambda w_: pl.BlockSpec((tm, w_), lambda i: (i, 0))
    last_cols = lambda a, w_: pl.BlockSpec((tm, w_), lambda i: (i, a.shape[1] // w_ - 1))
    per_layer = lambda w_: pl.BlockSpec((1, 1, w_), lambda i: (layer, 0, 0))
    resident = lambda a: pl.BlockSpec((None,) + a.shape[1:], lambda i: (layer, 0, 0), pipeline_mode=pl.Buffered(1))
    return pl.pallas_call(
        functools.partial(_tail_kernel, d_model=d, d_ff=d_ff, alpha=alpha),
        grid=(n // tm,),
        in_specs=[
            row(d), row(oa.shape[1]),
            row(nb), row(nb), last_cols(zb, nb),
            row(nc), row(nc), last_cols(zc, nc),
            pl.BlockSpec((1, 1, mod.shape[2]), lambda i: ((i * tm) // rows_per_mod, 0, 0)),
            per_layer(nb), per_layer(nc),
            resident(w_out),
            pl.BlockSpec((1, 2, d), lambda i: (layer, 0, 0)),
            pl.BlockSpec((1, 2, d), lambda i: (layer, 0, 0)),
            resident(w_ffn_in), resident(w_ffn_out),
        ],
        out_specs=pl.BlockSpec((tm, d), lambda i: (i, 0)),
        out_shape=jax.ShapeDtypeStruct((n, d), F32),
        compiler_params=_cparams("parallel"),
        name="out_proj_ffn",
    )(x, oa, obf, obb, zb, ocf, ocb, zc, mod, bnorm, cnorm, w_out, ln_g, ln_b, w_ffn_in, w_ffn_out)


def _pair_scalar_order(p):
    return [d * B_HEADS + 2 * p + hh for d in range(2) for hh in range(2)]


def _layout_w_in(w_in):
    depth, d, _ = w_in.shape
    a_end = N_A
    b_end = a_end + N_B
    beta0 = b_end
    dec0 = beta0 + 2 * B_HEADS
    c0 = dec0 + 2 * B_HEADS
    parts = [w_in[:, :, :b_end]]
    for p in range(B_HEADS // 2):
        order = _pair_scalar_order(p)
        cols = [beta0 + i for i in order] + [dec0 + i for i in order]
        parts.append(w_in[:, :, np.asarray(cols)])
        parts.append(jnp.zeros((depth, d, LANES - len(cols)), w_in.dtype))
    parts.append(w_in[:, :, c0:])
    return jnp.concatenate(parts, axis=-1).astype(BF16)


def _delta_params(delta_a_log, delta_dt_bias):
    depth = delta_a_log.shape[0]
    pairs = B_HEADS // 2
    al = delta_a_log.reshape(depth, 2 * B_HEADS)
    dt = delta_dt_bias.reshape(depth, 2 * B_HEADS)
    prow = jnp.zeros((depth, pairs, 8, LANES), F32)
    pcol = jnp.zeros((depth, pairs, 8, LANES), F32)
    for p in range(pairs):
        order = np.asarray(_pair_scalar_order(p))
        prow = prow.at[:, p, 0, 4:8].set(al[:, order]).at[:, p, 1, 4:8].set(dt[:, order])
        pcol = pcol.at[:, p, 4:8, 0].set(al[:, order]).at[:, p, 4:8, 1].set(dt[:, order])
    return prow, pcol


def kernel(x_prompt, x_sample, cache_attn_k, cache_attn_v, state_delta, state_hgrn, c, c_ctx, w_mod, b_mod, w_in, conv_w, delta_a_log, delta_dt_bias, delta_norm, hgrn_lb, hgrn_norm, diff_lambda, diff_norm, w_out, ln_g, ln_b, w_ffn_in, w_ffn_out):
    depth, d, _ = w_in.shape
    bp, lp, _ = x_prompt.shape
    bs, ls, _ = x_sample.shape
    alpha = (2 * depth) ** 0.25

    cond_rows = 8 * ((1 + bs + 7) // 8)
    cond = jnp.concatenate([c_ctx[None], c, jnp.zeros((cond_rows - 1 - bs, d), F32)], axis=0)
    mod = _modulation(cond, w_mod, b_mod)

    w_in_l = _layout_w_in(w_in)
    w_out_b = w_out.astype(BF16)
    w_ffn_in_b = w_ffn_in.astype(BF16)
    w_ffn_out_b = w_ffn_out.astype(BF16)
    prow, pcol = _delta_params(delta_a_log, delta_dt_bias)
    bnorm = jnp.tile(delta_norm, (1, B_HEADS)).reshape(depth, 1, B_HEADS * B_DV)
    cnorm = jnp.tile(hgrn_norm, (1, C_HEADS)).reshape(depth, 1, C_HEADS * C_DV)
    lb = hgrn_lb.reshape(2, depth, C_HEADS // 2, LANES).transpose(2, 0, 1, 3)
    rope = _rope_tables(ls)
    dconsts = _delta_consts()
    head_mask = dconsts[0][0, M_HEADS]
    hconsts = _hgrn_consts()
    past = cache_attn_k.shape[2]
    ck = cache_attn_k.reshape(bs, depth, past, A_HEADS * 2 * A_QK)
    cv = cache_attn_v.reshape(bs, depth, past, A_HEADS * A_V)
    sh0 = jnp.swapaxes(state_hgrn, -1, -2)
    tails = (bnorm, cnorm, w_out_b, ln_g, ln_b, w_ffn_in_b, w_ffn_out_b, alpha)

    xp = x_prompt.reshape(bp * lp, d)
    xs = x_sample.reshape(bs * ls, d)
    k_all = jnp.zeros((bp, depth, lp, A_HEADS * 2 * A_QK), F32)
    v_all = jnp.zeros((bp, depth, lp, A_HEADS * A_V), F32)
    new_sd, new_sh = [], []
    for l in range(depth):
        mod_p = mod[l, 0:1][:, None, :]
        mod_s = mod[l, 1:1 + bs][:, None, :]

        zq, k_all, v_all, zb, zs, zc = _in_proj(xp, mod_p, w_in_l[l], lp, kv_out=(k_all, v_all), layer=l)
        oa = _attention(zq, lp, l, diff_lambda, diff_norm, kv_all=(k_all, v_all))
        obf, obb, sd = _delta_scan(_delta_prep(zb, zs, lp, l, conv_w, prow, pcol, dconsts), bp, lp, l, head_mask, None)
        ocf, ocb, sh = _hgrn_scan(_hgrn_prep(zc, l, lb, hconsts), bp, lp, l, None)
        xp = _tail(xp, oa, obf, obb, zb, ocf, ocb, zc, mod_p, l, *tails)
        new_sd.append(sd)
        new_sh.append(jnp.swapaxes(sh, -1, -2))

        za, zb, zs, zc = _in_proj(xs, mod_s, w_in_l[l], ls, rope_tabs=rope)
        oa = _attention(za, ls, l, diff_lambda, diff_norm, ctx_kv=(ck, cv))
        obf, obb, _ = _delta_scan(_delta_prep(zb, zs, ls, l, conv_w, prow, pcol, dconsts), bs, ls, l, head_mask,
                                  state_delta)
        ocf, ocb, _ = _hgrn_scan(_hgrn_prep(zc, l, lb, hconsts), bs, ls, l, sh0)
        xs = _tail(xs, oa, obf, obb, zb, ocf, ocb, zc, mod_s, l, *tails)

    return (xp.reshape(bp, lp, d), xs.reshape(bs, ls, d),
            k_all.reshape(bp, depth, lp, A_HEADS, 2 * A_QK), v_all.reshape(bp, depth, lp, A_HEADS, A_V),
            jnp.stack(new_sd, 1), jnp.stack(new_sh, 1))
```

```python
import functools
import math

import numpy as np
import jax
import jax.numpy as jnp
from jax import lax
from jax.experimental import pallas as pl
from jax.experimental.pallas import tpu as pltpu

F32 = jnp.float32
BF16 = jnp.bfloat16

A_HEADS = 4
A_QK = 64
A_V = 2 * A_QK
B_HEADS = 4
B_DK = 64
B_DV = 64
C_HEADS = 4
C_DK = 64
C_DV = 64
GRID_W = 64
ROPE_BASE = 10000.0
LN_EPS = 1e-5
RMS_EPS = 1e-6
L2_EPS = 1e-6

LANES = 128
TOK = 128
DELTA_CHUNK = 64
VMEM_LIMIT = 60 * 1024 * 1024


def _cparams(*sem):
    return pltpu.CompilerParams(dimension_semantics=sem, vmem_limit_bytes=VMEM_LIMIT)


def _dot(a, b):
    return jnp.dot(a, b, preferred_element_type=F32)


def _dot_nt(a, b):
    return lax.dot_general(a, b, (((1,), (1,)), ((), ())), preferred_element_type=F32)


def _bdot(a, b):
    return _dot(a.astype(BF16), b.astype(BF16))


def _split3(x):
    hi = x.astype(BF16)
    r = x - hi.astype(F32)
    mid = r.astype(BF16)
    lo = (r - mid.astype(F32)).astype(BF16)
    return hi, mid, lo


def _dot01(w01, x, terms=3):
    hi, mid, lo = _split3(x)
    if terms == 2:
        return _dot(w01, mid) + _dot(w01, hi)
    return (_dot(w01, lo) + _dot(w01, mid)) + _dot(w01, hi)


def _dot01_nt(x, w01):
    hi, mid, lo = _split3(x)
    return (_dot_nt(lo, w01) + _dot_nt(mid, w01)) + _dot_nt(hi, w01)


def _sigmoid(x):
    return 1.0 / (1.0 + jnp.exp(-x))


def _silu(x):
    return x * _sigmoid(x)


def _softplus(x):
    return jnp.maximum(x, 0.0) + jnp.log(1.0 + jnp.exp(-jnp.abs(x)))


def _lane_iota(shape):
    return lax.broadcasted_iota(jnp.int32, shape, len(shape) - 1)


def _row_iota(shape):
    return lax.broadcasted_iota(jnp.int32, shape, len(shape) - 2)


def _layer_norm(x, g, b):
    mu = jnp.mean(x, axis=-1, keepdims=True)
    xc = x - mu
    var = jnp.mean(xc * xc, axis=-1, keepdims=True)
    return xc * lax.rsqrt(var + LN_EPS) * g + b


def _mod_kernel(c_ref, w_ref, b_ref, o_ref):
    cs = _silu(c_ref[...])
    o_ref[0] = _bdot(cs, w_ref[0]) + b_ref[0]


def _modulation(cond, w_mod, b_mod):
    depth, d, n = w_mod.shape
    rows = cond.shape[0]
    tn = 1536 if n % 1536 == 0 else n
    return pl.pallas_call(
        _mod_kernel,
        grid=(depth, n // tn),
        in_specs=[
            pl.BlockSpec((rows, d), lambda l, j: (0, 0)),
            pl.BlockSpec((1, d, tn), lambda l, j: (l, 0, j)),
            pl.BlockSpec((1, 1, tn), lambda l, j: (l, 0, j)),
        ],
        out_specs=pl.BlockSpec((1, rows, tn), lambda l, j: (l, 0, j)),
        out_shape=jax.ShapeDtypeStruct((depth, rows, n), F32),
        compiler_params=_cparams("parallel", "parallel"),
        name="modulation",
    )(cond, w_mod, b_mod.reshape(depth, 1, n))


N_A = 3 * A_HEADS * A_V
N_B = 4 * B_HEADS * B_DK
N_S = 2 * LANES
N_C = 5 * C_HEADS * C_DK
N_Z = N_A + N_B + N_S + N_C


N_QK = A_HEADS * 2 * A_QK
Q_SCALE = (A_QK ** -0.5) * math.log2(math.e)


def _in_proj_kernel(x_ref, mod_ref, w_ref, *rest, d_model, rope):
    if rope:
        rc_ref, rs1_ref, rs2_ref, za_ref, zb_ref, zs_ref, zc_ref, zsr_ref = rest
    else:
        _, _, zq_ref, ko_ref, vo_ref, zb_ref, zs_ref, zc_ref, zsr_ref = rest
    m = mod_ref[0]
    shift = m[:, 0:d_model]
    scale = m[:, d_model:2 * d_model]
    h = (x_ref[...] * (1.0 + scale) + shift).astype(BF16)
    if rope:
        wide = 2 * LANES
        rc, rs1, rs2 = (jnp.concatenate([t[...], t[...]], axis=1) for t in (rc_ref, rs1_ref, rs2_ref))
        for g in range(N_A // wide):
            blk = _dot(h, w_ref[:, g * wide:(g + 1) * wide])
            if g < 2 * N_QK // wide:
                up = pltpu.roll(blk, wide - 16, axis=1)
                dn = pltpu.roll(blk, 16, axis=1)
                blk = blk * rc + up * rs1 + dn * rs2
            if g < N_QK // wide:
                blk = blk * Q_SCALE
            za_ref[:, g * wide:(g + 1) * wide] = blk.astype(za_ref.dtype)
    else:
        zq_ref[...] = _dot(h, w_ref[:, 0:N_QK])
        ko_ref[0, 0] = _dot(h, w_ref[:, N_QK:2 * N_QK])
        vo_ref[0, 0] = _dot(h, w_ref[:, 2 * N_QK:N_A])
    zb_ref[...] = _dot(h, w_ref[:, N_A:N_A + N_B])
    zs = _dot(h, w_ref[:, N_A + N_B:N_A + N_B + N_S])
    zs_ref[...] = zs
    for g in range(zs.shape[0] // TOK):
        for p in range(N_S // LANES):
            zsr_ref[p, g] = zs[g * TOK:(g + 1) * TOK, p * LANES:(p + 1) * LANES].T[0:8, :]
    zc_ref[...] = _dot(h, w_ref[:, N_A + N_B + N_S:N_Z])


def _in_proj(x, mod, w, seq_len, rope_tabs=None, kv_out=None, layer=0):
    n, d = x.shape
    rope = rope_tabs is not None
    tm = min(512, seq_len) if rope else seq_len
    tiles_per_seq = seq_len // tm
    rows_per_mod = n // mod.shape[0]
    in_specs = [
        pl.BlockSpec((tm, d), lambda i: (i, 0)),
        pl.BlockSpec((1, 1, mod.shape[2]), lambda i: ((i * tm) // rows_per_mod, 0, 0)),
        pl.BlockSpec((None, d, N_Z), lambda i: (layer, 0, 0)),
    ]
    args = [x, mod, w]
    row = lambda w_: pl.BlockSpec((tm, w_), lambda i: (i, 0))
    tail_w = [N_B, N_S, N_C]
    zsr_spec = pl.BlockSpec((N_S // LANES, tm // TOK, 8, TOK), lambda i: (0, i, 0, 0))
    zsr_shape = jax.ShapeDtypeStruct((N_S // LANES, n // TOK, 8, TOK), F32)
    if rope:
        in_specs += [pl.BlockSpec((tm, LANES), lambda i: (i % tiles_per_seq, 0))] * 3
        args += list(rope_tabs)
        out_specs = [row(N_A)] + [row(w_) for w_ in tail_w] + [zsr_spec]
        out_shape = ([jax.ShapeDtypeStruct((n, N_A), BF16)] + [jax.ShapeDtypeStruct((n, w_), F32) for w_ in tail_w]
                     + [zsr_shape])
        aliases = {}
    else:
        k_all, v_all = kv_out
        in_specs += [pl.BlockSpec(memory_space=pl.ANY)] * 2
        args += [k_all, v_all]
        kv_spec = lambda a: pl.BlockSpec((1, 1, tm, a.shape[3]), lambda i: (i, layer, 0, 0))
        out_specs = [row(N_QK), kv_spec(k_all), kv_spec(v_all)] + [row(w_) for w_ in tail_w] + [zsr_spec]
        out_shape = ([jax.ShapeDtypeStruct((n, N_QK), F32), jax.ShapeDtypeStruct(k_all.shape, F32),
                      jax.ShapeDtypeStruct(v_all.shape, F32)]
                     + [jax.ShapeDtypeStruct((n, w_), F32) for w_ in tail_w] + [zsr_shape])
        aliases = {3: 1, 4: 2}
    return pl.pallas_call(
        functools.partial(_in_proj_kernel, d_model=d, rope=rope),
        grid=(n // tm,),
        in_specs=in_specs,
        out_specs=out_specs,
        out_shape=out_shape,
        input_output_aliases=aliases,
        compiler_params=_cparams("parallel"),
        name="in_proj_rope" if rope else "in_proj",
    )(*args)


def _rope_tables(seq_len):
    half = A_QK // 2
    nf = half // 2
    pos = jnp.arange(seq_len)
    row = (pos // GRID_W).astype(F32)
    col = (pos % GRID_W).astype(F32)
    inv_freq = ROPE_BASE ** (-jnp.arange(nf, dtype=F32) / nf)
    ang_r = row[:, None] * inv_freq
    ang_c = col[:, None] * inv_freq
    cos64 = jnp.concatenate([jnp.cos(ang_r), jnp.cos(ang_r), jnp.cos(ang_c), jnp.cos(ang_c)], -1)
    zero = jnp.zeros_like(ang_r)
    s1_64 = jnp.concatenate([-jnp.sin(ang_r), zero, -jnp.sin(ang_c), zero], -1)
    s2_64 = jnp.concatenate([zero, jnp.sin(ang_r), zero, jnp.sin(ang_c)], -1)
    rep = LANES // A_QK
    return tuple(jnp.tile(t, (1, rep)) for t in (cos64, s1_64, s2_64))


def _attn_kernel(q_ref, k_ref, v_ref, *rest, lam_init, latent, sb, tq, seq_len):
    if latent:
        ck_ref, cv_ref, dl_ref, nrm_ref, o_ref, s_scr, vx_scr = rest
    else:
        dl_ref, nrm_ref, o_ref, s_scr, vx_scr = rest

    def build_values():
        ones = jnp.ones((seq_len, LANES), BF16)
        for sq in range(sb):
            for h in range(A_HEADS):
                hs = slice(h * LANES, (h + 1) * LANES)
                r0 = sq * vx_scr.shape[0] // sb
                v = v_ref[:, hs] if latent else v_ref[sq, 0, :, hs].astype(BF16)
                vx_scr[r0:r0 + seq_len, 2 * h * LANES:(2 * h + 1) * LANES] = v
                vx_scr[r0:r0 + seq_len, (2 * h + 1) * LANES:(2 * h + 2) * LANES] = ones
                if latent:
                    past = cv_ref.shape[2]
                    vx_scr[seq_len:, 2 * h * LANES:(2 * h + 1) * LANES] = cv_ref[0, 0, :, hs].astype(BF16)
                    vx_scr[seq_len:, (2 * h + 1) * LANES:(2 * h + 2) * LANES] = ones[0:past]

    if latent:
        pl.when(pl.program_id(1) == 0)(build_values)
    else:
        build_values()
    dl = dl_ref[0]
    lam = (jnp.exp(jnp.sum(dl[0:1] * dl[1:2], axis=1, keepdims=True))
           - jnp.exp(jnp.sum(dl[2:3] * dl[3:4], axis=1, keepdims=True)) + lam_init)
    lo = _lane_iota((tq, LANES)) < A_QK
    units = [(sq, h) for sq in range(sb) for h in range(A_HEADS)]

    def scores(unit, slot):
        sq, h = unit
        hs = slice(h * LANES, (h + 1) * LANES)
        q = q_ref[sq * tq:(sq + 1) * tq, hs]
        if latent:
            k = k_ref[:, hs]
            zero = jnp.zeros_like(q)
        else:
            q = q * Q_SCALE
            k = k_ref[sq, 0, :, hs].astype(BF16)
            zero = 0.0
        qq = jnp.concatenate([jnp.where(lo, q, zero), jnp.where(lo, zero, q)], axis=0).astype(BF16)
        s_scr[slot, :, 0:seq_len] = _dot_nt(qq, k)
        if latent:
            s_scr[slot, :, seq_len:] = _dot_nt(qq, ck_ref[0, 0, :, hs].astype(BF16))

    def finish(unit, slot):
        sq, h = unit
        hs = slice(h * LANES, (h + 1) * LANES)
        s = s_scr[slot]
        e = jnp.exp2((s - jnp.max(s, axis=-1, keepdims=True)).astype(BF16))
        keys = vx_scr.shape[0] // sb
        o2 = _dot(e, vx_scr[sq * keys:(sq + 1) * keys, 2 * h * LANES:(2 * h + 2) * LANES])
        r = 1.0 / o2[:, LANES:LANES + 1]
        o2 = o2[:, 0:LANES]
        o = o2[:tq] * r[:tq] - o2[tq:] * (lam * r[tq:])
        o = o * lax.rsqrt(jnp.mean(o * o, axis=-1, keepdims=True) + RMS_EPS) * nrm_ref[0]
        o_ref[sq * tq:(sq + 1) * tq, hs] = o * (1.0 - lam_init)

    scores(units[0], 0)
    for i, unit in enumerate(units):
        if i + 1 < len(units):
            scores(units[i + 1], (i + 1) % 2)
        finish(unit, i % 2)


def _attention(q_arr, seq_len, layer, diff_lambda, diff_norm, kv_all=None, ctx_kv=None):
    n = q_arr.shape[0]
    n_seq = n // seq_len
    latent = ctx_kv is not None
    lam_init = 0.8 - 0.6 * math.exp(-0.3 * layer)
    if latent:
        sb, tq = 1, min(128, seq_len)
        nq = seq_len // tq
        col = lambda c: pl.BlockSpec((seq_len, N_QK), lambda s, i: (s, c))
        ck, cv = ctx_kv
        cspec = lambda a: pl.BlockSpec((1, 1) + a.shape[2:], lambda s, i: (s, layer, 0, 0))
        in_specs = [pl.BlockSpec((tq, N_QK), lambda s, i: (s * nq + i, 0)), col(1), col(2), cspec(ck), cspec(cv)]
        args = [q_arr, q_arr, q_arr, ck, cv]
    else:
        sb, tq, nq = min(4, n_seq), seq_len, 1
        k_all, v_all = kv_all
        kspec = lambda a: pl.BlockSpec((sb, 1) + a.shape[2:], lambda s, i: (s, layer, 0, 0))
        in_specs = [pl.BlockSpec((sb * tq, N_QK), lambda s, i: (s, 0)), kspec(k_all), kspec(v_all)]
        args = [q_arr, k_all, v_all]
    in_specs += [
        pl.BlockSpec((1, 4, A_QK), lambda s, i: (layer, 0, 0)),
        pl.BlockSpec((1, 1, A_V), lambda s, i: (layer, 0, 0)),
    ]
    args += [diff_lambda, diff_norm.reshape(diff_norm.shape[0], 1, A_V)]
    keys = seq_len + (ctx_kv[0].shape[2] if latent else 0)
    return pl.pallas_call(
        functools.partial(_attn_kernel, lam_init=lam_init, latent=latent, sb=sb, tq=tq, seq_len=seq_len),
        grid=(n_seq // sb, nq),
        in_specs=in_specs,
        out_specs=pl.BlockSpec((sb * tq, A_HEADS * A_V), lambda s, i: (s * nq + i, 0)),
        out_shape=jax.ShapeDtypeStruct((n, A_HEADS * A_V), F32),
        scratch_shapes=[pltpu.VMEM((2, 2 * tq, keys), F32), pltpu.VMEM((sb * keys, 2 * A_HEADS * A_V), BF16)],
        compiler_params=_cparams("parallel", "arbitrary"),
        name="diff_attn_latent" if latent else "diff_attn",
    )(*args)


def _head_sum(x):
    lane = _lane_iota(x.shape)
    lo = lane < 64
    s0 = jnp.sum(jnp.where(lo, x, 0.0), axis=-1, keepdims=True)
    s1 = jnp.sum(jnp.where(lo, 0.0, x), axis=-1, keepdims=True)
    return jnp.where(lo, s0, s1)


def _pair_blockdiag(a, b):
    z = jnp.zeros_like(a)
    return jnp.concatenate([jnp.concatenate([a, z], axis=1), jnp.concatenate([z, b], axis=1)], axis=0)


def _gated_head_norm(o, norm_row, gate):
    parts = []
    for hp in range(o.shape[1] // LANES):
        blk = o[:, hp * LANES:(hp + 1) * LANES]
        parts.append(blk * lax.rsqrt(_head_sum(blk * blk) * (1.0 / 64.0) + RMS_EPS))
    return jnp.concatenate(parts, axis=1) * norm_row * _silu(gate)


DELTA_MERGES = (4, 8, 16, 32, 64)
M_CAUSAL, M_STRICT, M_PAIR, M_MERGE0 = 0, 1, 2, 3
M_EYE = M_MERGE0 + len(DELTA_MERGES)
M_HEADS = M_EYE + 1


def _delta_consts():
    t = np.arange(TOK)[:, None]
    s = np.arange(TOK)[None, :]

    def same(n):
        return (t // n) == (s // n)

    masks, cums = [], []
    for d in range(2):
        before = (s <= t) if d == 0 else (s >= t)
        strict = (s < t) if d == 0 else (s > t)
        causal = same(DELTA_CHUNK) & before
        st = same(DELTA_CHUNK) & strict
        rows = [causal, st, st & same(2)]
        rows += [st & same(n) & ~same(n // 2) for n in DELTA_MERGES]
        rows += [t == s, same(64)]
        masks.append(np.stack(rows))
        cums.append(np.concatenate([causal, same(DELTA_CHUNK) & ~before, same(DELTA_CHUNK)], 0))
    return (jnp.asarray(np.stack(masks), F32), jnp.asarray(np.stack(cums), BF16))


def _delta_prep_kernel(z_ref, zp_ref, zn_ref, sc_ref, sr_ref, cw_ref, prow_ref, pcol_ref, msk_ref, cum_ref,
                       u_ref, w_ref, qd_ref, qkm_ref, kdt_ref, gl_ref, *, blocks_per_seq, g_tiles):
    jloc = pl.program_id(0) % blocks_per_seq
    rb = g_tiles * TOK
    nqk = B_HEADS * B_DK
    pairs = B_HEADS // 2
    x = z_ref[...]
    cw = cw_ref[0]
    prev = jnp.where(jloc > 0, zp_ref[7:8, :], 0.0)
    nxt = jnp.where(jloc < blocks_per_seq - 1, zn_ref[0:1, :], 0.0)
    row = _row_iota(x.shape)
    dn = jnp.where(row == 0, prev, pltpu.roll(x, 1, axis=0))
    up = jnp.where(row == rb - 1, nxt, pltpu.roll(x, rb - 1, axis=0))
    y = _silu(cw[0:1] * dn + cw[1:2] * x + cw[2:3] * up)

    lo = _lane_iota((TOK, LANES)) < 64
    hi = jnp.logical_not(lo)
    units = [(g, p) for g in range(g_tiles) for p in range(pairs)]
    chains = [(g, p, d, hh) for (g, p) in units for d in range(2) for hh in range(2)]

    q, k, v, kk, qk, gcol, grow = {}, {}, {}, {}, {}, {}, {}
    for (g, p) in units:
        rs = slice(g * TOK, (g + 1) * TOK)
        qs = y[rs, p * LANES:(p + 1) * LANES]
        ks = y[rs, nqk + p * LANES:nqk + (p + 1) * LANES]
        v[g, p] = y[rs, 2 * nqk + p * LANES:2 * nqk + (p + 1) * LANES]
        q[g, p] = qs * lax.rsqrt(_head_sum(qs * qs) + L2_EPS) * (B_DK ** -0.5)
        k[g, p] = ks * lax.rsqrt(_head_sum(ks * ks) + L2_EPS)
    for (g, p) in units:
        k16 = k[g, p].astype(BF16)
        for hh, sel in enumerate((lo, hi)):
            kk[g, p, hh] = _dot_nt(jnp.where(sel, k[g, p], 0.0).astype(BF16), k16)
            qk[g, p, hh] = _dot_nt(jnp.where(sel, q[g, p], 0.0).astype(BF16), k16)
    for (g, p) in units:
        prow = prow_ref[0, p]
        pcol = pcol_ref[0, p]
        xs = sc_ref[g * TOK:(g + 1) * TOK, p * LANES:(p + 1) * LANES]
        lane = _lane_iota(xs.shape)
        gcol[g, p] = jnp.where(lane < 4, _sigmoid(xs), -jnp.exp(prow[0:1]) * _softplus(xs + prow[1:2]))
        xr = sr_ref[p, g]
        rowi = _row_iota(xr.shape)
        grow[g, p] = jnp.where(rowi < 4, _sigmoid(xr), -jnp.exp(pcol[:, 0:1]) * _softplus(xr + pcol[:, 1:2]))
    cs, br = {}, {}
    for g in range(g_tiles):
        gc = jnp.concatenate([gcol[g, p] for p in range(pairs)], axis=1)
        gr = jnp.concatenate([grow[g, p] for p in range(pairs)], axis=0)
        for d in range(2):
            csd = _dot01(cum_ref[d], gc)
            brd = _dot01_nt(gr, cum_ref[d, 0:TOK])
            for p in range(pairs):
                cs[g, p, d] = csd[:, p * LANES:(p + 1) * LANES]
                br[g, p, d] = brd[p * 8:(p + 1) * 8]
    beta, bcol, dec, m, x = {}, {}, {}, {}, {}
    for ch in chains:
        g, p, d, hh = ch
        cb, cg = d * 2 + hh, 4 + d * 2 + hh
        beta[ch] = gcol[g, p][:, cb:cb + 1]
        bcol[ch] = cs[g, p, d][0:TOK, cg:cg + 1]
        dec[ch] = jnp.exp(jnp.minimum(bcol[ch] - br[g, p, d][cg:cg + 1, :], 0.0)) * msk_ref[d, M_CAUSAL]
        m[ch] = kk[g, p, hh] * beta[ch] * dec[ch] * msk_ref[d, M_STRICT]
        x[ch] = msk_ref[d, M_EYE] - m[ch] * msk_ref[d, M_PAIR]
    for lvl in range(len(DELTA_MERGES)):
        yv = {ch: _bdot(m[ch] * msk_ref[ch[2], M_MERGE0 + lvl], x[ch]) for ch in chains}
        x = {ch: x[ch] - _bdot(x[ch], yv[ch]) for ch in chains}
    for (g, p) in units:
        rs = slice(g * TOK, (g + 1) * TOK)
        for d in range(2):
            sol = None
            eq, ek, gl, qkm = [], [], [], []
            for hh, sel in enumerate((lo, hi)):
                ch = (g, p, d, hh)
                cg = 4 + d * 2 + hh
                eb = jnp.exp(bcol[ch])
                rhs = jnp.concatenate([jnp.where(sel, v[g, p] * beta[ch], 0.0),
                                       jnp.where(sel, k[g, p] * (beta[ch] * eb), 0.0)], axis=1)
                part = _bdot(x[ch], rhs)
                sol = part if sol is None else sol + part
                qkm.append((qk[g, p, hh] * dec[ch]).astype(BF16))
                eq.append(eb)
                ek.append(jnp.exp(cs[g, p, d][TOK:2 * TOK, cg:cg + 1]))
                gl.append(jnp.exp(cs[g, p, d][2 * TOK:3 * TOK, cg:cg + 1]))
            u_ref[d, p, rs, :] = sol[:, 0:LANES]
            w_ref[d, p, rs, :] = sol[:, LANES:2 * LANES].astype(BF16)
            qd_ref[d, p, rs, :] = (q[g, p] * jnp.where(lo, eq[0], eq[1])).astype(BF16)
            qkm_ref[d, p, rs, :] = jnp.concatenate(qkm, axis=1)
            kd = k[g, p] * jnp.where(lo, ek[0], ek[1])
            kdt_ref[d, p, rs, :] = kd.T.astype(BF16)
            glf = jnp.where(lo, gl[0], gl[1])
            gl_ref[d, p, g * 16:(g + 1) * 16, :] = jnp.concatenate([glf[0:8], glf[64:72]], axis=0)


def _delta_prep(zb, zs, zs_rows, seq_len, layer, conv_w, prow, pcol, consts):
    n = zb.shape[0]
    pairs = B_HEADS // 2
    g_tiles = 2
    rb = g_tiles * TOK
    blocks_per_seq = seq_len // rb
    n_blocks = n // rb
    nconv = conv_w.shape[2]
    msk, cum = consts
    last8 = n // 8 - 1
    out_w = [LANES, LANES, LANES, 2 * LANES, LANES]
    out_t = [F32, BF16, BF16, BF16, BF16]
    return pl.pallas_call(
        functools.partial(_delta_prep_kernel, blocks_per_seq=blocks_per_seq, g_tiles=g_tiles),
        grid=(n_blocks,),
        in_specs=[
            pl.BlockSpec((rb, nconv), lambda i: (i, 0)),
            pl.BlockSpec((8, nconv), lambda i: (jnp.maximum(i * (rb // 8) - 1, 0), 0)),
            pl.BlockSpec((8, nconv), lambda i: (jnp.minimum((i + 1) * (rb // 8), last8), 0)),
            pl.BlockSpec((rb, pairs * LANES), lambda i: (i, 0)),
            pl.BlockSpec((pairs, g_tiles, 8, TOK), lambda i: (0, i, 0, 0)),
            pl.BlockSpec((1, 3, nconv), lambda i: (layer, 0, 0)),
            pl.BlockSpec((1, pairs, 8, LANES), lambda i: (layer, 0, 0, 0)),
            pl.BlockSpec((1, pairs, 8, LANES), lambda i: (layer, 0, 0, 0)),
            pl.BlockSpec(msk.shape, lambda i: (0, 0, 0, 0)),
            pl.BlockSpec(cum.shape, lambda i: (0, 0, 0)),
        ],
        out_specs=[pl.BlockSpec((2, pairs, rb, w_), lambda i: (0, 0, i, 0)) for w_ in out_w]
        + [pl.BlockSpec((2, pairs, g_tiles * 16, LANES), lambda i: (0, 0, i, 0))],
        out_shape=[jax.ShapeDtypeStruct((2, pairs, n, w_), t_) for w_, t_ in zip(out_w, out_t)]
        + [jax.ShapeDtypeStruct((2, pairs, n // TOK * 16, LANES), F32)],
        compiler_params=_cparams("parallel"),
        name="delta_prep",
    )(zb, zb, zb, zs, zs_rows, conv_w, prow, pcol, msk, cum)


def _delta_scan_kernel(*refs, sb, tb, has_s0):
    fwd, bwd = refs[0:6], refs[6:12]
    hm_ref = refs[12]
    if has_s0:
        s0_ref, of_ref, ob_ref, s_s = refs[13:]
    else:
        _, of_ref, ob_ref, sout_ref, s_s = refs[13:]
    t = pl.program_id(1)
    n_t = pl.num_programs(1)
    n_g = tb // TOK
    pairs = B_HEADS // 2
    chains = [(d, sq, p) for d in range(2) for sq in range(sb) for p in range(pairs)]

    @pl.when(t == 0)
    def _():
        for (d, sq, p) in chains:
            if has_s0:
                s_s[d, sq, p] = _pair_blockdiag(s0_ref[sq, 0, d, 2 * p], s0_ref[sq, 0, d, 2 * p + 1])
            else:
                s_s[d, sq, p] = jnp.zeros((LANES, LANES), F32)

    lane64 = _lane_iota((DELTA_CHUNK, LANES)) < 64
    zpad = jnp.zeros((DELTA_CHUNK, LANES), BF16)

    def body(gi, carry):
        for step in range(2):
            tmp = {}
            for ch in chains:
                d, sq, p = ch
                u_r, w_r = (fwd, bwd)[d][0:2]
                g = gi if d == 0 else n_g - 1 - gi
                c = step if d == 0 else 1 - step
                rc = pl.multiple_of(g * TOK + c * DELTA_CHUNK, DELTA_CHUNK)
                s = s_s[d, sq, p]
                s16 = s.astype(BF16)
                vn = u_r[0, p, sq, pl.ds(rc, DELTA_CHUNK), :] - _dot(w_r[0, p, sq, pl.ds(rc, DELTA_CHUNK), :], s16)
                tmp[ch] = (s, s16, vn, g, c, rc)
            for ch in chains:
                d, sq, p = ch
                _, _, qd_r, qkm_r, kdt_r, gl_r = (fwd, bwd)[d]
                o_r = (of_ref, ob_ref)[d]
                s, s16, vn, g, c, rc = tmp[ch]
                r0 = pl.multiple_of(g * TOK, TOK)
                v0 = jnp.where(lane64, vn, 0.0).astype(BF16)
                v1 = jnp.where(lane64, 0.0, vn).astype(BF16)
                vnb = vn.astype(BF16)
                if c == 0:
                    vext = jnp.concatenate([vnb, zpad], axis=0)
                    v2 = jnp.concatenate([v0, zpad, v1, zpad], axis=0)
                else:
                    vext = jnp.concatenate([zpad, vnb], axis=0)
                    v2 = jnp.concatenate([zpad, v0, zpad, v1], axis=0)
                o = (_dot(qd_r[0, p, sq, pl.ds(rc, DELTA_CHUNK), :], s16)
                     + _dot(qkm_r[0, p, sq, pl.ds(rc, DELTA_CHUNK), :], v2))
                o_r[sq, pl.ds(rc, DELTA_CHUNK), p * LANES:(p + 1) * LANES] = o
                glr = gl_r[0, p, sq, pl.ds(pl.multiple_of(g * 16 + c * 8, 8), 8), :][0:1]
                s_s[d, sq, p] = s * glr + _dot(kdt_r[0, p, sq, pl.ds(r0, TOK), :], vext) * hm_ref[...]
        return carry

    lax.fori_loop(0, n_g, body, 0)

    if not has_s0:
        @pl.when(t == n_t - 1)
        def _():
            for (d, sq, p) in chains:
                s = s_s[d, sq, p]
                sout_ref[sq, 0, d, 2 * p] = s[0:64, 0:64]
                sout_ref[sq, 0, d, 2 * p + 1] = s[64:128, 64:128]


def _delta_scan(prep, n_seq, seq_len, layer, head_mask, s0=None, s_all=None):
    pairs = B_HEADS // 2
    has_s0 = s0 is not None
    sb = n_seq if n_seq <= 2 else 8
    tb = min(seq_len, 512)
    n_t = seq_len // tb
    arrs = [a.reshape(2, pairs, n_seq, a.shape[2] // n_seq, a.shape[3]) for a in prep]

    def specs(d):
        tmap = (lambda s, t: (d, 0, s, t, 0)) if d == 0 else (lambda s, t: (d, 0, s, n_t - 1 - t, 0))
        return [pl.BlockSpec((1, pairs, sb, a.shape[3] // n_t, a.shape[4]), tmap) for a in arrs]

    in_specs = specs(0) + specs(1) + [pl.BlockSpec((TOK, LANES), lambda s, t: (0, 0))]
    args = arrs + arrs + [head_mask]
    st_spec = pl.BlockSpec((sb, 1, 2, B_HEADS, B_DK, B_DV), lambda s, t: (s, layer, 0, 0, 0, 0))
    width = B_HEADS * B_DV
    out_specs = [
        pl.BlockSpec((sb, tb, width), lambda s, t: (s, t, 0)),
        pl.BlockSpec((sb, tb, width), lambda s, t: (s, n_t - 1 - t, 0)),
    ]
    out_shape = [jax.ShapeDtypeStruct((n_seq, seq_len, width), F32)] * 2
    if has_s0:
        in_specs.append(st_spec)
        args.append(s0)
        aliases = {}
    else:
        in_specs.append(pl.BlockSpec(memory_space=pl.ANY))
        args.append(s_all)
        out_specs.append(st_spec)
        out_shape.append(jax.ShapeDtypeStruct(s_all.shape, F32))
        aliases = {len(args) - 1: 2}
    outs = pl.pallas_call(
        functools.partial(_delta_scan_kernel, sb=sb, tb=tb, has_s0=has_s0),
        grid=(n_seq // sb, n_t),
        in_specs=in_specs,
        out_specs=out_specs,
        out_shape=out_shape,
        input_output_aliases=aliases,
        scratch_shapes=[pltpu.VMEM((2, sb, pairs, LANES, LANES), F32)],
        compiler_params=_cparams("parallel", "arbitrary"),
        name="delta_scan_s0" if has_s0 else "delta_scan",
    )(*args)
    n = n_seq * seq_len
    return (outs[0].reshape(n, width), outs[1].reshape(n, width)) + tuple(outs[2:])


HGRN_LEVELS = 7
HGRN_WROWS = (HGRN_LEVELS + 2) * TOK + 8


def _hgrn_consts():
    t = np.arange(TOK)
    masks = [t[:, None] == t[None, :]]
    for lvl in range(1, HGRN_LEVELS + 1):
        n = 1 << lvl
        masks.append((t[:, None] // n) == (t[None, :] // n))
    ws = []
    for d in range(2):
        tau = t if d == 0 else TOK - 1 - t
        tt, ti = tau[:, None], tau[None, :]
        blocks = []
        for lvl in range(1, HGRN_LEVELS + 1):
            n = 1 << lvl
            piv = (tau - tau % n + n // 2 - 1)[:, None]
            upper = ((tau % n) >= n // 2)[:, None]
            blocks.append(np.where(upper, (ti > piv) & (ti <= tt), (ti > tt) & (ti <= piv)))
        blocks.append(ti <= tt)
        blocks.append(ti > tt)
        blocks.append(np.ones((8, TOK), bool))
        ws.append(np.concatenate(blocks, 0))
    return jnp.asarray(np.stack(masks), F32), jnp.asarray(np.stack(ws), BF16)


def _hgrn_prep_kernel(z_ref, lb_ref, msk_ref, w_ref, intra_ref, qd_ref, kv_ref, gl_ref, *, layer, g_tiles):
    pairs = C_HEADS // 2
    nk = C_HEADS * C_DK
    lo = _lane_iota((TOK, LANES)) < 64
    row = _row_iota((TOK, LANES))
    units = [(g, p) for g in range(g_tiles) for p in range(pairs)]
    chains = [(g, p, d) for (g, p) in units for d in range(2)]

    lbs = {}
    for p in range(pairs):
        for d in range(2):
            x = lb_ref[p, d]
            e = jnp.exp(x - jnp.max(x, axis=0, keepdims=True))
            sm = e / jnp.sum(e, axis=0, keepdims=True)
            if layer > 0:
                lbs[p, d] = jnp.sum(sm[1:layer + 1], axis=0, keepdims=True)
            else:
                lbs[p, d] = jnp.zeros((1, LANES), F32)

    def stack_heads(x):
        return jnp.concatenate([jnp.where(lo, x, 0.0), jnp.where(lo, 0.0, x)], axis=0).astype(BF16)

    def both_heads(mask):
        return jnp.concatenate([mask, mask], axis=1)

    q, vt, key, ex, acc = {}, {}, {}, {}, {}
    for (g, p) in units:
        rs = slice(g * TOK, (g + 1) * TOK)
        q[g, p] = _silu(z_ref[rs, p * LANES:(p + 1) * LANES])
        vt[g, p] = z_ref[rs, 3 * nk + p * LANES:3 * nk + (p + 1) * LANES].T
    for g in range(g_tiles):
        for d in range(2):
            lg = []
            for p in range(pairs):
                f = z_ref[g * TOK:(g + 1) * TOK, (1 + d) * nk + p * LANES:(1 + d) * nk + (p + 1) * LANES]
                forget = lbs[p, d] + (1.0 - lbs[p, d]) * _sigmoid(f)
                key[g, p, d] = 1.0 - forget
                lg.append(jnp.log(forget))
            spans = _dot01(w_ref[d], jnp.concatenate(lg, axis=1), terms=2)
            for p in range(pairs):
                ex[g, p, d] = spans[:, p * LANES:(p + 1) * LANES]
    for ch in chains:
        acc[ch] = _dot_nt(key[ch].astype(BF16), stack_heads(q[ch[0], ch[1]])) * both_heads(msk_ref[0])
    for lvl in range(1, HGRN_LEVELS + 1):
        for ch in chains:
            g, p, d = ch
            tau = row if d == 0 else TOK - 1 - row
            e = jnp.exp(ex[ch][(lvl - 1) * TOK:lvl * TOK])
            up = (tau & (1 << (lvl - 1))) != 0
            qt = jnp.where(up, q[g, p] * e, 0.0)
            kt = jnp.where(up, 0.0, key[ch] * e)
            a = _dot_nt(kt.astype(BF16), stack_heads(qt))
            if lvl < HGRN_LEVELS:
                a = a * both_heads(msk_ref[lvl])
            acc[ch] = acc[ch] + a
    base = HGRN_LEVELS * TOK
    for ch in chains:
        g, p, d = ch
        rs = slice(g * TOK, (g + 1) * TOK)
        vv = vt[g, p]
        intra_t = (_dot(jnp.where(row < 64, vv, 0.0).astype(BF16), acc[ch][:, 0:LANES].astype(BF16))
                   + _dot(jnp.where(row < 64, 0.0, vv).astype(BF16), acc[ch][:, LANES:2 * LANES].astype(BF16)))
        intra_ref[d, p, rs, :] = intra_t.T
        qd_ref[d, p, rs, :] = (q[g, p] * jnp.exp(ex[ch][base:base + TOK])).astype(BF16)
        kd = (key[ch] * jnp.exp(ex[ch][base + TOK:base + 2 * TOK])).astype(BF16)
        kv_ref[d, p, rs, :] = _dot(vv.astype(BF16), kd) * msk_ref[6]
        gl_ref[d, p, g * 8:(g + 1) * 8, :] = jnp.exp(ex[ch][base + 2 * TOK:base + 2 * TOK + 8])


def _hgrn_prep(zc, layer, lb, consts):
    n = zc.shape[0]
    pairs = C_HEADS // 2
    g_tiles = 2
    rb = g_tiles * TOK
    msk, wst = consts
    n_in = 4 * C_HEADS * C_DK
    return pl.pallas_call(
        functools.partial(_hgrn_prep_kernel, layer=layer, g_tiles=g_tiles),
        grid=(n // rb,),
        in_specs=[
            pl.BlockSpec((rb, n_in), lambda i: (i, 0)),
            pl.BlockSpec(lb.shape, lambda i: (0, 0, 0, 0)),
            pl.BlockSpec(msk.shape, lambda i: (0, 0, 0)),
            pl.BlockSpec(wst.shape, lambda i: (0, 0, 0)),
        ],
        out_specs=[pl.BlockSpec((2, pairs, rb, LANES), lambda i: (0, 0, i, 0))] * 3
        + [pl.BlockSpec((2, pairs, g_tiles * 8, LANES), lambda i: (0, 0, i, 0))],
        out_shape=[jax.ShapeDtypeStruct((2, pairs, n, LANES), t_) for t_ in (F32, BF16, F32)]
        + [jax.ShapeDtypeStruct((2, pairs, n // TOK * 8, LANES), F32)],
        compiler_params=_cparams("parallel"),
        name="hgrn_prep",
    )(zc, lb, msk, wst)


def _hgrn_scan_kernel(*refs, sb, tb, has_s0):
    fwd, bwd = refs[0:4], refs[4:8]
    if has_s0:
        s0_ref, of_ref, ob_ref, s_s = refs[8:]
    else:
        _, of_ref, ob_ref, sout_ref, s_s = refs[8:]
    t = pl.program_id(1)
    n_t = pl.num_programs(1)
    n_g = tb // TOK
    pairs = C_HEADS // 2
    chains = [(d, sq, p) for d in range(2) for sq in range(sb) for p in range(pairs)]

    @pl.when(t == 0)
    def _():
        for (d, sq, p) in chains:
            if has_s0:
                s_s[d, sq, p] = _pair_blockdiag(s0_ref[sq, 0, d, 2 * p], s0_ref[sq, 0, d, 2 * p + 1])
            else:
                s_s[d, sq, p] = jnp.zeros((LANES, LANES), F32)

    def body(gi, carry):
        for (d, sq, p) in chains:
            intra_r, qd_r, kv_r, gl_r = (fwd, bwd)[d]
            o_r = (of_ref, ob_ref)[d]
            g = gi if d == 0 else n_g - 1 - gi
            r0 = pl.multiple_of(g * TOK, TOK)
            st = s_s[d, sq, p]
            o = intra_r[0, p, sq, pl.ds(r0, TOK), :] + _dot_nt(qd_r[0, p, sq, pl.ds(r0, TOK), :], st.astype(BF16))
            o_r[sq, pl.ds(r0, TOK), p * LANES:(p + 1) * LANES] = o
            glr = gl_r[0, p, sq, pl.ds(pl.multiple_of(g * 8, 8), 8), :][0:1]
            s_s[d, sq, p] = st * glr + kv_r[0, p, sq, pl.ds(r0, TOK), :]
        return carry

    lax.fori_loop(0, n_g, body, 0)

    if not has_s0:
        @pl.when(t == n_t - 1)
        def _():
            for (d, sq, p) in chains:
                s = s_s[d, sq, p].T
                sout_ref[sq, 0, d, 2 * p] = s[0:64, 0:64]
                sout_ref[sq, 0, d, 2 * p + 1] = s[64:128, 64:128]


def _hgrn_scan(prep, n_seq, seq_len, layer, s0t=None, s_all=None):
    pairs = C_HEADS // 2
    has_s0 = s0t is not None
    sb = n_seq if n_seq <= 2 else 8
    tb = min(seq_len, 512)
    n_t = seq_len // tb
    arrs = [a.reshape(2, pairs, n_seq, a.shape[2] // n_seq, a.shape[3]) for a in prep]

    def specs(d):
        tmap = (lambda s, t: (d, 0, s, t, 0)) if d == 0 else (lambda s, t: (d, 0, s, n_t - 1 - t, 0))
        return [pl.BlockSpec((1, pairs, sb, a.shape[3] // n_t, a.shape[4]), tmap) for a in arrs]

    in_specs = specs(0) + specs(1)
    args = arrs + arrs
    st_spec = pl.BlockSpec((sb, 1, 2, C_HEADS, C_DK, C_DV), lambda s, t: (s, layer, 0, 0, 0, 0))
    width = C_HEADS * C_DV
    out_specs = [
        pl.BlockSpec((sb, tb, width), lambda s, t: (s, t, 0)),
        pl.BlockSpec((sb, tb, width), lambda s, t: (s, n_t - 1 - t, 0)),
    ]
    out_shape = [jax.ShapeDtypeStruct((n_seq, seq_len, width), F32)] * 2
    if has_s0:
        in_specs.append(st_spec)
        args.append(s0t)
        aliases = {}
    else:
        in_specs.append(pl.BlockSpec(memory_space=pl.ANY))
        args.append(s_all)
        out_specs.append(st_spec)
        out_shape.append(jax.ShapeDtypeStruct(s_all.shape, F32))
        aliases = {len(args) - 1: 2}
    outs = pl.pallas_call(
        functools.partial(_hgrn_scan_kernel, sb=sb, tb=tb, has_s0=has_s0),
        grid=(n_seq // sb, n_t),
        in_specs=in_specs,
        out_specs=out_specs,
        out_shape=out_shape,
        input_output_aliases=aliases,
        scratch_shapes=[pltpu.VMEM((2, sb, pairs, LANES, LANES), F32)],
        compiler_params=_cparams("parallel", "arbitrary"),
        name="hgrn_scan_s0" if has_s0 else "hgrn_scan",
    )(*args)
    n = n_seq * seq_len
    return (outs[0].reshape(n, width), outs[1].reshape(n, width)) + tuple(outs[2:])


FFN_SPLIT = 2


def _tail_kernel(x_ref, oa_ref, obf_ref, obb_ref, bg_ref, ocf_ref, ocb_ref, cg_ref, mod_ref, bn_ref, cn_ref,
                 wo_ref, lng_ref, lnb_ref, wi_ref, wd_ref, o_ref, *, d_model, d_ff, alpha):
    m = mod_ref[0]
    na, nb = oa_ref.shape[1], obf_ref.shape[1]
    ob = _gated_head_norm(obf_ref[...] + obb_ref[...], bn_ref[0], bg_ref[...])
    oc = _gated_head_norm(ocf_ref[...] + ocb_ref[...], cn_ref[0], cg_ref[...])
    y = _bdot(oa_ref[...], wo_ref[0:na]) + _bdot(ob, wo_ref[na:na + nb]) + _bdot(oc, wo_ref[na + nb:])
    x1 = _layer_norm(alpha * x_ref[...] + m[:, 2 * d_model:3 * d_model] * y, lng_ref[0, 0:1], lnb_ref[0, 0:1])
    h = (x1 * (1.0 + m[:, 4 * d_model:5 * d_model]) + m[:, 3 * d_model:4 * d_model]).astype(BF16)
    ck = d_ff // FFN_SPLIT
    acc = None
    for c0 in range(0, d_ff, ck):
        gt = _dot(h, wi_ref[:, c0:c0 + ck])
        up = _dot(h, wi_ref[:, d_ff + c0:d_ff + c0 + ck])
        part = _dot((_silu(gt) * up).astype(BF16), wd_ref[c0:c0 + ck, :])
        acc = part if acc is None else acc + part
    o_ref[...] = _layer_norm(alpha * x1 + m[:, 5 * d_model:6 * d_model] * acc, lng_ref[0, 1:2], lnb_ref[0, 1:2])


def _tail(x, oa, obf, obb, zb, ocf, ocb, zc, mod, layer, bnorm, cnorm, w_out, ln_g, ln_b, w_ffn_in, w_ffn_out,
          alpha):
    n, d = x.shape
    tm = 512
    d_ff = w_ffn_out.shape[1]
    rows_per_mod = n // mod.shape[0]
    nb, nc = obf.shape[1], ocf.shape[1]
    row = lambda w_: pl.BlockSpec((tm, w_), lambda i: (i, 0))
    last_cols = lambda a, w_: pl.BlockSpec((tm, w_), lambda i: (i, a.shape[1] // w_ - 1))
    per_layer = lambda w_: pl.BlockSpec((1, 1, w_), lambda i: (layer, 0, 0))
    resident = lambda a: pl.BlockSpec((None,) + a.shape[1:], lambda i: (layer, 0, 0), pipeline_mode=pl.Buffered(1))
    return pl.pallas_call(
        functools.partial(_tail_kernel, d_model=d, d_ff=d_ff, alpha=alpha),
        grid=(n // tm,),
        in_specs=[
            row(d), row(oa.shape[1]),
            row(nb), row(nb), last_cols(zb, nb),
            row(nc), row(nc), last_cols(zc, nc),
            pl.BlockSpec((1, 1, mod.shape[2]), lambda i: ((i * tm) // rows_per_mod, 0, 0)),
            per_layer(nb), per_layer(nc),
            resident(w_out),
            pl.BlockSpec((1, 2, d), lambda i: (layer, 0, 0)),
            pl.BlockSpec((1, 2, d), lambda i: (layer, 0, 0)),
            resident(w_ffn_in), resident(w_ffn_out),
        ],
        out_specs=pl.BlockSpec((tm, d), lambda i: (i, 0)),
        out_shape=jax.ShapeDtypeStruct((n, d), F32),
        compiler_params=_cparams("parallel"),
        name="out_proj_ffn",
    )(x, oa, obf, obb, zb, ocf, ocb, zc, mod, bnorm, cnorm, w_out, ln_g, ln_b, w_ffn_in, w_ffn_out)


def _pair_scalar_order(p):
    return [d * B_HEADS + 2 * p + hh for d in range(2) for hh in range(2)]


def _layout_w_in(w_in):
    depth, d, _ = w_in.shape
    a_end = N_A
    b_end = a_end + N_B
    beta0 = b_end
    dec0 = beta0 + 2 * B_HEADS
    c0 = dec0 + 2 * B_HEADS
    parts = [w_in[:, :, :b_end]]
    for p in range(B_HEADS // 2):
        order = _pair_scalar_order(p)
        cols = [beta0 + i for i in order] + [dec0 + i for i in order]
        parts.append(w_in[:, :, np.asarray(cols)])
        parts.append(jnp.zeros((depth, d, LANES - len(cols)), w_in.dtype))
    parts.append(w_in[:, :, c0:])
    return jnp.concatenate(parts, axis=-1).astype(BF16)


def _delta_params(delta_a_log, delta_dt_bias):
    depth = delta_a_log.shape[0]
    pairs = B_HEADS // 2
    al = delta_a_log.reshape(depth, 2 * B_HEADS)
    dt = delta_dt_bias.reshape(depth, 2 * B_HEADS)
    prow = jnp.zeros((depth, pairs, 8, LANES), F32)
    pcol = jnp.zeros((depth, pairs, 8, LANES), F32)
    for p in range(pairs):
        order = np.asarray(_pair_scalar_order(p))
        prow = prow.at[:, p, 0, 4:8].set(al[:, order]).at[:, p, 1, 4:8].set(dt[:, order])
        pcol = pcol.at[:, p, 4:8, 0].set(al[:, order]).at[:, p, 4:8, 1].set(dt[:, order])
    return prow, pcol


def kernel(x_prompt, x_sample, cache_attn_k, cache_attn_v, state_delta, state_hgrn, c, c_ctx, w_mod, b_mod, w_in, conv_w, delta_a_log, delta_dt_bias, delta_norm, hgrn_lb, hgrn_norm, diff_lambda, diff_norm, w_out, ln_g, ln_b, w_ffn_in, w_ffn_out):
    depth, d, _ = w_in.shape
    bp, lp, _ = x_prompt.shape
    bs, ls, _ = x_sample.shape
    alpha = (2 * depth) ** 0.25

    cond_rows = 8 * ((1 + bs + 7) // 8)
    cond = jnp.concatenate([c_ctx[None], c, jnp.zeros((cond_rows - 1 - bs, d), F32)], axis=0)
    mod = _modulation(cond, w_mod, b_mod)

    w_in_l = _layout_w_in(w_in)
    w_out_b = w_out.astype(BF16)
    w_ffn_in_b = w_ffn_in.astype(BF16)
    w_ffn_out_b = w_ffn_out.astype(BF16)
    prow, pcol = _delta_params(delta_a_log, delta_dt_bias)
    bnorm = jnp.tile(delta_norm, (1, B_HEADS)).reshape(depth, 1, B_HEADS * B_DV)
    cnorm = jnp.tile(hgrn_norm, (1, C_HEADS)).reshape(depth, 1, C_HEADS * C_DV)
    lb = hgrn_lb.reshape(2, depth, C_HEADS // 2, LANES).transpose(2, 0, 1, 3)
    rope = _rope_tables(ls)
    dconsts = _delta_consts()
    head_mask = dconsts[0][0, M_HEADS]
    hconsts = _hgrn_consts()
    past = cache_attn_k.shape[2]
    ck = cache_attn_k.reshape(bs, depth, past, A_HEADS * 2 * A_QK)
    cv = cache_attn_v.reshape(bs, depth, past, A_HEADS * A_V)
    sh0 = jnp.swapaxes(state_hgrn, -1, -2)
    tails = (bnorm, cnorm, w_out_b, ln_g, ln_b, w_ffn_in_b, w_ffn_out_b, alpha)

    xp = x_prompt.reshape(bp * lp, d)
    xs = x_sample.reshape(bs * ls, d)
    k_all = jnp.zeros((bp, depth, lp, A_HEADS * 2 * A_QK), F32)
    v_all = jnp.zeros((bp, depth, lp, A_HEADS * A_V), F32)
    sd_all = jnp.zeros((bp, depth, 2, B_HEADS, B_DK, B_DV), F32)
    sh_all = jnp.zeros((bp, depth, 2, C_HEADS, C_DK, C_DV), F32)
    for l in range(depth):
        mod_p = mod[l, 0:1][:, None, :]
        mod_s = mod[l, 1:1 + bs][:, None, :]

        zq, k_all, v_all, zb, zs, zc, zsr = _in_proj(xp, mod_p, w_in_l, lp, kv_out=(k_all, v_all), layer=l)
        oa = _attention(zq, lp, l, diff_lambda, diff_norm, kv_all=(k_all, v_all))
        prep = _delta_prep(zb, zs, zsr, lp, l, conv_w, prow, pcol, dconsts)
        obf, obb, sd_all = _delta_scan(prep, bp, lp, l, head_mask, s_all=sd_all)
        ocf, ocb, sh_all = _hgrn_scan(_hgrn_prep(zc, l, lb, hconsts), bp, lp, l, s_all=sh_all)
        xp = _tail(xp, oa, obf, obb, zb, ocf, ocb, zc, mod_p, l, *tails)

        za, zb, zs, zc, zsr = _in_proj(xs, mod_s, w_in_l, ls, rope_tabs=rope, layer=l)
        oa = _attention(za, ls, l, diff_lambda, diff_norm, ctx_kv=(ck, cv))
        prep = _delta_prep(zb, zs, zsr, ls, l, conv_w, prow, pcol, dconsts)
        obf, obb = _delta_scan(prep, bs, ls, l, head_mask, s0=state_delta)
        ocf, ocb = _hgrn_scan(_hgrn_prep(zc, l, lb, hconsts), bs, ls, l, s0t=sh0)
        xs = _tail(xs, oa, obf, obb, zb, ocf, ocb, zc, mod_s, l, *tails)

    return (xp.reshape(bp, lp, d), xs.reshape(bs, ls, d),
            k_all.reshape(bp, depth, lp, A_HEADS, 2 * A_QK), v_all.reshape(bp, depth, lp, A_HEADS, A_V),
            sd_all, sh_all)
```

```python
import functools
import math

import numpy as np
import jax
import jax.numpy as jnp
from jax import lax
from jax.experimental import pallas as pl
from jax.experimental.pallas import tpu as pltpu

F32 = jnp.float32
BF16 = jnp.bfloat16

A_HEADS = 4
A_QK = 64
A_V = 2 * A_QK
B_HEADS = 4
B_DK = 64
B_DV = 64
C_HEADS = 4
C_DK = 64
C_DV = 64
GRID_W = 64
ROPE_BASE = 10000.0
LN_EPS = 1e-5
RMS_EPS = 1e-6
L2_EPS = 1e-6

LANES = 128
TOK = 128
DELTA_CHUNK = 64
VMEM_LIMIT = 60 * 1024 * 1024


def _cparams(*sem):
    return pltpu.CompilerParams(dimension_semantics=sem, vmem_limit_bytes=VMEM_LIMIT)


def _dot(a, b):
    return jnp.dot(a, b, preferred_element_type=F32)


def _dot_nt(a, b):
    return lax.dot_general(a, b, (((1,), (1,)), ((), ())), preferred_element_type=F32)


def _bdot(a, b):
    return _dot(a.astype(BF16), b.astype(BF16))


def _split3(x):
    hi = x.astype(BF16)
    r = x - hi.astype(F32)
    mid = r.astype(BF16)
    lo = (r - mid.astype(F32)).astype(BF16)
    return hi, mid, lo


def _dot01(w01_rep, x, terms=3):
    pieces = _split3(x)[:terms]
    return _dot(w01_rep, jnp.concatenate(pieces, axis=0))


def _dot01_nt(x, w01):
    hi, mid, lo = _split3(x)
    return (_dot_nt(lo, w01) + _dot_nt(mid, w01)) + _dot_nt(hi, w01)


def _sigmoid(x):
    return 1.0 / (1.0 + jnp.exp(-x))


def _silu(x):
    return x * _sigmoid(x)


def _softplus(x):
    return jnp.maximum(x, 0.0) + jnp.log(1.0 + jnp.exp(-jnp.abs(x)))


def _lane_iota(shape):
    return lax.broadcasted_iota(jnp.int32, shape, len(shape) - 1)


def _row_iota(shape):
    return lax.broadcasted_iota(jnp.int32, shape, len(shape) - 2)


def _layer_norm(x, g, b):
    mu = jnp.mean(x, axis=-1, keepdims=True)
    xc = x - mu
    var = jnp.mean(xc * xc, axis=-1, keepdims=True)
    return xc * lax.rsqrt(var + LN_EPS) * g + b


def _mod_kernel(c_ref, w_ref, b_ref, o_ref):
    cs = _silu(c_ref[...])
    o_ref[0] = _bdot(cs, w_ref[0]) + b_ref[0]


def _modulation(cond, w_mod, b_mod):
    depth, d, n = w_mod.shape
    rows = cond.shape[0]
    tn = 1536 if n % 1536 == 0 else n
    return pl.pallas_call(
        _mod_kernel,
        grid=(depth, n // tn),
        in_specs=[
            pl.BlockSpec((rows, d), lambda l, j: (0, 0)),
            pl.BlockSpec((1, d, tn), lambda l, j: (l, 0, j)),
            pl.BlockSpec((1, 1, tn), lambda l, j: (l, 0, j)),
        ],
        out_specs=pl.BlockSpec((1, rows, tn), lambda l, j: (l, 0, j)),
        out_shape=jax.ShapeDtypeStruct((depth, rows, n), F32),
        compiler_params=_cparams("parallel", "parallel"),
        name="modulation",
    )(cond, w_mod, b_mod.reshape(depth, 1, n))


N_A = 3 * A_HEADS * A_V
N_B = 4 * B_HEADS * B_DK
N_S = 2 * LANES
N_C = 5 * C_HEADS * C_DK
N_Z = N_A + N_B + N_S + N_C


N_QK = A_HEADS * 2 * A_QK
Q_SCALE = (A_QK ** -0.5) * math.log2(math.e)


def _in_proj_kernel(x_ref, mod_ref, wab_ref, ws_ref, wc_ref, *rest, d_model, rope):
    if rope:
        rc_ref, rs1_ref, rs2_ref, za_ref, zb_ref, zs_ref, zc_ref, zsr_ref, w_ref = rest
    else:
        _, _, zq_ref, ko_ref, vo_ref, zb_ref, zs_ref, zc_ref, zsr_ref, w_ref = rest

    @pl.when(pl.program_id(0) == 0)
    def _():
        w_ref[...] = wab_ref[...].astype(BF16)

    m = mod_ref[0]
    shift = m[:, 0:d_model]
    scale = m[:, d_model:2 * d_model]
    h = (x_ref[...] * (1.0 + scale) + shift).astype(BF16)
    if rope:
        wide = 2 * LANES
        rc, rs1, rs2 = (jnp.concatenate([t[...], t[...]], axis=1) for t in (rc_ref, rs1_ref, rs2_ref))
        for g in range(N_A // wide):
            blk = _dot(h, w_ref[:, g * wide:(g + 1) * wide])
            if g < 2 * N_QK // wide:
                up = pltpu.roll(blk, wide - 16, axis=1)
                dn = pltpu.roll(blk, 16, axis=1)
                blk = blk * rc + up * rs1 + dn * rs2
            if g < N_QK // wide:
                blk = blk * Q_SCALE
            za_ref[:, g * wide:(g + 1) * wide] = blk.astype(za_ref.dtype)
    else:
        zq_ref[...] = _dot(h, w_ref[:, 0:N_QK])
        kk = _dot(h, w_ref[:, N_QK:2 * N_QK])
        vv = _dot(h, w_ref[:, 2 * N_QK:N_A])
        rows = kk.shape[0]
        for hd in range(A_HEADS):
            ko_ref[0, 0, pl.ds(hd, rows, stride=A_HEADS), :] = kk[:, hd * LANES:(hd + 1) * LANES]
            vo_ref[0, 0, pl.ds(hd, rows, stride=A_HEADS), :] = vv[:, hd * LANES:(hd + 1) * LANES]
    zb_ref[...] = _dot(h, w_ref[:, N_A:N_A + N_B])
    zs = _dot(h, ws_ref[...])
    zs_ref[...] = zs
    for g in range(zs.shape[0] // TOK):
        for p in range(N_S // LANES):
            zsr_ref[p, g] = zs[g * TOK:(g + 1) * TOK, p * LANES:(p + 1) * LANES].T[0:8, :]
    zc_ref[...] = _dot(h, wc_ref[...])


def _in_proj(x, mod, w, seq_len, rope_tabs=None, kv_out=None, layer=0):
    n, d = x.shape
    rope = rope_tabs is not None
    tm = min(512, seq_len) if rope else seq_len
    tiles_per_seq = seq_len // tm
    rows_per_mod = n // mod.shape[0]
    in_specs = [
        pl.BlockSpec((tm, d), lambda i: (i, 0)),
        pl.BlockSpec((1, 1, mod.shape[2]), lambda i: ((i * tm) // rows_per_mod, 0, 0)),
        pl.BlockSpec((None, d, N_A + N_B), lambda i: (layer, 0, 0), pipeline_mode=pl.Buffered(1)),
        pl.BlockSpec((None, d, N_S), lambda i: (layer, 0, 0)),
        pl.BlockSpec((None, d, N_C), lambda i: (layer, 0, 0)),
    ]
    args = [x, mod] + list(w)
    row = lambda w_: pl.BlockSpec((tm, w_), lambda i: (i, 0))
    tail_w = [N_B, N_S, N_C]
    zsr_spec = pl.BlockSpec((N_S // LANES, tm // TOK, 8, TOK), lambda i: (0, i, 0, 0))
    zsr_shape = jax.ShapeDtypeStruct((N_S // LANES, n // TOK, 8, TOK), F32)
    if rope:
        in_specs += [pl.BlockSpec((tm, LANES), lambda i: (i % tiles_per_seq, 0))] * 3
        args += list(rope_tabs)
        out_specs = [row(N_A)] + [row(w_) for w_ in tail_w] + [zsr_spec]
        out_shape = ([jax.ShapeDtypeStruct((n, N_A), BF16)] + [jax.ShapeDtypeStruct((n, w_), F32) for w_ in tail_w]
                     + [zsr_shape])
        aliases = {}
    else:
        k_all, v_all = kv_out
        in_specs += [pl.BlockSpec(memory_space=pl.ANY)] * 2
        args += [k_all, v_all]
        kv_spec = lambda a: pl.BlockSpec((1, 1, tm * A_HEADS, a.shape[3]), lambda i: (i, layer, 0, 0))
        out_specs = [row(N_QK), kv_spec(k_all), kv_spec(v_all)] + [row(w_) for w_ in tail_w] + [zsr_spec]
        out_shape = ([jax.ShapeDtypeStruct((n, N_QK), F32), jax.ShapeDtypeStruct(k_all.shape, F32),
                      jax.ShapeDtypeStruct(v_all.shape, F32)]
                     + [jax.ShapeDtypeStruct((n, w_), F32) for w_ in tail_w] + [zsr_shape])
        aliases = {5: 1, 6: 2}
    return pl.pallas_call(
        functools.partial(_in_proj_kernel, d_model=d, rope=rope),
        grid=(n // tm,),
        in_specs=in_specs,
        out_specs=out_specs,
        out_shape=out_shape,
        input_output_aliases=aliases,
        scratch_shapes=[pltpu.VMEM((d, N_A + N_B), BF16)],
        compiler_params=_cparams("arbitrary"),
        name="in_proj_rope" if rope else "in_proj",
    )(*args)


def _rope_tables(seq_len):
    half = A_QK // 2
    nf = half // 2
    pos = jnp.arange(seq_len)
    row = (pos // GRID_W).astype(F32)
    col = (pos % GRID_W).astype(F32)
    inv_freq = ROPE_BASE ** (-jnp.arange(nf, dtype=F32) / nf)
    ang_r = row[:, None] * inv_freq
    ang_c = col[:, None] * inv_freq
    cos64 = jnp.concatenate([jnp.cos(ang_r), jnp.cos(ang_r), jnp.cos(ang_c), jnp.cos(ang_c)], -1)
    zero = jnp.zeros_like(ang_r)
    s1_64 = jnp.concatenate([-jnp.sin(ang_r), zero, -jnp.sin(ang_c), zero], -1)
    s2_64 = jnp.concatenate([zero, jnp.sin(ang_r), zero, jnp.sin(ang_c)], -1)
    rep = LANES // A_QK
    return tuple(jnp.tile(t, (1, rep)) for t in (cos64, s1_64, s2_64))


def _attn_kernel(q_ref, k_ref, v_ref, *rest, lam_init, latent, sb, tq, seq_len):
    if latent:
        ck_ref, cv_ref, dl_ref, nrm_ref, o_ref, s_scr, vx_scr = rest
        past = cv_ref.shape[2]

        @pl.when(pl.program_id(1) == 0)
        def _():
            ones = jnp.ones((seq_len, LANES), BF16)
            for h in range(A_HEADS):
                hs = slice(h * LANES, (h + 1) * LANES)
                vx_scr[0:seq_len, 2 * h * LANES:(2 * h + 1) * LANES] = v_ref[:, hs]
                vx_scr[0:seq_len, (2 * h + 1) * LANES:(2 * h + 2) * LANES] = ones
                vx_scr[seq_len:, 2 * h * LANES:(2 * h + 1) * LANES] = cv_ref[0, 0, :, hs].astype(BF16)
                vx_scr[seq_len:, (2 * h + 1) * LANES:(2 * h + 2) * LANES] = ones[0:past]
    else:
        dl_ref, nrm_ref, o_ref, s_scr = rest
    dl = dl_ref[0]
    lam = (jnp.exp(jnp.sum(dl[0:1] * dl[1:2], axis=1, keepdims=True))
           - jnp.exp(jnp.sum(dl[2:3] * dl[3:4], axis=1, keepdims=True)) + lam_init)
    lo = _lane_iota((tq, LANES)) < A_QK
    units = [(sq, h) for sq in range(sb) for h in range(A_HEADS)]

    def scores(unit, slot):
        sq, h = unit
        hs = slice(h * LANES, (h + 1) * LANES)
        q = q_ref[sq * tq:(sq + 1) * tq, hs]
        if latent:
            k = k_ref[:, hs]
            zero = jnp.zeros_like(q)
        else:
            q = q * Q_SCALE
            k = k_ref[sq, 0, pl.ds(h, seq_len, stride=A_HEADS), :].astype(BF16)
            zero = 0.0
        qq = jnp.concatenate([jnp.where(lo, q, zero), jnp.where(lo, zero, q)], axis=0).astype(BF16)
        s_scr[slot, :, 0:seq_len] = _dot_nt(qq, k)
        if latent:
            s_scr[slot, :, seq_len:] = _dot_nt(qq, ck_ref[0, 0, :, hs].astype(BF16))

    def finish(unit, slot):
        sq, h = unit
        hs = slice(h * LANES, (h + 1) * LANES)
        s = s_scr[slot]
        if latent:
            e = jnp.exp2((s - jnp.max(s, axis=-1, keepdims=True)).astype(BF16))
            o2 = _dot(e, vx_scr[:, 2 * h * LANES:(2 * h + 2) * LANES])
            r = 1.0 / o2[:, LANES:LANES + 1]
            o = o2[:tq, 0:LANES] * r[:tq] - o2[tq:, 0:LANES] * (lam * r[tq:])
        else:
            e = jnp.exp2(s - jnp.max(s, axis=-1, keepdims=True))
            r = 1.0 / jnp.sum(e, axis=-1, keepdims=True)
            p = (e[:tq] * r[:tq] - e[tq:] * (lam * r[tq:])).astype(BF16)
            o = _dot(p, v_ref[sq, 0, pl.ds(h, seq_len, stride=A_HEADS), :].astype(BF16))
        o = o * lax.rsqrt(jnp.mean(o * o, axis=-1, keepdims=True) + RMS_EPS) * nrm_ref[0]
        o_ref[sq * tq:(sq + 1) * tq, hs] = o * (1.0 - lam_init)

    scores(units[0], 0)
    for i, unit in enumerate(units):
        if i + 1 < len(units):
            scores(units[i + 1], (i + 1) % 2)
        finish(unit, i % 2)


def _attention(q_arr, seq_len, layer, diff_lambda, diff_norm, kv_all=None, ctx_kv=None):
    n = q_arr.shape[0]
    n_seq = n // seq_len
    latent = ctx_kv is not None
    lam_init = 0.8 - 0.6 * math.exp(-0.3 * layer)
    if latent:
        sb, tq = 1, min(128, seq_len)
        nq = seq_len // tq
        col = lambda c: pl.BlockSpec((seq_len, N_QK), lambda s, i: (s, c))
        ck, cv = ctx_kv
        cspec = lambda a: pl.BlockSpec((1, 1) + a.shape[2:], lambda s, i: (s, layer, 0, 0))
        in_specs = [pl.BlockSpec((tq, N_QK), lambda s, i: (s * nq + i, 0)), col(1), col(2), cspec(ck), cspec(cv)]
        args = [q_arr, q_arr, q_arr, ck, cv]
    else:
        sb, tq, nq = min(4, n_seq), seq_len, 1
        k_all, v_all = kv_all
        kspec = lambda a: pl.BlockSpec((sb, 1) + a.shape[2:], lambda s, i: (s, layer, 0, 0))
        in_specs = [pl.BlockSpec((sb * tq, N_QK), lambda s, i: (s, 0)), kspec(k_all), kspec(v_all)]
        args = [q_arr, k_all, v_all]
    in_specs += [
        pl.BlockSpec((1, 4, A_QK), lambda s, i: (layer, 0, 0)),
        pl.BlockSpec((1, 1, A_V), lambda s, i: (layer, 0, 0)),
    ]
    args += [diff_lambda, diff_norm.reshape(diff_norm.shape[0], 1, A_V)]
    keys = seq_len + (ctx_kv[0].shape[2] if latent else 0)
    return pl.pallas_call(
        functools.partial(_attn_kernel, lam_init=lam_init, latent=latent, sb=sb, tq=tq, seq_len=seq_len),
        grid=(n_seq // sb, nq),
        in_specs=in_specs,
        out_specs=pl.BlockSpec((sb * tq, A_HEADS * A_V), lambda s, i: (s * nq + i, 0)),
        out_shape=jax.ShapeDtypeStruct((n, A_HEADS * A_V), F32),
        scratch_shapes=[pltpu.VMEM((2, 2 * tq, keys), F32)]
        + ([pltpu.VMEM((keys, 2 * A_HEADS * A_V), BF16)] if latent else []),
        compiler_params=_cparams("parallel", "arbitrary"),
        name="diff_attn_latent" if latent else "diff_attn",
    )(*args)


def _head_sum(x):
    lane = _lane_iota(x.shape)
    lo = lane < 64
    s0 = jnp.sum(jnp.where(lo, x, 0.0), axis=-1, keepdims=True)
    s1 = jnp.sum(jnp.where(lo, 0.0, x), axis=-1, keepdims=True)
    return jnp.where(lo, s0, s1)


def _pair_blockdiag(a, b):
    z = jnp.zeros_like(a)
    return jnp.concatenate([jnp.concatenate([a, z], axis=1), jnp.concatenate([z, b], axis=1)], axis=0)


def _gated_head_norm(o, norm_row, gate):
    parts = []
    for hp in range(o.shape[1] // LANES):
        blk = o[:, hp * LANES:(hp + 1) * LANES]
        parts.append(blk * lax.rsqrt(_head_sum(blk * blk) * (1.0 / 64.0) + RMS_EPS))
    return jnp.concatenate(parts, axis=1) * norm_row * _silu(gate)


DELTA_MERGES = (4, 8, 16, 32, 64)
M_CAUSAL, M_STRICT, M_PAIR, M_MERGE0 = 0, 1, 2, 3
M_EYE = M_MERGE0 + len(DELTA_MERGES)
M_HEADS = M_EYE + 1


def _delta_consts():
    t = np.arange(TOK)[:, None]
    s = np.arange(TOK)[None, :]

    def same(n):
        return (t // n) == (s // n)

    masks, cums = [], []
    for d in range(2):
        before = (s <= t) if d == 0 else (s >= t)
        strict = (s < t) if d == 0 else (s > t)
        causal = same(DELTA_CHUNK) & before
        st = same(DELTA_CHUNK) & strict
        rows = [causal, st, st & same(2)]
        rows += [st & same(n) & ~same(n // 2) for n in DELTA_MERGES]
        rows += [t == s, same(64)]
        masks.append(np.stack(rows))
        cums.append(np.concatenate([causal, same(DELTA_CHUNK) & ~before, same(DELTA_CHUNK)], 0))
    cums = np.stack(cums)
    return (jnp.asarray(np.stack(masks), F32), jnp.asarray(np.tile(cums, (1, 1, 3)), BF16),
            jnp.asarray(cums[:, 0:TOK], BF16))


def _delta_prep_kernel(z_ref, zp_ref, zn_ref, sc_ref, sr_ref, cw_ref, prow_ref, pcol_ref, msk_ref, cum_ref, cumr_ref,
                       u_ref, w_ref, qd_ref, qkm_ref, kdt_ref, gl_ref, *, blocks_per_seq, g_tiles):
    jloc = pl.program_id(0) % blocks_per_seq
    rb = g_tiles * TOK
    nqk = B_HEADS * B_DK
    pairs = B_HEADS // 2
    x = z_ref[...]
    cw = cw_ref[0]
    prev = jnp.where(jloc > 0, zp_ref[7:8, :], 0.0)
    nxt = jnp.where(jloc < blocks_per_seq - 1, zn_ref[0:1, :], 0.0)
    row = _row_iota(x.shape)
    dn = jnp.where(row == 0, prev, pltpu.roll(x, 1, axis=0))
    up = jnp.where(row == rb - 1, nxt, pltpu.roll(x, rb - 1, axis=0))
    y = _silu(cw[0:1] * dn + cw[1:2] * x + cw[2:3] * up)

    lo = _lane_iota((TOK, LANES)) < 64
    hi = jnp.logical_not(lo)
    units = [(g, p) for g in range(g_tiles) for p in range(pairs)]
    chains = [(g, p, d, hh) for (g, p) in units for d in range(2) for hh in range(2)]

    q, k, v, kk, qk, gcol, grow = {}, {}, {}, {}, {}, {}, {}
    for (g, p) in units:
        rs = slice(g * TOK, (g + 1) * TOK)
        qs = y[rs, p * LANES:(p + 1) * LANES]
        ks = y[rs, nqk + p * LANES:nqk + (p + 1) * LANES]
        v[g, p] = y[rs, 2 * nqk + p * LANES:2 * nqk + (p + 1) * LANES]
        q[g, p] = qs * lax.rsqrt(_head_sum(qs * qs) + L2_EPS) * (B_DK ** -0.5)
        k[g, p] = ks * lax.rsqrt(_head_sum(ks * ks) + L2_EPS)
    for (g, p) in units:
        k16 = k[g, p].astype(BF16)
        for hh, sel in enumerate((lo, hi)):
            kk[g, p, hh] = _dot_nt(jnp.where(sel, k[g, p], 0.0).astype(BF16), k16)
            qk[g, p, hh] = _dot_nt(jnp.where(sel, q[g, p], 0.0).astype(BF16), k16)
    for (g, p) in units:
        prow = prow_ref[0, p]
        pcol = pcol_ref[0, p]
        xs = sc_ref[g * TOK:(g + 1) * TOK, p * LANES:(p + 1) * LANES]
        lane = _lane_iota(xs.shape)
        gcol[g, p] = jnp.where(lane < 4, _sigmoid(xs), -jnp.exp(prow[0:1]) * _softplus(xs + prow[1:2]))
        xr = sr_ref[p, g]
        rowi = _row_iota(xr.shape)
        grow[g, p] = jnp.where(rowi < 4, _sigmoid(xr), -jnp.exp(pcol[:, 0:1]) * _softplus(xr + pcol[:, 1:2]))
    cs, br = {}, {}
    for g in range(g_tiles):
        gc = jnp.concatenate([gcol[g, p] for p in range(pairs)], axis=1)
        gr = jnp.concatenate([grow[g, p] for p in range(pairs)], axis=0)
        for d in range(2):
            csd = _dot01(cum_ref[d], gc)
            brd = _dot01_nt(gr, cumr_ref[d])
            for p in range(pairs):
                cs[g, p, d] = csd[:, p * LANES:(p + 1) * LANES]
                br[g, p, d] = brd[p * 8:(p + 1) * 8]
    beta, bcol, dec, m, x = {}, {}, {}, {}, {}
    for ch in chains:
        g, p, d, hh = ch
        cb, cg = d * 2 + hh, 4 + d * 2 + hh
        beta[ch] = gcol[g, p][:, cb:cb + 1]
        bcol[ch] = cs[g, p, d][0:TOK, cg:cg + 1]
        dec[ch] = jnp.exp(jnp.minimum(bcol[ch] - br[g, p, d][cg:cg + 1, :], 0.0)) * msk_ref[d, M_CAUSAL]
        m[ch] = kk[g, p, hh] * beta[ch] * dec[ch] * msk_ref[d, M_STRICT]
        x[ch] = msk_ref[d, M_EYE] - m[ch] * msk_ref[d, M_PAIR]
    for lvl in range(len(DELTA_MERGES)):
        yv = {ch: _bdot(m[ch] * msk_ref[ch[2], M_MERGE0 + lvl], x[ch]) for ch in chains}
        x = {ch: x[ch] - _bdot(x[ch], yv[ch]) for ch in chains}
    for (g, p) in units:
        rs = slice(g * TOK, (g + 1) * TOK)
        for d in range(2):
            sol = None
            eq, ek, gl, qkm = [], [], [], []
            for hh, sel in enumerate((lo, hi)):
                ch = (g, p, d, hh)
                cg = 4 + d * 2 + hh
                eb = jnp.exp(bcol[ch])
                rhs = jnp.concatenate([jnp.where(sel, v[g, p] * beta[ch], 0.0),
                                       jnp.where(sel, k[g, p] * (beta[ch] * eb), 0.0)], axis=1)
                part = _bdot(x[ch], rhs)
                sol = part if sol is None else sol + part
                qkm.append((qk[g, p, hh] * dec[ch]).astype(BF16))
                eq.append(eb)
                ek.append(jnp.exp(cs[g, p, d][TOK:2 * TOK, cg:cg + 1]))
                gl.append(jnp.exp(cs[g, p, d][2 * TOK:3 * TOK, cg:cg + 1]))
            u_ref[d, p, rs, :] = sol[:, 0:LANES]
            w_ref[d, p, rs, :] = sol[:, LANES:2 * LANES].astype(BF16)
            qd_ref[d, p, rs, :] = (q[g, p] * jnp.where(lo, eq[0], eq[1])).astype(BF16)
            qkm_ref[d, p, rs, :] = jnp.concatenate(qkm, axis=1)
            kd = k[g, p] * jnp.where(lo, ek[0], ek[1])
            kdt_ref[d, p, rs, :] = kd.T.astype(BF16)
            glf = jnp.where(lo, gl[0], gl[1])
            gl_ref[d, p, g * 16:(g + 1) * 16, :] = jnp.concatenate([glf[0:8], glf[64:72]], axis=0)


def _delta_prep(zb, zs, zs_rows, seq_len, layer, conv_w, prow, pcol, consts):
    n = zb.shape[0]
    pairs = B_HEADS // 2
    g_tiles = 2
    rb = g_tiles * TOK
    blocks_per_seq = seq_len // rb
    n_blocks = n // rb
    nconv = conv_w.shape[2]
    msk, cum, cum_rows = consts
    last8 = n // 8 - 1
    out_w = [LANES, LANES, LANES, 2 * LANES, LANES]
    out_t = [F32, BF16, BF16, BF16, BF16]
    return pl.pallas_call(
        functools.partial(_delta_prep_kernel, blocks_per_seq=blocks_per_seq, g_tiles=g_tiles),
        grid=(n_blocks,),
        in_specs=[
            pl.BlockSpec((rb, nconv), lambda i: (i, 0)),
            pl.BlockSpec((8, nconv), lambda i: (jnp.maximum(i * (rb // 8) - 1, 0), 0)),
            pl.BlockSpec((8, nconv), lambda i: (jnp.minimum((i + 1) * (rb // 8), last8), 0)),
            pl.BlockSpec((rb, pairs * LANES), lambda i: (i, 0)),
            pl.BlockSpec((pairs, g_tiles, 8, TOK), lambda i: (0, i, 0, 0)),
            pl.BlockSpec((1, 3, nconv), lambda i: (layer, 0, 0)),
            pl.BlockSpec((1, pairs, 8, LANES), lambda i: (layer, 0, 0, 0)),
            pl.BlockSpec((1, pairs, 8, LANES), lambda i: (layer, 0, 0, 0)),
            pl.BlockSpec(msk.shape, lambda i: (0, 0, 0, 0)),
            pl.BlockSpec(cum.shape, lambda i: (0, 0, 0)),
            pl.BlockSpec(cum_rows.shape, lambda i: (0, 0, 0)),
        ],
        out_specs=[pl.BlockSpec((2, pairs, rb, w_), lambda i: (0, 0, i, 0)) for w_ in out_w]
        + [pl.BlockSpec((2, pairs, g_tiles * 16, LANES), lambda i: (0, 0, i, 0))],
        out_shape=[jax.ShapeDtypeStruct((2, pairs, n, w_), t_) for w_, t_ in zip(out_w, out_t)]
        + [jax.ShapeDtypeStruct((2, pairs, n // TOK * 16, LANES), F32)],
        compiler_params=_cparams("parallel"),
        name="delta_prep",
    )(zb, zb, zb, zs, zs_rows, conv_w, prow, pcol, msk, cum, cum_rows)


def _delta_scan_kernel(*refs, sb, tb, has_s0):
    fwd, bwd = refs[0:6], refs[6:12]
    hm_ref = refs[12]
    if has_s0:
        s0_ref, of_ref, ob_ref, s_s = refs[13:]
    else:
        _, of_ref, ob_ref, sout_ref, s_s = refs[13:]
    t = pl.program_id(1)
    n_t = pl.num_programs(1)
    n_g = tb // TOK
    pairs = B_HEADS // 2
    chains = [(d, sq, p) for d in range(2) for sq in range(sb) for p in range(pairs)]

    @pl.when(t == 0)
    def _():
        for (d, sq, p) in chains:
            if has_s0:
                s_s[d, sq, p] = _pair_blockdiag(s0_ref[sq, 0, d, 2 * p], s0_ref[sq, 0, d, 2 * p + 1])
            else:
                s_s[d, sq, p] = jnp.zeros((LANES, LANES), F32)

    lane64 = _lane_iota((DELTA_CHUNK, LANES)) < 64
    zpad = jnp.zeros((DELTA_CHUNK, LANES), BF16)

    def body(gi, carry):
        for step in range(2):
            tmp = {}
            for ch in chains:
                d, sq, p = ch
                u_r, w_r = (fwd, bwd)[d][0:2]
                g = gi if d == 0 else n_g - 1 - gi
                c = step if d == 0 else 1 - step
                rc = pl.multiple_of(g * TOK + c * DELTA_CHUNK, DELTA_CHUNK)
                s = s_s[d, sq, p]
                s16 = s.astype(BF16)
                vn = u_r[0, p, sq, pl.ds(rc, DELTA_CHUNK), :] - _dot(w_r[0, p, sq, pl.ds(rc, DELTA_CHUNK), :], s16)
                tmp[ch] = (s, s16, vn, g, c, rc)
            for ch in chains:
                d, sq, p = ch
                _, _, qd_r, qkm_r, kdt_r, gl_r = (fwd, bwd)[d]
                o_r = (of_ref, ob_ref)[d]
                s, s16, vn, g, c, rc = tmp[ch]
                r0 = pl.multiple_of(g * TOK, TOK)
                v0 = jnp.where(lane64, vn, 0.0).astype(BF16)
                v1 = jnp.where(lane64, 0.0, vn).astype(BF16)
                vnb = vn.astype(BF16)
                if c == 0:
                    vext = jnp.concatenate([vnb, zpad], axis=0)
                    v2 = jnp.concatenate([v0, zpad, v1, zpad], axis=0)
                else:
                    vext = jnp.concatenate([zpad, vnb], axis=0)
                    v2 = jnp.concatenate([zpad, v0, zpad, v1], axis=0)
                o = (_dot(qd_r[0, p, sq, pl.ds(rc, DELTA_CHUNK), :], s16)
                     + _dot(qkm_r[0, p, sq, pl.ds(rc, DELTA_CHUNK), :], v2))
                o_r[sq, pl.ds(rc, DELTA_CHUNK), p * LANES:(p + 1) * LANES] = o
                glr = gl_r[0, p, sq, pl.ds(pl.multiple_of(g * 16 + c * 8, 8), 8), :][0:1]
                s_s[d, sq, p] = s * glr + _dot(kdt_r[0, p, sq, pl.ds(r0, TOK), :], vext) * hm_ref[...]
        return carry

    lax.fori_loop(0, n_g, body, 0)

    if not has_s0:
        @pl.when(t == n_t - 1)
        def _():
            for (d, sq, p) in chains:
                s = s_s[d, sq, p]
                sout_ref[sq, 0, d, 2 * p] = s[0:64, 0:64]
                sout_ref[sq, 0, d, 2 * p + 1] = s[64:128, 64:128]


def _delta_scan(prep, n_seq, seq_len, layer, head_mask, s0=None, s_all=None):
    pairs = B_HEADS // 2
    has_s0 = s0 is not None
    sb = n_seq if n_seq <= 2 else 8
    tb = min(seq_len, 512)
    n_t = seq_len // tb
    arrs = [a.reshape(2, pairs, n_seq, a.shape[2] // n_seq, a.shape[3]) for a in prep]

    def specs(d):
        tmap = (lambda s, t: (d, 0, s, t, 0)) if d == 0 else (lambda s, t: (d, 0, s, n_t - 1 - t, 0))
        return [pl.BlockSpec((1, pairs, sb, a.shape[3] // n_t, a.shape[4]), tmap) for a in arrs]

    in_specs = specs(0) + specs(1) + [pl.BlockSpec((TOK, LANES), lambda s, t: (0, 0))]
    args = arrs + arrs + [head_mask]
    st_spec = pl.BlockSpec((sb, 1, 2, B_HEADS, B_DK, B_DV), lambda s, t: (s, layer, 0, 0, 0, 0))
    width = B_HEADS * B_DV
    out_specs = [
        pl.BlockSpec((sb, tb, width), lambda s, t: (s, t, 0)),
        pl.BlockSpec((sb, tb, width), lambda s, t: (s, n_t - 1 - t, 0)),
    ]
    out_shape = [jax.ShapeDtypeStruct((n_seq, seq_len, width), F32)] * 2
    if has_s0:
        in_specs.append(st_spec)
        args.append(s0)
        aliases = {}
    else:
        in_specs.append(pl.BlockSpec(memory_space=pl.ANY))
        args.append(s_all)
        out_specs.append(st_spec)
        out_shape.append(jax.ShapeDtypeStruct(s_all.shape, F32))
        aliases = {len(args) - 1: 2}
    outs = pl.pallas_call(
        functools.partial(_delta_scan_kernel, sb=sb, tb=tb, has_s0=has_s0),
        grid=(n_seq // sb, n_t),
        in_specs=in_specs,
        out_specs=out_specs,
        out_shape=out_shape,
        input_output_aliases=aliases,
        scratch_shapes=[pltpu.VMEM((2, sb, pairs, LANES, LANES), F32)],
        compiler_params=_cparams("parallel", "arbitrary"),
        name="delta_scan_s0" if has_s0 else "delta_scan",
    )(*args)
    n = n_seq * seq_len
    return (outs[0].reshape(n, width), outs[1].reshape(n, width)) + tuple(outs[2:])


HGRN_LEVELS = 7
HGRN_WROWS = (HGRN_LEVELS + 2) * TOK + 8
HGRN_SPAN_TERMS = 2


def _hgrn_consts():
    t = np.arange(TOK)
    masks = [t[:, None] == t[None, :]]
    for lvl in range(1, HGRN_LEVELS + 1):
        n = 1 << lvl
        masks.append((t[:, None] // n) == (t[None, :] // n))
    ws = []
    for d in range(2):
        tau = t if d == 0 else TOK - 1 - t
        tt, ti = tau[:, None], tau[None, :]
        blocks = []
        for lvl in range(1, HGRN_LEVELS + 1):
            n = 1 << lvl
            piv = (tau - tau % n + n // 2 - 1)[:, None]
            upper = ((tau % n) >= n // 2)[:, None]
            blocks.append(np.where(upper, (ti > piv) & (ti <= tt), (ti > tt) & (ti <= piv)))
        blocks.append(ti <= tt)
        blocks.append(ti > tt)
        blocks.append(np.ones((8, TOK), bool))
        ws.append(np.concatenate(blocks, 0))
    return jnp.asarray(np.stack(masks), F32), jnp.asarray(np.tile(np.stack(ws), (1, 1, HGRN_SPAN_TERMS)), BF16)


def _hgrn_prep_kernel(z_ref, lb_ref, msk_ref, w_ref, intra_ref, qd_ref, kv_ref, gl_ref, *, layer, g_tiles):
    pairs = C_HEADS // 2
    nk = C_HEADS * C_DK
    lo = _lane_iota((TOK, LANES)) < 64
    row = _row_iota((TOK, LANES))
    units = [(g, p) for g in range(g_tiles) for p in range(pairs)]
    chains = [(g, p, d) for (g, p) in units for d in range(2)]

    lbs = {}
    for p in range(pairs):
        for d in range(2):
            x = lb_ref[p, d]
            e = jnp.exp(x - jnp.max(x, axis=0, keepdims=True))
            sm = e / jnp.sum(e, axis=0, keepdims=True)
            if layer > 0:
                lbs[p, d] = jnp.sum(sm[1:layer + 1], axis=0, keepdims=True)
            else:
                lbs[p, d] = jnp.zeros((1, LANES), F32)

    def stack_heads(x):
        return jnp.concatenate([jnp.where(lo, x, 0.0), jnp.where(lo, 0.0, x)], axis=0).astype(BF16)

    def both_heads(mask):
        return jnp.concatenate([mask, mask], axis=1)

    q, vt, key, ex, acc = {}, {}, {}, {}, {}
    for (g, p) in units:
        rs = slice(g * TOK, (g + 1) * TOK)
        q[g, p] = _silu(z_ref[rs, p * LANES:(p + 1) * LANES])
        vt[g, p] = z_ref[rs, 3 * nk + p * LANES:3 * nk + (p + 1) * LANES].T
    for g in range(g_tiles):
        for d in range(2):
            lg = []
            for p in range(pairs):
                f = z_ref[g * TOK:(g + 1) * TOK, (1 + d) * nk + p * LANES:(1 + d) * nk + (p + 1) * LANES]
                forget = lbs[p, d] + (1.0 - lbs[p, d]) * _sigmoid(f)
                key[g, p, d] = 1.0 - forget
                lg.append(jnp.log(forget))
            spans = _dot01(w_ref[d], jnp.concatenate(lg, axis=1), terms=HGRN_SPAN_TERMS)
            for p in range(pairs):
                ex[g, p, d] = spans[:, p * LANES:(p + 1) * LANES]
    for ch in chains:
        acc[ch] = _dot_nt(key[ch].astype(BF16), stack_heads(q[ch[0], ch[1]])) * both_heads(msk_ref[0])
    for lvl in range(1, HGRN_LEVELS + 1):
        for ch in chains:
            g, p, d = ch
            tau = row if d == 0 else TOK - 1 - row
            e = jnp.exp(ex[ch][(lvl - 1) * TOK:lvl * TOK])
            up = (tau & (1 << (lvl - 1))) != 0
            qt = jnp.where(up, q[g, p] * e, 0.0)
            kt = jnp.where(up, 0.0, key[ch] * e)
            a = _dot_nt(kt.astype(BF16), stack_heads(qt))
            if lvl < HGRN_LEVELS:
                a = a * both_heads(msk_ref[lvl])
            acc[ch] = acc[ch] + a
    base = HGRN_LEVELS * TOK
    for ch in chains:
        g, p, d = ch
        rs = slice(g * TOK, (g + 1) * TOK)
        vv = vt[g, p]
        intra_t = (_dot(jnp.where(row < 64, vv, 0.0).astype(BF16), acc[ch][:, 0:LANES].astype(BF16))
                   + _dot(jnp.where(row < 64, 0.0, vv).astype(BF16), acc[ch][:, LANES:2 * LANES].astype(BF16)))
        intra_ref[d, p, rs, :] = intra_t.T
        qd_ref[d, p, rs, :] = (q[g, p] * jnp.exp(ex[ch][base:base + TOK])).astype(BF16)
        kd = (key[ch] * jnp.exp(ex[ch][base + TOK:base + 2 * TOK])).astype(BF16)
        kv_ref[d, p, rs, :] = _dot(vv.astype(BF16), kd) * msk_ref[6]
        gl_ref[d, p, g * 8:(g + 1) * 8, :] = jnp.exp(ex[ch][base + 2 * TOK:base + 2 * TOK + 8])


def _hgrn_prep(zc, layer, lb, consts):
    n = zc.shape[0]
    pairs = C_HEADS // 2
    g_tiles = 2
    rb = g_tiles * TOK
    msk, wst = consts
    n_in = 4 * C_HEADS * C_DK
    return pl.pallas_call(
        functools.partial(_hgrn_prep_kernel, layer=layer, g_tiles=g_tiles),
        grid=(n // rb,),
        in_specs=[
            pl.BlockSpec((rb, n_in), lambda i: (i, 0)),
            pl.BlockSpec(lb.shape, lambda i: (0, 0, 0, 0)),
            pl.BlockSpec(msk.shape, lambda i: (0, 0, 0)),
            pl.BlockSpec(wst.shape, lambda i: (0, 0, 0)),
        ],
        out_specs=[pl.BlockSpec((2, pairs, rb, LANES), lambda i: (0, 0, i, 0))] * 3
        + [pl.BlockSpec((2, pairs, g_tiles * 8, LANES), lambda i: (0, 0, i, 0))],
        out_shape=[jax.ShapeDtypeStruct((2, pairs, n, LANES), t_) for t_ in (F32, BF16, F32)]
        + [jax.ShapeDtypeStruct((2, pairs, n // TOK * 8, LANES), F32)],
        compiler_params=_cparams("parallel"),
        name="hgrn_prep",
    )(zc, lb, msk, wst)


def _hgrn_scan_kernel(*refs, sb, tb, has_s0):
    fwd, bwd = refs[0:4], refs[4:8]
    if has_s0:
        s0_ref, of_ref, ob_ref, s_s = refs[8:]
    else:
        _, of_ref, ob_ref, sout_ref, s_s = refs[8:]
    t = pl.program_id(1)
    n_t = pl.num_programs(1)
    n_g = tb // TOK
    pairs = C_HEADS // 2
    chains = [(d, sq, p) for d in range(2) for sq in range(sb) for p in range(pairs)]

    @pl.when(t == 0)
    def _():
        for (d, sq, p) in chains:
            if has_s0:
                s_s[d, sq, p] = _pair_blockdiag(s0_ref[sq, 0, d, 2 * p], s0_ref[sq, 0, d, 2 * p + 1])
            else:
                s_s[d, sq, p] = jnp.zeros((LANES, LANES), F32)

    def body(gi, carry):
        for (d, sq, p) in chains:
            intra_r, qd_r, kv_r, gl_r = (fwd, bwd)[d]
            o_r = (of_ref, ob_ref)[d]
            g = gi if d == 0 else n_g - 1 - gi
            r0 = pl.multiple_of(g * TOK, TOK)
            st = s_s[d, sq, p]
            o = intra_r[0, p, sq, pl.ds(r0, TOK), :] + _dot_nt(qd_r[0, p, sq, pl.ds(r0, TOK), :], st.astype(BF16))
            o_r[sq, pl.ds(r0, TOK), p * LANES:(p + 1) * LANES] = o
            glr = gl_r[0, p, sq, pl.ds(pl.multiple_of(g * 8, 8), 8), :][0:1]
            s_s[d, sq, p] = st * glr + kv_r[0, p, sq, pl.ds(r0, TOK), :]
        return carry

    lax.fori_loop(0, n_g, body, 0)

    if not has_s0:
        @pl.when(t == n_t - 1)
        def _():
            for (d, sq, p) in chains:
                s = s_s[d, sq, p].T
                sout_ref[sq, 0, d, 2 * p] = s[0:64, 0:64]
                sout_ref[sq, 0, d, 2 * p + 1] = s[64:128, 64:128]


def _hgrn_scan(prep, n_seq, seq_len, layer, s0t=None, s_all=None):
    pairs = C_HEADS // 2
    has_s0 = s0t is not None
    sb = n_seq if n_seq <= 2 else 8
    tb = min(seq_len, 512)
    n_t = seq_len // tb
    arrs = [a.reshape(2, pairs, n_seq, a.shape[2] // n_seq, a.shape[3]) for a in prep]

    def specs(d):
        tmap = (lambda s, t: (d, 0, s, t, 0)) if d == 0 else (lambda s, t: (d, 0, s, n_t - 1 - t, 0))
        return [pl.BlockSpec((1, pairs, sb, a.shape[3] // n_t, a.shape[4]), tmap) for a in arrs]

    in_specs = specs(0) + specs(1)
    args = arrs + arrs
    st_spec = pl.BlockSpec((sb, 1, 2, C_HEADS, C_DK, C_DV), lambda s, t: (s, layer, 0, 0, 0, 0))
    width = C_HEADS * C_DV
    out_specs = [
        pl.BlockSpec((sb, tb, width), lambda s, t: (s, t, 0)),
        pl.BlockSpec((sb, tb, width), lambda s, t: (s, n_t - 1 - t, 0)),
    ]
    out_shape = [jax.ShapeDtypeStruct((n_seq, seq_len, width), F32)] * 2
    if has_s0:
        in_specs.append(st_spec)
        args.append(s0t)
        aliases = {}
    else:
        in_specs.append(pl.BlockSpec(memory_space=pl.ANY))
        args.append(s_all)
        out_specs.append(st_spec)
        out_shape.append(jax.ShapeDtypeStruct(s_all.shape, F32))
        aliases = {len(args) - 1: 2}
    outs = pl.pallas_call(
        functools.partial(_hgrn_scan_kernel, sb=sb, tb=tb, has_s0=has_s0),
        grid=(n_seq // sb, n_t),
        in_specs=in_specs,
        out_specs=out_specs,
        out_shape=out_shape,
        input_output_aliases=aliases,
        scratch_shapes=[pltpu.VMEM((2, sb, pairs, LANES, LANES), F32)],
        compiler_params=_cparams("parallel", "arbitrary"),
        name="hgrn_scan_s0" if has_s0 else "hgrn_scan",
    )(*args)
    n = n_seq * seq_len
    return (outs[0].reshape(n, width), outs[1].reshape(n, width)) + tuple(outs[2:])


FFN_SPLIT = 2


def _tail_kernel(x_ref, oa_ref, obf_ref, obb_ref, bg_ref, ocf_ref, ocb_ref, cg_ref, mod_ref, bn_ref, cn_ref,
                 wo_ref, lng_ref, lnb_ref, wi_ref, wd_ref, o_ref, *, d_model, d_ff, alpha):
    m = mod_ref[0]
    na, nb = oa_ref.shape[1], obf_ref.shape[1]
    ob = _gated_head_norm(obf_ref[...] + obb_ref[...], bn_ref[0], bg_ref[...])
    oc = _gated_head_norm(ocf_ref[...] + ocb_ref[...], cn_ref[0], cg_ref[...])
    y = _bdot(oa_ref[...], wo_ref[0:na]) + _bdot(ob, wo_ref[na:na + nb]) + _bdot(oc, wo_ref[na + nb:])
    x1 = _layer_norm(alpha * x_ref[...] + m[:, 2 * d_model:3 * d_model] * y, lng_ref[0, 0:1], lnb_ref[0, 0:1])
    h = (x1 * (1.0 + m[:, 4 * d_model:5 * d_model]) + m[:, 3 * d_model:4 * d_model]).astype(BF16)
    ck = d_ff // FFN_SPLIT
    acc = None
    for c0 in range(0, d_ff, ck):
        gt = _dot(h, wi_ref[:, c0:c0 + ck])
        up = _dot(h, wi_ref[:, d_ff + c0:d_ff + c0 + ck])
        part = _dot((_silu(gt) * up).astype(BF16), wd_ref[c0:c0 + ck, :])
        acc = part if acc is None else acc + part
    o_ref[...] = _layer_norm(alpha * x1 + m[:, 5 * d_model:6 * d_model] * acc, lng_ref[0, 1:2], lnb_ref[0, 1:2])


def _tail(x, oa, obf, obb, zb, ocf, ocb, zc, mod, layer, bnorm, cnorm, w_out, ln_g, ln_b, w_ffn_in, w_ffn_out,
          alpha):
    n, d = x.shape
    tm = 512
    d_ff = w_ffn_out.shape[1]
    rows_per_mod = n // mod.shape[0]
    nb, nc = obf.shape[1], ocf.shape[1]
    row = lambda w_: pl.BlockSpec((tm, w_), lambda i: (i, 0))
    last_cols = lambda a, w_: pl.BlockSpec((tm, w_), lambda i: (i, a.shape[1] // w_ - 1))
    per_layer = lambda w_: pl.BlockSpec((1, 1, w_), lambda i: (layer, 0, 0))
    resident = lambda a: pl.BlockSpec((None,) + a.shape[1:], lambda i: (layer, 0, 0), pipeline_mode=pl.Buffered(1))
    return pl.pallas_call(
        functools.partial(_tail_kernel, d_model=d, d_ff=d_ff, alpha=alpha),
        grid=(n // tm,),
        in_specs=[
            row(d), row(oa.shape[1]),
            row(nb), row(nb), last_cols(zb, nb),
            row(nc), row(nc), last_cols(zc, nc),
            pl.BlockSpec((1, 1, mod.shape[2]), lambda i: ((i * tm) // rows_per_mod, 0, 0)),
            per_layer(nb), per_layer(nc),
            resident(w_out),
            pl.BlockSpec((1, 2, d), lambda i: (layer, 0, 0)),
            pl.BlockSpec((1, 2, d), lambda i: (layer, 0, 0)),
            resident(w_ffn_in), resident(w_ffn_out),
        ],
        out_specs=pl.BlockSpec((tm, d), lambda i: (i, 0)),
        out_shape=jax.ShapeDtypeStruct((n, d), F32),
        compiler_params=_cparams("parallel"),
        name="out_proj_ffn",
    )(x, oa, obf, obb, zb, ocf, ocb, zc, mod, bnorm, cnorm, w_out, ln_g, ln_b, w_ffn_in, w_ffn_out)


def _pair_scalar_order(p):
    return [d * B_HEADS + 2 * p + hh for d in range(2) for hh in range(2)]


def _layout_w_in(w_in):
    depth, d, _ = w_in.shape
    beta0 = N_A + N_B
    dec0 = beta0 + 2 * B_HEADS
    c0 = dec0 + 2 * B_HEADS
    parts = []
    for p in range(B_HEADS // 2):
        order = _pair_scalar_order(p)
        cols = [beta0 + i for i in order] + [dec0 + i for i in order]
        parts.append(w_in[:, :, np.asarray(cols)].astype(BF16))
        parts.append(jnp.zeros((depth, d, LANES - len(cols)), BF16))
    return w_in, jnp.concatenate(parts, axis=-1), w_in[:, :, c0:].astype(BF16)


def _delta_params(delta_a_log, delta_dt_bias):
    depth = delta_a_log.shape[0]
    pairs = B_HEADS // 2
    al = delta_a_log.reshape(depth, 2 * B_HEADS)
    dt = delta_dt_bias.reshape(depth, 2 * B_HEADS)
    prow = jnp.zeros((depth, pairs, 8, LANES), F32)
    pcol = jnp.zeros((depth, pairs, 8, LANES), F32)
    for p in range(pairs):
        order = np.asarray(_pair_scalar_order(p))
        prow = prow.at[:, p, 0, 4:8].set(al[:, order]).at[:, p, 1, 4:8].set(dt[:, order])
        pcol = pcol.at[:, p, 4:8, 0].set(al[:, order]).at[:, p, 4:8, 1].set(dt[:, order])
    return prow, pcol


def kernel(x_prompt, x_sample, cache_attn_k, cache_attn_v, state_delta, state_hgrn, c, c_ctx, w_mod, b_mod, w_in, conv_w, delta_a_log, delta_dt_bias, delta_norm, hgrn_lb, hgrn_norm, diff_lambda, diff_norm, w_out, ln_g, ln_b, w_ffn_in, w_ffn_out):
    depth, d, _ = w_in.shape
    bp, lp, _ = x_prompt.shape
    bs, ls, _ = x_sample.shape
    alpha = (2 * depth) ** 0.25

    cond_rows = 8 * ((1 + bs + 7) // 8)
    cond = jnp.concatenate([c_ctx[None], c, jnp.zeros((cond_rows - 1 - bs, d), F32)], axis=0)
    mod = _modulation(cond, w_mod, b_mod)

    w_in_l = _layout_w_in(w_in)
    w_out_b = w_out.astype(BF16)
    w_ffn_in_b = w_ffn_in.astype(BF16)
    w_ffn_out_b = w_ffn_out.astype(BF16)
    prow, pcol = _delta_params(delta_a_log, delta_dt_bias)
    bnorm = jnp.tile(delta_norm, (1, B_HEADS)).reshape(depth, 1, B_HEADS * B_DV)
    cnorm = jnp.tile(hgrn_norm, (1, C_HEADS)).reshape(depth, 1, C_HEADS * C_DV)
    lb = hgrn_lb.reshape(2, depth, C_HEADS // 2, LANES).transpose(2, 0, 1, 3)
    rope = _rope_tables(ls)
    dconsts = _delta_consts()
    head_mask = dconsts[0][0, M_HEADS]
    hconsts = _hgrn_consts()
    past = cache_attn_k.shape[2]
    ck = cache_attn_k.reshape(bs, depth, past, A_HEADS * 2 * A_QK)
    cv = cache_attn_v.reshape(bs, depth, past, A_HEADS * A_V)
    sh0 = jnp.swapaxes(state_hgrn, -1, -2)
    tails = (bnorm, cnorm, w_out_b, ln_g, ln_b, w_ffn_in_b, w_ffn_out_b, alpha)

    xp = x_prompt.reshape(bp * lp, d)
    xs = x_sample.reshape(bs * ls, d)
    k_all = jnp.zeros((bp, depth, lp * A_HEADS, 2 * A_QK), F32)
    v_all = jnp.zeros((bp, depth, lp * A_HEADS, A_V), F32)
    sd_all = jnp.zeros((bp, depth, 2, B_HEADS, B_DK, B_DV), F32)
    sh_all = jnp.zeros((bp, depth, 2, C_HEADS, C_DK, C_DV), F32)
    for l in range(depth):
        mod_p = mod[l, 0:1][:, None, :]
        mod_s = mod[l, 1:1 + bs][:, None, :]

        zq, k_all, v_all, zb, zs, zc, zsr = _in_proj(xp, mod_p, w_in_l, lp, kv_out=(k_all, v_all), layer=l)
        oa = _attention(zq, lp, l, diff_lambda, diff_norm, kv_all=(k_all, v_all))
        prep = _delta_prep(zb, zs, zsr, lp, l, conv_w, prow, pcol, dconsts)
        obf, obb, sd_all = _delta_scan(prep, bp, lp, l, head_mask, s_all=sd_all)
        ocf, ocb, sh_all = _hgrn_scan(_hgrn_prep(zc, l, lb, hconsts), bp, lp, l, s_all=sh_all)
        xp = _tail(xp, oa, obf, obb, zb, ocf, ocb, zc, mod_p, l, *tails)

        za, zb, zs, zc, zsr = _in_proj(xs, mod_s, w_in_l, ls, rope_tabs=rope, layer=l)
        oa = _attention(za, ls, l, diff_lambda, diff_norm, ctx_kv=(ck, cv))
        prep = _delta_prep(zb, zs, zsr, ls, l, conv_w, prow, pcol, dconsts)
        obf, obb = _delta_scan(prep, bs, ls, l, head_mask, s0=state_delta)
        ocf, ocb = _hgrn_scan(_hgrn_prep(zc, l, lb, hconsts), bs, ls, l, s0t=sh0)
        xs = _tail(xs, oa, obf, obb, zb, ocf, ocb, zc, mod_s, l, *tails)

    return (xp.reshape(bp, lp, d), xs.reshape(bs, ls, d),
            k_all.reshape(bp, depth, lp, A_HEADS, 2 * A_QK), v_all.reshape(bp, depth, lp, A_HEADS, A_V),
            sd_all, sh_all)
```

```python
import functools
import math

import numpy as np
import jax
import jax.numpy as jnp
from jax import lax
from jax.experimental import pallas as pl
from jax.experimental.pallas import tpu as pltpu

F32 = jnp.float32
BF16 = jnp.bfloat16

A_HEADS = 4
A_QK = 64
A_V = 2 * A_QK
B_HEADS = 4
B_DK = 64
B_DV = 64
C_HEADS = 4
C_DK = 64
C_DV = 64
GRID_W = 64
ROPE_BASE = 10000.0
LN_EPS = 1e-5
RMS_EPS = 1e-6
L2_EPS = 1e-6

LANES = 128
TOK = 128
DELTA_CHUNK = 64
VMEM_LIMIT = 60 * 1024 * 1024


def _cparams(*sem):
    return pltpu.CompilerParams(dimension_semantics=sem, vmem_limit_bytes=VMEM_LIMIT)


def _dot(a, b):
    return jnp.dot(a, b, preferred_element_type=F32)


def _dot_nt(a, b):
    return lax.dot_general(a, b, (((1,), (1,)), ((), ())), preferred_element_type=F32)


def _bdot(a, b):
    return _dot(a.astype(BF16), b.astype(BF16))


def _split3(x):
    hi = x.astype(BF16)
    r = x - hi.astype(F32)
    mid = r.astype(BF16)
    lo = (r - mid.astype(F32)).astype(BF16)
    return hi, mid, lo


def _dot01(w01_rep, x, terms=3):
    pieces = _split3(x)[:terms]
    return _dot(w01_rep, jnp.concatenate(pieces, axis=0))


def _dot01_nt(x, w01):
    hi, mid, lo = _split3(x)
    return (_dot_nt(lo, w01) + _dot_nt(mid, w01)) + _dot_nt(hi, w01)


def _sigmoid(x):
    return 1.0 / (1.0 + jnp.exp(-x))


def _silu(x):
    return x * _sigmoid(x)


def _softplus(x):
    return jnp.maximum(x, 0.0) + jnp.log(1.0 + jnp.exp(-jnp.abs(x)))


def _lane_iota(shape):
    return lax.broadcasted_iota(jnp.int32, shape, len(shape) - 1)


def _row_iota(shape):
    return lax.broadcasted_iota(jnp.int32, shape, len(shape) - 2)


def _layer_norm(x, g, b):
    mu = jnp.mean(x, axis=-1, keepdims=True)
    xc = x - mu
    var = jnp.mean(xc * xc, axis=-1, keepdims=True)
    return xc * lax.rsqrt(var + LN_EPS) * g + b


def _mod_kernel(c_ref, w_ref, b_ref, o_ref):
    cs = _silu(c_ref[...])
    o_ref[0] = _bdot(cs, w_ref[0]) + b_ref[0]


def _modulation(cond, w_mod, b_mod):
    depth, d, n = w_mod.shape
    rows = cond.shape[0]
    tn = 1536 if n % 1536 == 0 else n
    return pl.pallas_call(
        _mod_kernel,
        grid=(depth, n // tn),
        in_specs=[
            pl.BlockSpec((rows, d), lambda l, j: (0, 0)),
            pl.BlockSpec((1, d, tn), lambda l, j: (l, 0, j)),
            pl.BlockSpec((1, 1, tn), lambda l, j: (l, 0, j)),
        ],
        out_specs=pl.BlockSpec((1, rows, tn), lambda l, j: (l, 0, j)),
        out_shape=jax.ShapeDtypeStruct((depth, rows, n), F32),
        compiler_params=_cparams("parallel", "parallel"),
        name="modulation",
    )(cond, w_mod, b_mod.reshape(depth, 1, n))


N_A = 3 * A_HEADS * A_V
N_B = 4 * B_HEADS * B_DK
N_S = 2 * LANES
N_C = 5 * C_HEADS * C_DK
N_Z = N_A + N_B + N_S + N_C


N_QK = A_HEADS * 2 * A_QK
Q_SCALE = (A_QK ** -0.5) * math.log2(math.e)


def _in_proj_kernel(x_ref, mod_ref, wab_ref, ws_ref, wc_ref, *rest, d_model, rope):
    if rope:
        rc_ref, rs1_ref, rs2_ref, za_ref, zb_ref, zs_ref, zc_ref, zsr_ref, w_ref = rest
    else:
        _, _, zq_ref, ko_ref, vo_ref, zb_ref, zs_ref, zc_ref, zsr_ref, w_ref = rest

    @pl.when(pl.program_id(0) == 0)
    def _():
        w_ref[...] = wab_ref[...].astype(BF16)

    m = mod_ref[0]
    shift = m[:, 0:d_model]
    scale = m[:, d_model:2 * d_model]
    h = (x_ref[...] * (1.0 + scale) + shift).astype(BF16)
    if rope:
        wide = 2 * LANES
        rc, rs1, rs2 = (jnp.concatenate([t[...], t[...]], axis=1) for t in (rc_ref, rs1_ref, rs2_ref))
        for g in range(N_A // wide):
            blk = _dot(h, w_ref[:, g * wide:(g + 1) * wide])
            if g < 2 * N_QK // wide:
                up = pltpu.roll(blk, wide - 16, axis=1)
                dn = pltpu.roll(blk, 16, axis=1)
                blk = blk * rc + up * rs1 + dn * rs2
            if g < N_QK // wide:
                blk = blk * Q_SCALE
            za_ref[:, g * wide:(g + 1) * wide] = blk.astype(za_ref.dtype)
    else:
        zq_ref[...] = _dot(h, w_ref[:, 0:N_QK])
        kk = _dot(h, w_ref[:, N_QK:2 * N_QK])
        vv = _dot(h, w_ref[:, 2 * N_QK:N_A])
        rows = kk.shape[0]
        for hd in range(A_HEADS):
            ko_ref[0, 0, pl.ds(hd, rows, stride=A_HEADS), :] = kk[:, hd * LANES:(hd + 1) * LANES]
            vo_ref[0, 0, pl.ds(hd, rows, stride=A_HEADS), :] = vv[:, hd * LANES:(hd + 1) * LANES]
    zb_ref[...] = _dot(h, w_ref[:, N_A:N_A + N_B])
    zs = _dot(h, ws_ref[...])
    zs_ref[...] = zs
    for g in range(zs.shape[0] // TOK):
        for p in range(N_S // LANES):
            zsr_ref[p, g] = zs[g * TOK:(g + 1) * TOK, p * LANES:(p + 1) * LANES].T[0:8, :]
    zc_ref[...] = _dot(h, wc_ref[...])


def _in_proj(x, mod, w, seq_len, rope_tabs=None, kv_out=None, layer=0):
    n, d = x.shape
    rope = rope_tabs is not None
    tm = min(512, seq_len) if rope else seq_len
    tiles_per_seq = seq_len // tm
    rows_per_mod = n // mod.shape[0]
    in_specs = [
        pl.BlockSpec((tm, d), lambda i: (i, 0)),
        pl.BlockSpec((1, 1, mod.shape[2]), lambda i: ((i * tm) // rows_per_mod, 0, 0)),
        pl.BlockSpec((None, d, N_A + N_B), lambda i: (layer, 0, 0), pipeline_mode=pl.Buffered(1)),
        pl.BlockSpec((None, d, N_S), lambda i: (layer, 0, 0)),
        pl.BlockSpec((None, d, N_C), lambda i: (layer, 0, 0)),
    ]
    args = [x, mod] + list(w)
    row = lambda w_: pl.BlockSpec((tm, w_), lambda i: (i, 0))
    tail_w = [N_B, N_S, N_C]
    zsr_spec = pl.BlockSpec((N_S // LANES, tm // TOK, 8, TOK), lambda i: (0, i, 0, 0))
    zsr_shape = jax.ShapeDtypeStruct((N_S // LANES, n // TOK, 8, TOK), F32)
    if rope:
        in_specs += [pl.BlockSpec((tm, LANES), lambda i: (i % tiles_per_seq, 0))] * 3
        args += list(rope_tabs)
        out_specs = [row(N_A)] + [row(w_) for w_ in tail_w] + [zsr_spec]
        out_shape = ([jax.ShapeDtypeStruct((n, N_A), BF16)] + [jax.ShapeDtypeStruct((n, w_), F32) for w_ in tail_w]
                     + [zsr_shape])
        aliases = {}
    else:
        k_all, v_all = kv_out
        in_specs += [pl.BlockSpec(memory_space=pl.ANY)] * 2
        args += [k_all, v_all]
        kv_spec = lambda a: pl.BlockSpec((1, 1, tm * A_HEADS, a.shape[3]), lambda i: (i, layer, 0, 0))
        out_specs = [row(N_QK), kv_spec(k_all), kv_spec(v_all)] + [row(w_) for w_ in tail_w] + [zsr_spec]
        out_shape = ([jax.ShapeDtypeStruct((n, N_QK), F32), jax.ShapeDtypeStruct(k_all.shape, F32),
                      jax.ShapeDtypeStruct(v_all.shape, F32)]
                     + [jax.ShapeDtypeStruct((n, w_), F32) for w_ in tail_w] + [zsr_shape])
        aliases = {5: 1, 6: 2}
    return pl.pallas_call(
        functools.partial(_in_proj_kernel, d_model=d, rope=rope),
        grid=(n // tm,),
        in_specs=in_specs,
        out_specs=out_specs,
        out_shape=out_shape,
        input_output_aliases=aliases,
        scratch_shapes=[pltpu.VMEM((d, N_A + N_B), BF16)],
        compiler_params=_cparams("arbitrary"),
        name="in_proj_rope" if rope else "in_proj",
    )(*args)


def _rope_tables(seq_len):
    half = A_QK // 2
    nf = half // 2
    pos = jnp.arange(seq_len)
    row = (pos // GRID_W).astype(F32)
    col = (pos % GRID_W).astype(F32)
    inv_freq = ROPE_BASE ** (-jnp.arange(nf, dtype=F32) / nf)
    ang_r = row[:, None] * inv_freq
    ang_c = col[:, None] * inv_freq
    cos64 = jnp.concatenate([jnp.cos(ang_r), jnp.cos(ang_r), jnp.cos(ang_c), jnp.cos(ang_c)], -1)
    zero = jnp.zeros_like(ang_r)
    s1_64 = jnp.concatenate([-jnp.sin(ang_r), zero, -jnp.sin(ang_c), zero], -1)
    s2_64 = jnp.concatenate([zero, jnp.sin(ang_r), zero, jnp.sin(ang_c)], -1)
    rep = LANES // A_QK
    return tuple(jnp.tile(t, (1, rep)) for t in (cos64, s1_64, s2_64))


def _attn_kernel(q_ref, k_ref, v_ref, *rest, lam_init, latent, sb, tq, seq_len):
    if latent:
        ck_ref, cv_ref, dl_ref, nrm_ref, o_ref, s_scr, vx_scr = rest
        past = cv_ref.shape[2]

        @pl.when(pl.program_id(1) == 0)
        def _():
            ones = jnp.ones((seq_len, LANES), BF16)
            for h in range(A_HEADS):
                hs = slice(h * LANES, (h + 1) * LANES)
                vx_scr[0:seq_len, 2 * h * LANES:(2 * h + 1) * LANES] = v_ref[:, hs]
                vx_scr[0:seq_len, (2 * h + 1) * LANES:(2 * h + 2) * LANES] = ones
                vx_scr[seq_len:, 2 * h * LANES:(2 * h + 1) * LANES] = cv_ref[0, 0, :, hs].astype(BF16)
                vx_scr[seq_len:, (2 * h + 1) * LANES:(2 * h + 2) * LANES] = ones[0:past]
    else:
        dl_ref, nrm_ref, o_ref, s_scr = rest
    dl = dl_ref[0]
    lam = (jnp.exp(jnp.sum(dl[0:1] * dl[1:2], axis=1, keepdims=True))
           - jnp.exp(jnp.sum(dl[2:3] * dl[3:4], axis=1, keepdims=True)) + lam_init)
    lo = _lane_iota((tq, LANES)) < A_QK
    units = [(sq, h) for sq in range(sb) for h in range(A_HEADS)]

    def scores(unit, slot):
        sq, h = unit
        hs = slice(h * LANES, (h + 1) * LANES)
        q = q_ref[sq * tq:(sq + 1) * tq, hs]
        if latent:
            k = k_ref[:, hs]
            zero = jnp.zeros_like(q)
        else:
            q = q * Q_SCALE
            k = k_ref[sq, 0, pl.ds(h, seq_len, stride=A_HEADS), :].astype(BF16)
            zero = 0.0
        qq = jnp.concatenate([jnp.where(lo, q, zero), jnp.where(lo, zero, q)], axis=0).astype(BF16)
        s_scr[slot, :, 0:seq_len] = _dot_nt(qq, k)
        if latent:
            s_scr[slot, :, seq_len:] = _dot_nt(qq, ck_ref[0, 0, :, hs].astype(BF16))

    def finish(unit, slot):
        sq, h = unit
        hs = slice(h * LANES, (h + 1) * LANES)
        s = s_scr[slot]
        if latent:
            e = jnp.exp2((s - jnp.max(s, axis=-1, keepdims=True)).astype(BF16))
            o2 = _dot(e, vx_scr[:, 2 * h * LANES:(2 * h + 2) * LANES])
            r = 1.0 / o2[:, LANES:LANES + 1]
            o = o2[:tq, 0:LANES] * r[:tq] - o2[tq:, 0:LANES] * (lam * r[tq:])
        else:
            e = jnp.exp2(s - jnp.max(s, axis=-1, keepdims=True))
            r = 1.0 / jnp.sum(e, axis=-1, keepdims=True)
            p = (e[:tq] * r[:tq] - e[tq:] * (lam * r[tq:])).astype(BF16)
            o = _dot(p, v_ref[sq, 0, pl.ds(h, seq_len, stride=A_HEADS), :].astype(BF16))
        o = o * lax.rsqrt(jnp.mean(o * o, axis=-1, keepdims=True) + RMS_EPS) * nrm_ref[0]
        o_ref[sq * tq:(sq + 1) * tq, hs] = o * (1.0 - lam_init)

    scores(units[0], 0)
    for i, unit in enumerate(units):
        if i + 1 < len(units):
            scores(units[i + 1], (i + 1) % 2)
        finish(unit, i % 2)


def _attention(q_arr, seq_len, layer, diff_lambda, diff_norm, kv_all=None, ctx_kv=None):
    n = q_arr.shape[0]
    n_seq = n // seq_len
    latent = ctx_kv is not None
    lam_init = 0.8 - 0.6 * math.exp(-0.3 * layer)
    if latent:
        tq = min(128, seq_len)
        sb = min(4, seq_len // tq)
        nq = seq_len // (sb * tq)
        col = lambda c: pl.BlockSpec((seq_len, N_QK), lambda s, i: (s, c))
        ck, cv = ctx_kv
        cspec = lambda a: pl.BlockSpec((1, 1) + a.shape[2:], lambda s, i: (s, layer, 0, 0))
        in_specs = [pl.BlockSpec((sb * tq, N_QK), lambda s, i: (s * nq + i, 0)), col(1), col(2), cspec(ck), cspec(cv)]
        args = [q_arr, q_arr, q_arr, ck, cv]
    else:
        sb, tq, nq = min(4, n_seq), seq_len, 1
        k_all, v_all = kv_all
        kspec = lambda a: pl.BlockSpec((sb, 1) + a.shape[2:], lambda s, i: (s, layer, 0, 0))
        in_specs = [pl.BlockSpec((sb * tq, N_QK), lambda s, i: (s, 0)), kspec(k_all), kspec(v_all)]
        args = [q_arr, k_all, v_all]
    in_specs += [
        pl.BlockSpec((1, 4, A_QK), lambda s, i: (layer, 0, 0)),
        pl.BlockSpec((1, 1, A_V), lambda s, i: (layer, 0, 0)),
    ]
    args += [diff_lambda, diff_norm.reshape(diff_norm.shape[0], 1, A_V)]
    keys = seq_len + (ctx_kv[0].shape[2] if latent else 0)
    return pl.pallas_call(
        functools.partial(_attn_kernel, lam_init=lam_init, latent=latent, sb=sb, tq=tq, seq_len=seq_len),
        grid=(n_seq // (1 if latent else sb), nq),
        in_specs=in_specs,
        out_specs=pl.BlockSpec((sb * tq, A_HEADS * A_V), lambda s, i: (s * nq + i, 0)),
        out_shape=jax.ShapeDtypeStruct((n, A_HEADS * A_V), F32),
        scratch_shapes=[pltpu.VMEM((2, 2 * tq, keys), F32)]
        + ([pltpu.VMEM((keys, 2 * A_HEADS * A_V), BF16)] if latent else []),
        compiler_params=_cparams("parallel", "arbitrary"),
        name="diff_attn_latent" if latent else "diff_attn",
    )(*args)


def _head_sum(x):
    lane = _lane_iota(x.shape)
    lo = lane < 64
    s0 = jnp.sum(jnp.where(lo, x, 0.0), axis=-1, keepdims=True)
    s1 = jnp.sum(jnp.where(lo, 0.0, x), axis=-1, keepdims=True)
    return jnp.where(lo, s0, s1)


def _pair_blockdiag(a, b):
    z = jnp.zeros_like(a)
    return jnp.concatenate([jnp.concatenate([a, z], axis=1), jnp.concatenate([z, b], axis=1)], axis=0)


def _gated_head_norm(o, norm_row, gate):
    parts = []
    for hp in range(o.shape[1] // LANES):
        blk = o[:, hp * LANES:(hp + 1) * LANES]
        parts.append(blk * lax.rsqrt(_head_sum(blk * blk) * (1.0 / 64.0) + RMS_EPS))
    return jnp.concatenate(parts, axis=1) * norm_row * _silu(gate)


DELTA_MERGES = (4, 8, 16, 32, 64)
M_CAUSAL, M_STRICT, M_PAIR, M_MERGE0 = 0, 1, 2, 3
M_EYE = M_MERGE0 + len(DELTA_MERGES)
M_HEADS = M_EYE + 1


def _delta_consts():
    t = np.arange(TOK)[:, None]
    s = np.arange(TOK)[None, :]

    def same(n):
        return (t // n) == (s // n)

    masks, cums = [], []
    for d in range(2):
        before = (s <= t) if d == 0 else (s >= t)
        strict = (s < t) if d == 0 else (s > t)
        causal = same(DELTA_CHUNK) & before
        st = same(DELTA_CHUNK) & strict
        rows = [causal, st, st & same(2)]
        rows += [st & same(n) & ~same(n // 2) for n in DELTA_MERGES]
        rows += [t == s, same(64)]
        masks.append(np.stack(rows))
        cums.append(np.concatenate([causal, same(DELTA_CHUNK) & ~before, same(DELTA_CHUNK)], 0))
    cums = np.stack(cums)
    return (jnp.asarray(np.stack(masks), F32), jnp.asarray(np.tile(cums, (1, 1, 3)), BF16),
            jnp.asarray(cums[:, 0:TOK], BF16))


def _delta_prep_kernel(z_ref, zp_ref, zn_ref, sc_ref, sr_ref, cw_ref, prow_ref, pcol_ref, msk_ref, cum_ref, cumr_ref,
                       u_ref, w_ref, qd_ref, qkm_ref, kdt_ref, gl_ref, *, blocks_per_seq, g_tiles):
    jloc = pl.program_id(0) % blocks_per_seq
    rb = g_tiles * TOK
    nqk = B_HEADS * B_DK
    pairs = B_HEADS // 2
    x = z_ref[...]
    cw = cw_ref[0]
    prev = jnp.where(jloc > 0, zp_ref[7:8, :], 0.0)
    nxt = jnp.where(jloc < blocks_per_seq - 1, zn_ref[0:1, :], 0.0)
    row = _row_iota(x.shape)
    dn = jnp.where(row == 0, prev, pltpu.roll(x, 1, axis=0))
    up = jnp.where(row == rb - 1, nxt, pltpu.roll(x, rb - 1, axis=0))
    y = _silu(cw[0:1] * dn + cw[1:2] * x + cw[2:3] * up)

    lo = _lane_iota((TOK, LANES)) < 64
    hi = jnp.logical_not(lo)
    units = [(g, p) for g in range(g_tiles) for p in range(pairs)]
    chains = [(g, p, d, hh) for (g, p) in units for d in range(2) for hh in range(2)]

    q, k, v, kk, qk, gcol, grow = {}, {}, {}, {}, {}, {}, {}
    for (g, p) in units:
        rs = slice(g * TOK, (g + 1) * TOK)
        qs = y[rs, p * LANES:(p + 1) * LANES]
        ks = y[rs, nqk + p * LANES:nqk + (p + 1) * LANES]
        v[g, p] = y[rs, 2 * nqk + p * LANES:2 * nqk + (p + 1) * LANES]
        q[g, p] = qs * lax.rsqrt(_head_sum(qs * qs) + L2_EPS) * (B_DK ** -0.5)
        k[g, p] = ks * lax.rsqrt(_head_sum(ks * ks) + L2_EPS)
    for (g, p) in units:
        k16 = k[g, p].astype(BF16)
        for hh, sel in enumerate((lo, hi)):
            kk[g, p, hh] = _dot_nt(jnp.where(sel, k[g, p], 0.0).astype(BF16), k16)
            qk[g, p, hh] = _dot_nt(jnp.where(sel, q[g, p], 0.0).astype(BF16), k16)
    for (g, p) in units:
        prow = prow_ref[0, p]
        pcol = pcol_ref[0, p]
        xs = sc_ref[g * TOK:(g + 1) * TOK, p * LANES:(p + 1) * LANES]
        lane = _lane_iota(xs.shape)
        gcol[g, p] = jnp.where(lane < 4, _sigmoid(xs), -jnp.exp(prow[0:1]) * _softplus(xs + prow[1:2]))
        xr = sr_ref[p, g]
        rowi = _row_iota(xr.shape)
        grow[g, p] = jnp.where(rowi < 4, _sigmoid(xr), -jnp.exp(pcol[:, 0:1]) * _softplus(xr + pcol[:, 1:2]))
    cs, br = {}, {}
    for g in range(g_tiles):
        gc = jnp.concatenate([gcol[g, p] for p in range(pairs)], axis=1)
        gr = jnp.concatenate([grow[g, p] for p in range(pairs)], axis=0)
        for d in range(2):
            csd = _dot01(cum_ref[d], gc)
            brd = _dot01_nt(gr, cumr_ref[d])
            for p in range(pairs):
                cs[g, p, d] = csd[:, p * LANES:(p + 1) * LANES]
                br[g, p, d] = brd[p * 8:(p + 1) * 8]
    beta, bcol, dec, m, x = {}, {}, {}, {}, {}
    for ch in chains:
        g, p, d, hh = ch
        cb, cg = d * 2 + hh, 4 + d * 2 + hh
        beta[ch] = gcol[g, p][:, cb:cb + 1]
        bcol[ch] = cs[g, p, d][0:TOK, cg:cg + 1]
        dec[ch] = jnp.exp(jnp.minimum(bcol[ch] - br[g, p, d][cg:cg + 1, :], 0.0)) * msk_ref[d, M_CAUSAL]
        m[ch] = kk[g, p, hh] * beta[ch] * dec[ch] * msk_ref[d, M_STRICT]
        x[ch] = msk_ref[d, M_EYE] - m[ch] * msk_ref[d, M_PAIR]
    for lvl in range(len(DELTA_MERGES)):
        yv = {ch: _bdot(m[ch] * msk_ref[ch[2], M_MERGE0 + lvl], x[ch]) for ch in chains}
        x = {ch: x[ch] - _bdot(x[ch], yv[ch]) for ch in chains}
    for (g, p) in units:
        rs = slice(g * TOK, (g + 1) * TOK)
        for d in range(2):
            sol = None
            eq, ek, gl, qkm = [], [], [], []
            for hh, sel in enumerate((lo, hi)):
                ch = (g, p, d, hh)
                cg = 4 + d * 2 + hh
                eb = jnp.exp(bcol[ch])
                rhs = jnp.concatenate([jnp.where(sel, v[g, p] * beta[ch], 0.0),
                                       jnp.where(sel, k[g, p] * (beta[ch] * eb), 0.0)], axis=1)
                part = _bdot(x[ch], rhs)
                sol = part if sol is None else sol + part
                qkm.append((qk[g, p, hh] * dec[ch]).astype(BF16))
                eq.append(eb)
                ek.append(jnp.exp(cs[g, p, d][TOK:2 * TOK, cg:cg + 1]))
                gl.append(jnp.exp(cs[g, p, d][2 * TOK:3 * TOK, cg:cg + 1]))
            u_ref[d, p, rs, :] = sol[:, 0:LANES]
            w_ref[d, p, rs, :] = sol[:, LANES:2 * LANES].astype(BF16)
            qd_ref[d, p, rs, :] = (q[g, p] * jnp.where(lo, eq[0], eq[1])).astype(BF16)
            qkm_ref[d, p, rs, :] = jnp.concatenate(qkm, axis=1)
            kd = k[g, p] * jnp.where(lo, ek[0], ek[1])
            kdt_ref[d, p, rs, :] = kd.T.astype(BF16)
            glf = jnp.where(lo, gl[0], gl[1])
            gl_ref[d, p, g * 16:(g + 1) * 16, :] = jnp.concatenate([glf[0:8], glf[64:72]], axis=0)


def _delta_prep(zb, zs, zs_rows, seq_len, layer, conv_w, prow, pcol, consts):
    n = zb.shape[0]
    pairs = B_HEADS // 2
    g_tiles = 2
    rb = g_tiles * TOK
    blocks_per_seq = seq_len // rb
    n_blocks = n // rb
    nconv = conv_w.shape[2]
    msk, cum, cum_rows = consts
    last8 = n // 8 - 1
    out_w = [LANES, LANES, LANES, 2 * LANES, LANES]
    out_t = [F32, BF16, BF16, BF16, BF16]
    return pl.pallas_call(
        functools.partial(_delta_prep_kernel, blocks_per_seq=blocks_per_seq, g_tiles=g_tiles),
        grid=(n_blocks,),
        in_specs=[
            pl.BlockSpec((rb, nconv), lambda i: (i, 0)),
            pl.BlockSpec((8, nconv), lambda i: (jnp.maximum(i * (rb // 8) - 1, 0), 0)),
            pl.BlockSpec((8, nconv), lambda i: (jnp.minimum((i + 1) * (rb // 8), last8), 0)),
            pl.BlockSpec((rb, pairs * LANES), lambda i: (i, 0)),
            pl.BlockSpec((pairs, g_tiles, 8, TOK), lambda i: (0, i, 0, 0)),
            pl.BlockSpec((1, 3, nconv), lambda i: (layer, 0, 0)),
            pl.BlockSpec((1, pairs, 8, LANES), lambda i: (layer, 0, 0, 0)),
            pl.BlockSpec((1, pairs, 8, LANES), lambda i: (layer, 0, 0, 0)),
            pl.BlockSpec(msk.shape, lambda i: (0, 0, 0, 0)),
            pl.BlockSpec(cum.shape, lambda i: (0, 0, 0)),
            pl.BlockSpec(cum_rows.shape, lambda i: (0, 0, 0)),
        ],
        out_specs=[pl.BlockSpec((2, pairs, rb, w_), lambda i: (0, 0, i, 0)) for w_ in out_w]
        + [pl.BlockSpec((2, pairs, g_tiles * 16, LANES), lambda i: (0, 0, i, 0))],
        out_shape=[jax.ShapeDtypeStruct((2, pairs, n, w_), t_) for w_, t_ in zip(out_w, out_t)]
        + [jax.ShapeDtypeStruct((2, pairs, n // TOK * 16, LANES), F32)],
        compiler_params=_cparams("parallel"),
        name="delta_prep",
    )(zb, zb, zb, zs, zs_rows, conv_w, prow, pcol, msk, cum, cum_rows)


def _delta_scan_kernel(*refs, sb, tb, has_s0):
    fwd, bwd = refs[0:6], refs[6:12]
    hm_ref = refs[12]
    if has_s0:
        s0_ref, of_ref, ob_ref, s_s = refs[13:]
    else:
        _, of_ref, ob_ref, sout_ref, s_s = refs[13:]
    t = pl.program_id(1)
    n_t = pl.num_programs(1)
    n_g = tb // TOK
    pairs = B_HEADS // 2
    chains = [(d, sq, p) for d in range(2) for sq in range(sb) for p in range(pairs)]

    @pl.when(t == 0)
    def _():
        for (d, sq, p) in chains:
            if has_s0:
                s_s[d, sq, p] = _pair_blockdiag(s0_ref[sq, 0, d, 2 * p], s0_ref[sq, 0, d, 2 * p + 1])
            else:
                s_s[d, sq, p] = jnp.zeros((LANES, LANES), F32)

    lane64 = _lane_iota((DELTA_CHUNK, LANES)) < 64
    zpad = jnp.zeros((DELTA_CHUNK, LANES), BF16)

    def body(gi, carry):
        for step in range(2):
            tmp = {}
            for ch in chains:
                d, sq, p = ch
                u_r, w_r = (fwd, bwd)[d][0:2]
                g = gi if d == 0 else n_g - 1 - gi
                c = step if d == 0 else 1 - step
                rc = pl.multiple_of(g * TOK + c * DELTA_CHUNK, DELTA_CHUNK)
                s = s_s[d, sq, p]
                s16 = s.astype(BF16)
                vn = u_r[0, p, sq, pl.ds(rc, DELTA_CHUNK), :] - _dot(w_r[0, p, sq, pl.ds(rc, DELTA_CHUNK), :], s16)
                tmp[ch] = (s, s16, vn, g, c, rc)
            for ch in chains:
                d, sq, p = ch
                _, _, qd_r, qkm_r, kdt_r, gl_r = (fwd, bwd)[d]
                o_r = (of_ref, ob_ref)[d]
                s, s16, vn, g, c, rc = tmp[ch]
                r0 = pl.multiple_of(g * TOK, TOK)
                v0 = jnp.where(lane64, vn, 0.0).astype(BF16)
                v1 = jnp.where(lane64, 0.0, vn).astype(BF16)
                vnb = vn.astype(BF16)
                if c == 0:
                    vext = jnp.concatenate([vnb, zpad], axis=0)
                    v2 = jnp.concatenate([v0, zpad, v1, zpad], axis=0)
                else:
                    vext = jnp.concatenate([zpad, vnb], axis=0)
                    v2 = jnp.concatenate([zpad, v0, zpad, v1], axis=0)
                o = (_dot(qd_r[0, p, sq, pl.ds(rc, DELTA_CHUNK), :], s16)
                     + _dot(qkm_r[0, p, sq, pl.ds(rc, DELTA_CHUNK), :], v2))
                o_r[sq, pl.ds(rc, DELTA_CHUNK), p * LANES:(p + 1) * LANES] = o
                glr = gl_r[0, p, sq, pl.ds(pl.multiple_of(g * 16 + c * 8, 8), 8), :][0:1]
                s_s[d, sq, p] = s * glr + _dot(kdt_r[0, p, sq, pl.ds(r0, TOK), :], vext) * hm_ref[...]
        return carry

    lax.fori_loop(0, n_g, body, 0)

    if not has_s0:
        @pl.when(t == n_t - 1)
        def _():
            for (d, sq, p) in chains:
                s = s_s[d, sq, p]
                sout_ref[sq, 0, d, 2 * p] = s[0:64, 0:64]
                sout_ref[sq, 0, d, 2 * p + 1] = s[64:128, 64:128]


def _delta_scan(prep, n_seq, seq_len, layer, head_mask, s0=None, s_all=None):
    pairs = B_HEADS // 2
    has_s0 = s0 is not None
    sb = n_seq if n_seq <= 2 else 8
    tb = min(seq_len, 512)
    n_t = seq_len // tb
    arrs = [a.reshape(2, pairs, n_seq, a.shape[2] // n_seq, a.shape[3]) for a in prep]

    def specs(d):
        tmap = (lambda s, t: (d, 0, s, t, 0)) if d == 0 else (lambda s, t: (d, 0, s, n_t - 1 - t, 0))
        return [pl.BlockSpec((1, pairs, sb, a.shape[3] // n_t, a.shape[4]), tmap) for a in arrs]

    in_specs = specs(0) + specs(1) + [pl.BlockSpec((TOK, LANES), lambda s, t: (0, 0))]
    args = arrs + arrs + [head_mask]
    st_spec = pl.BlockSpec((sb, 1, 2, B_HEADS, B_DK, B_DV), lambda s, t: (s, layer, 0, 0, 0, 0))
    width = B_HEADS * B_DV
    out_specs = [
        pl.BlockSpec((sb, tb, width), lambda s, t: (s, t, 0)),
        pl.BlockSpec((sb, tb, width), lambda s, t: (s, n_t - 1 - t, 0)),
    ]
    out_shape = [jax.ShapeDtypeStruct((n_seq, seq_len, width), F32)] * 2
    if has_s0:
        in_specs.append(st_spec)
        args.append(s0)
        aliases = {}
    else:
        in_specs.append(pl.BlockSpec(memory_space=pl.ANY))
        args.append(s_all)
        out_specs.append(st_spec)
        out_shape.append(jax.ShapeDtypeStruct(s_all.shape, F32))
        aliases = {len(args) - 1: 2}
    outs = pl.pallas_call(
        functools.partial(_delta_scan_kernel, sb=sb, tb=tb, has_s0=has_s0),
        grid=(n_seq // sb, n_t),
        in_specs=in_specs,
        out_specs=out_specs,
        out_shape=out_shape,
        input_output_aliases=aliases,
        scratch_shapes=[pltpu.VMEM((2, sb, pairs, LANES, LANES), F32)],
        compiler_params=_cparams("parallel", "arbitrary"),
        name="delta_scan_s0" if has_s0 else "delta_scan",
    )(*args)
    n = n_seq * seq_len
    return (outs[0].reshape(n, width), outs[1].reshape(n, width)) + tuple(outs[2:])


HGRN_LEVELS = 7
HGRN_WROWS = (HGRN_LEVELS + 2) * TOK + 8
HGRN_SPAN_TERMS = 2


def _hgrn_consts():
    t = np.arange(TOK)
    masks = [t[:, None] == t[None, :]]
    for lvl in range(1, HGRN_LEVELS + 1):
        n = 1 << lvl
        masks.append((t[:, None] // n) == (t[None, :] // n))
    ws = []
    for d in range(2):
        tau = t if d == 0 else TOK - 1 - t
        tt, ti = tau[:, None], tau[None, :]
        blocks = []
        for lvl in range(1, HGRN_LEVELS + 1):
            n = 1 << lvl
            piv = (tau - tau % n + n // 2 - 1)[:, None]
            upper = ((tau % n) >= n // 2)[:, None]
            blocks.append(np.where(upper, (ti > piv) & (ti <= tt), (ti > tt) & (ti <= piv)))
        blocks.append(ti <= tt)
        blocks.append(ti > tt)
        blocks.append(np.ones((8, TOK), bool))
        ws.append(np.concatenate(blocks, 0))
    return jnp.asarray(np.stack(masks), F32), jnp.asarray(np.tile(np.stack(ws), (1, 1, HGRN_SPAN_TERMS)), BF16)


def _hgrn_prep_kernel(z_ref, lb_ref, msk_ref, w_ref, intra_ref, qd_ref, kv_ref, gl_ref, *, layer, g_tiles):
    pairs = C_HEADS // 2
    nk = C_HEADS * C_DK
    lo = _lane_iota((TOK, LANES)) < 64
    row = _row_iota((TOK, LANES))
    units = [(g, p) for g in range(g_tiles) for p in range(pairs)]
    chains = [(g, p, d) for (g, p) in units for d in range(2)]

    lbs = {}
    for p in range(pairs):
        for d in range(2):
            x = lb_ref[p, d]
            e = jnp.exp(x - jnp.max(x, axis=0, keepdims=True))
            sm = e / jnp.sum(e, axis=0, keepdims=True)
            if layer > 0:
                lbs[p, d] = jnp.sum(sm[1:layer + 1], axis=0, keepdims=True)
            else:
                lbs[p, d] = jnp.zeros((1, LANES), F32)

    def stack_heads(x):
        return jnp.concatenate([jnp.where(lo, x, 0.0), jnp.where(lo, 0.0, x)], axis=0).astype(BF16)

    def both_heads(mask):
        return jnp.concatenate([mask, mask], axis=1)

    q, vt, key, ex, acc = {}, {}, {}, {}, {}
    for (g, p) in units:
        rs = slice(g * TOK, (g + 1) * TOK)
        q[g, p] = _silu(z_ref[rs, p * LANES:(p + 1) * LANES])
        vt[g, p] = z_ref[rs, 3 * nk + p * LANES:3 * nk + (p + 1) * LANES].T
    for g in range(g_tiles):
        for d in range(2):
            lg = []
            for p in range(pairs):
                f = z_ref[g * TOK:(g + 1) * TOK, (1 + d) * nk + p * LANES:(1 + d) * nk + (p + 1) * LANES]
                forget = lbs[p, d] + (1.0 - lbs[p, d]) * _sigmoid(f)
                key[g, p, d] = 1.0 - forget
                lg.append(jnp.log(forget))
            spans = _dot01(w_ref[d], jnp.concatenate(lg, axis=1), terms=HGRN_SPAN_TERMS)
            for p in range(pairs):
                ex[g, p, d] = spans[:, p * LANES:(p + 1) * LANES]
    for ch in chains:
        acc[ch] = _dot_nt(key[ch].astype(BF16), stack_heads(q[ch[0], ch[1]])) * both_heads(msk_ref[0])
    for lvl in range(1, HGRN_LEVELS + 1):
        for ch in chains:
            g, p, d = ch
            tau = row if d == 0 else TOK - 1 - row
            e = jnp.exp(ex[ch][(lvl - 1) * TOK:lvl * TOK])
            up = (tau & (1 << (lvl - 1))) != 0
            qt = jnp.where(up, q[g, p] * e, 0.0)
            kt = jnp.where(up, 0.0, key[ch] * e)
            a = _dot_nt(kt.astype(BF16), stack_heads(qt))
            if lvl < HGRN_LEVELS:
                a = a * both_heads(msk_ref[lvl])
            acc[ch] = acc[ch] + a
    base = HGRN_LEVELS * TOK
    for ch in chains:
        g, p, d = ch
        rs = slice(g * TOK, (g + 1) * TOK)
        vv = vt[g, p]
        intra_t = (_dot(jnp.where(row < 64, vv, 0.0).astype(BF16), acc[ch][:, 0:LANES].astype(BF16))
                   + _dot(jnp.where(row < 64, 0.0, vv).astype(BF16), acc[ch][:, LANES:2 * LANES].astype(BF16)))
        intra_ref[d, p, rs, :] = intra_t.T
        qd_ref[d, p, rs, :] = (q[g, p] * jnp.exp(ex[ch][base:base + TOK])).astype(BF16)
        kd = (key[ch] * jnp.exp(ex[ch][base + TOK:base + 2 * TOK])).astype(BF16)
        kv_ref[d, p, rs, :] = _dot(vv.astype(BF16), kd) * msk_ref[6]
        gl_ref[d, p, g * 8:(g + 1) * 8, :] = jnp.exp(ex[ch][base + 2 * TOK:base + 2 * TOK + 8])


def _hgrn_prep(zc, layer, lb, consts):
    n = zc.shape[0]
    pairs = C_HEADS // 2
    g_tiles = 2
    rb = g_tiles * TOK
    msk, wst = consts
    n_in = 4 * C_HEADS * C_DK
    return pl.pallas_call(
        functools.partial(_hgrn_prep_kernel, layer=layer, g_tiles=g_tiles),
        grid=(n // rb,),
        in_specs=[
            pl.BlockSpec((rb, n_in), lambda i: (i, 0)),
            pl.BlockSpec(lb.shape, lambda i: (0, 0, 0, 0)),
            pl.BlockSpec(msk.shape, lambda i: (0, 0, 0)),
            pl.BlockSpec(wst.shape, lambda i: (0, 0, 0)),
        ],
        out_specs=[pl.BlockSpec((2, pairs, rb, LANES), lambda i: (0, 0, i, 0))] * 3
        + [pl.BlockSpec((2, pairs, g_tiles * 8, LANES), lambda i: (0, 0, i, 0))],
        out_shape=[jax.ShapeDtypeStruct((2, pairs, n, LANES), t_) for t_ in (F32, BF16, F32)]
        + [jax.ShapeDtypeStruct((2, pairs, n // TOK * 8, LANES), F32)],
        compiler_params=_cparams("parallel"),
        name="hgrn_prep",
    )(zc, lb, msk, wst)


def _hgrn_scan_kernel(*refs, sb, tb, has_s0):
    fwd, bwd = refs[0:4], refs[4:8]
    if has_s0:
        s0_ref, of_ref, ob_ref, s_s = refs[8:]
    else:
        _, of_ref, ob_ref, sout_ref, s_s = refs[8:]
    t = pl.program_id(1)
    n_t = pl.num_programs(1)
    n_g = tb // TOK
    pairs = C_HEADS // 2
    chains = [(d, sq, p) for d in range(2) for sq in range(sb) for p in range(pairs)]

    @pl.when(t == 0)
    def _():
        for (d, sq, p) in chains:
            if has_s0:
                s_s[d, sq, p] = _pair_blockdiag(s0_ref[sq, 0, d, 2 * p], s0_ref[sq, 0, d, 2 * p + 1])
            else:
                s_s[d, sq, p] = jnp.zeros((LANES, LANES), F32)

    def body(gi, carry):
        for (d, sq, p) in chains:
            intra_r, qd_r, kv_r, gl_r = (fwd, bwd)[d]
            o_r = (of_ref, ob_ref)[d]
            g = gi if d == 0 else n_g - 1 - gi
            r0 = pl.multiple_of(g * TOK, TOK)
            st = s_s[d, sq, p]
            o = intra_r[0, p, sq, pl.ds(r0, TOK), :] + _dot_nt(qd_r[0, p, sq, pl.ds(r0, TOK), :], st.astype(BF16))
            o_r[sq, pl.ds(r0, TOK), p * LANES:(p + 1) * LANES] = o
            glr = gl_r[0, p, sq, pl.ds(pl.multiple_of(g * 8, 8), 8), :][0:1]
            s_s[d, sq, p] = st * glr + kv_r[0, p, sq, pl.ds(r0, TOK), :]
        return carry

    lax.fori_loop(0, n_g, body, 0)

    if not has_s0:
        @pl.when(t == n_t - 1)
        def _():
            for (d, sq, p) in chains:
                s = s_s[d, sq, p].T
                sout_ref[sq, 0, d, 2 * p] = s[0:64, 0:64]
                sout_ref[sq, 0, d, 2 * p + 1] = s[64:128, 64:128]


def _hgrn_scan(prep, n_seq, seq_len, layer, s0t=None, s_all=None):
    pairs = C_HEADS // 2
    has_s0 = s0t is not None
    sb = n_seq if n_seq <= 2 else 8
    tb = min(seq_len, 512)
    n_t = seq_len // tb
    arrs = [a.reshape(2, pairs, n_seq, a.shape[2] // n_seq, a.shape[3]) for a in prep]

    def specs(d):
        tmap = (lambda s, t: (d, 0, s, t, 0)) if d == 0 else (lambda s, t: (d, 0, s, n_t - 1 - t, 0))
        return [pl.BlockSpec((1, pairs, sb, a.shape[3] // n_t, a.shape[4]), tmap) for a in arrs]

    in_specs = specs(0) + specs(1)
    args = arrs + arrs
    st_spec = pl.BlockSpec((sb, 1, 2, C_HEADS, C_DK, C_DV), lambda s, t: (s, layer, 0, 0, 0, 0))
    width = C_HEADS * C_DV
    out_specs = [
        pl.BlockSpec((sb, tb, width), lambda s, t: (s, t, 0)),
        pl.BlockSpec((sb, tb, width), lambda s, t: (s, n_t - 1 - t, 0)),
    ]
    out_shape = [jax.ShapeDtypeStruct((n_seq, seq_len, width), F32)] * 2
    if has_s0:
        in_specs.append(st_spec)
        args.append(s0t)
        aliases = {}
    else:
        in_specs.append(pl.BlockSpec(memory_space=pl.ANY))
        args.append(s_all)
        out_specs.append(st_spec)
        out_shape.append(jax.ShapeDtypeStruct(s_all.shape, F32))
        aliases = {len(args) - 1: 2}
    outs = pl.pallas_call(
        functools.partial(_hgrn_scan_kernel, sb=sb, tb=tb, has_s0=has_s0),
        grid=(n_seq // sb, n_t),
        in_specs=in_specs,
        out_specs=out_specs,
        out_shape=out_shape,
        input_output_aliases=aliases,
        scratch_shapes=[pltpu.VMEM((2, sb, pairs, LANES, LANES), F32)],
        compiler_params=_cparams("parallel", "arbitrary"),
        name="hgrn_scan_s0" if has_s0 else "hgrn_scan",
    )(*args)
    n = n_seq * seq_len
    return (outs[0].reshape(n, width), outs[1].reshape(n, width)) + tuple(outs[2:])


FFN_SPLIT = 2


def _tail_kernel(x_ref, oa_ref, obf_ref, obb_ref, bg_ref, ocf_ref, ocb_ref, cg_ref, mod_ref, bn_ref, cn_ref,
                 wo_ref, lng_ref, lnb_ref, wi_ref, wd_ref, o_ref, *, d_model, d_ff, alpha):
    m = mod_ref[0]
    na, nb = oa_ref.shape[1], obf_ref.shape[1]
    ob = _gated_head_norm(obf_ref[...] + obb_ref[...], bn_ref[0], bg_ref[...])
    oc = _gated_head_norm(ocf_ref[...] + ocb_ref[...], cn_ref[0], cg_ref[...])
    y = _bdot(oa_ref[...], wo_ref[0:na]) + _bdot(ob, wo_ref[na:na + nb]) + _bdot(oc, wo_ref[na + nb:])
    x1 = _layer_norm(alpha * x_ref[...] + m[:, 2 * d_model:3 * d_model] * y, lng_ref[0, 0:1], lnb_ref[0, 0:1])
    h = (x1 * (1.0 + m[:, 4 * d_model:5 * d_model]) + m[:, 3 * d_model:4 * d_model]).astype(BF16)
    ck = d_ff // FFN_SPLIT
    acc = None
    for c0 in range(0, d_ff, ck):
        gt = _dot(h, wi_ref[:, c0:c0 + ck])
        up = _dot(h, wi_ref[:, d_ff + c0:d_ff + c0 + ck])
        part = _dot((_silu(gt) * up).astype(BF16), wd_ref[c0:c0 + ck, :])
        acc = part if acc is None else acc + part
    o_ref[...] = _layer_norm(alpha * x1 + m[:, 5 * d_model:6 * d_model] * acc, lng_ref[0, 1:2], lnb_ref[0, 1:2])


def _tail(x, oa, obf, obb, zb, ocf, ocb, zc, mod, layer, bnorm, cnorm, w_out, ln_g, ln_b, w_ffn_in, w_ffn_out,
          alpha):
    n, d = x.shape
    tm = 512
    d_ff = w_ffn_out.shape[1]
    rows_per_mod = n // mod.shape[0]
    nb, nc = obf.shape[1], ocf.shape[1]
    row = lambda w_: pl.BlockSpec((tm, w_), lambda i: (i, 0))
    last_cols = lambda a, w_: pl.BlockSpec((tm, w_), lambda i: (i, a.shape[1] // w_ - 1))
    per_layer = lambda w_: pl.BlockSpec((1, 1, w_), lambda i: (layer, 0, 0))
    resident = lambda a: pl.BlockSpec((None,) + a.shape[1:], lambda i: (layer, 0, 0), pipeline_mode=pl.Buffered(1))
    return pl.pallas_call(
        functools.partial(_tail_kernel, d_model=d, d_ff=d_ff, alpha=alpha),
        grid=(n // tm,),
        in_specs=[
            row(d), row(oa.shape[1]),
            row(nb), row(nb), last_cols(zb, nb),
            row(nc), row(nc), last_cols(zc, nc),
            pl.BlockSpec((1, 1, mod.shape[2]), lambda i: ((i * tm) // rows_per_mod, 0, 0)),
            per_layer(nb), per_layer(nc),
            resident(w_out),
            pl.BlockSpec((1, 2, d), lambda i: (layer, 0, 0)),
            pl.BlockSpec((1, 2, d), lambda i: (layer, 0, 0)),
            resident(w_ffn_in), resident(w_ffn_out),
        ],
        out_specs=pl.BlockSpec((tm, d), lambda i: (i, 0)),
        out_shape=jax.ShapeDtypeStruct((n, d), F32),
        compiler_params=_cparams("parallel"),
        name="out_proj_ffn",
    )(x, oa, obf, obb, zb, ocf, ocb, zc, mod, bnorm, cnorm, w_out, ln_g, ln_b, w_ffn_in, w_ffn_out)


def _pair_scalar_order(p):
    return [d * B_HEADS + 2 * p + hh for d in range(2) for hh in range(2)]


def _layout_w_in(w_in):
    depth, d, _ = w_in.shape
    beta0 = N_A + N_B
    n_sc = 4 * B_HEADS
    scal = w_in[:, :, beta0:beta0 + n_sc].astype(BF16)
    parts = []
    for p in range(B_HEADS // 2):
        order = _pair_scalar_order(p)
        cols = order + [2 * B_HEADS + i for i in order]
        parts.append(scal[:, :, np.asarray(cols)])
        parts.append(jnp.zeros((depth, d, LANES - len(cols)), BF16))
    return w_in, jnp.concatenate(parts, axis=-1), w_in[:, :, beta0 + n_sc:].astype(BF16)


def _delta_params(delta_a_log, delta_dt_bias):
    depth = delta_a_log.shape[0]
    pairs = B_HEADS // 2
    al = delta_a_log.reshape(depth, 2 * B_HEADS)
    dt = delta_dt_bias.reshape(depth, 2 * B_HEADS)
    prow = jnp.zeros((depth, pairs, 8, LANES), F32)
    pcol = jnp.zeros((depth, pairs, 8, LANES), F32)
    for p in range(pairs):
        order = np.asarray(_pair_scalar_order(p))
        prow = prow.at[:, p, 0, 4:8].set(al[:, order]).at[:, p, 1, 4:8].set(dt[:, order])
        pcol = pcol.at[:, p, 4:8, 0].set(al[:, order]).at[:, p, 4:8, 1].set(dt[:, order])
    return prow, pcol


def kernel(x_prompt, x_sample, cache_attn_k, cache_attn_v, state_delta, state_hgrn, c, c_ctx, w_mod, b_mod, w_in, conv_w, delta_a_log, delta_dt_bias, delta_norm, hgrn_lb, hgrn_norm, diff_lambda, diff_norm, w_out, ln_g, ln_b, w_ffn_in, w_ffn_out):
    depth, d, _ = w_in.shape
    bp, lp, _ = x_prompt.shape
    bs, ls, _ = x_sample.shape
    alpha = (2 * depth) ** 0.25

    cond_rows = 8 * ((1 + bs + 7) // 8)
    cond = jnp.concatenate([c_ctx[None], c, jnp.zeros((cond_rows - 1 - bs, d), F32)], axis=0)
    mod = _modulation(cond, w_mod, b_mod)

    w_in_l = _layout_w_in(w_in)
    w_out_b = w_out.astype(BF16)
    w_ffn_in_b = w_ffn_in.astype(BF16)
    w_ffn_out_b = w_ffn_out.astype(BF16)
    prow, pcol = _delta_params(delta_a_log, delta_dt_bias)
    bnorm = jnp.tile(delta_norm, (1, B_HEADS)).reshape(depth, 1, B_HEADS * B_DV)
    cnorm = jnp.tile(hgrn_norm, (1, C_HEADS)).reshape(depth, 1, C_HEADS * C_DV)
    lb = hgrn_lb.reshape(2, depth, C_HEADS // 2, LANES).transpose(2, 0, 1, 3)
    rope = _rope_tables(ls)
    dconsts = _delta_consts()
    head_mask = dconsts[0][0, M_HEADS]
    hconsts = _hgrn_consts()
    past = cache_attn_k.shape[2]
    ck = cache_attn_k.reshape(bs, depth, past, A_HEADS * 2 * A_QK)
    cv = cache_attn_v.reshape(bs, depth, past, A_HEADS * A_V)
    sh0 = jnp.swapaxes(state_hgrn, -1, -2)
    tails = (bnorm, cnorm, w_out_b, ln_g, ln_b, w_ffn_in_b, w_ffn_out_b, alpha)

    xp = x_prompt.reshape(bp * lp, d)
    xs = x_sample.reshape(bs * ls, d)
    k_all = jnp.zeros((bp, depth, lp * A_HEADS, 2 * A_QK), F32)
    v_all = jnp.zeros((bp, depth, lp * A_HEADS, A_V), F32)
    sd_all = jnp.zeros((bp, depth, 2, B_HEADS, B_DK, B_DV), F32)
    sh_all = jnp.zeros((bp, depth, 2, C_HEADS, C_DK, C_DV), F32)
    for l in range(depth):
        mod_p = mod[l, 0:1][:, None, :]
        mod_s = mod[l, 1:1 + bs][:, None, :]

        zq, k_all, v_all, zb, zs, zc, zsr = _in_proj(xp, mod_p, w_in_l, lp, kv_out=(k_all, v_all), layer=l)
        oa = _attention(zq, lp, l, diff_lambda, diff_norm, kv_all=(k_all, v_all))
        prep = _delta_prep(zb, zs, zsr, lp, l, conv_w, prow, pcol, dconsts)
        obf, obb, sd_all = _delta_scan(prep, bp, lp, l, head_mask, s_all=sd_all)
        ocf, ocb, sh_all = _hgrn_scan(_hgrn_prep(zc, l, lb, hconsts), bp, lp, l, s_all=sh_all)
        xp = _tail(xp, oa, obf, obb, zb, ocf, ocb, zc, mod_p, l, *tails)

        za, zb, zs, zc, zsr = _in_proj(xs, mod_s, w_in_l, ls, rope_tabs=rope, layer=l)
        oa = _attention(za, ls, l, diff_lambda, diff_norm, ctx_kv=(ck, cv))
        prep = _delta_prep(zb, zs, zsr, ls, l, conv_w, prow, pcol, dconsts)
        obf, obb = _delta_scan(prep, bs, ls, l, head_mask, s0=state_delta)
        ocf, ocb = _hgrn_scan(_hgrn_prep(zc, l, lb, hconsts), bs, ls, l, s0t=sh0)
        xs = _tail(xs, oa, obf, obb, zb, ocf, ocb, zc, mod_s, l, *tails)

    return (xp.reshape(bp, lp, d), xs.reshape(bs, ls, d),
            k_all.reshape(bp, depth, lp, A_HEADS, 2 * A_QK), v_all.reshape(bp, depth, lp, A_HEADS, A_V),
            sd_all, sh_all)
```

```python
import functools
import math

import numpy as np
import jax
import jax.numpy as jnp
from jax import lax
from jax.experimental import pallas as pl
from jax.experimental.pallas import tpu as pltpu

F32 = jnp.float32
BF16 = jnp.bfloat16

A_HEADS = 4
A_QK = 64
A_V = 2 * A_QK
B_HEADS = 4
B_DK = 64
B_DV = 64
C_HEADS = 4
C_DK = 64
C_DV = 64
GRID_W = 64
ROPE_BASE = 10000.0
LN_EPS = 1e-5
RMS_EPS = 1e-6
L2_EPS = 1e-6

LANES = 128
TOK = 128
DELTA_CHUNK = 64
VMEM_LIMIT = 60 * 1024 * 1024


def _cparams(*sem):
    return pltpu.CompilerParams(dimension_semantics=sem, vmem_limit_bytes=VMEM_LIMIT)


def _dot(a, b):
    return jnp.dot(a, b, preferred_element_type=F32)


def _dot_nt(a, b):
    return lax.dot_general(a, b, (((1,), (1,)), ((), ())), preferred_element_type=F32)


def _bdot(a, b):
    return _dot(a.astype(BF16), b.astype(BF16))


def _split3(x):
    hi = x.astype(BF16)
    r = x - hi.astype(F32)
    mid = r.astype(BF16)
    lo = (r - mid.astype(F32)).astype(BF16)
    return hi, mid, lo


def _dot01(w01_rep, x, terms=3):
    pieces = _split3(x)[:terms]
    return _dot(w01_rep, jnp.concatenate(pieces, axis=0))


def _dot01_nt(x, w01):
    hi, mid, lo = _split3(x)
    return (_dot_nt(lo, w01) + _dot_nt(mid, w01)) + _dot_nt(hi, w01)


def _sigmoid(x):
    return 1.0 / (1.0 + jnp.exp(-x))


def _silu(x):
    return x * _sigmoid(x)


def _softplus(x):
    return jnp.maximum(x, 0.0) + jnp.log(1.0 + jnp.exp(-jnp.abs(x)))


def _lane_iota(shape):
    return lax.broadcasted_iota(jnp.int32, shape, len(shape) - 1)


def _row_iota(shape):
    return lax.broadcasted_iota(jnp.int32, shape, len(shape) - 2)


def _layer_norm(x, g, b):
    mu = jnp.mean(x, axis=-1, keepdims=True)
    xc = x - mu
    var = jnp.mean(xc * xc, axis=-1, keepdims=True)
    return xc * lax.rsqrt(var + LN_EPS) * g + b


def _mod_kernel(c_ref, w_ref, b_ref, o_ref):
    cs = _silu(c_ref[...])
    o_ref[0] = _bdot(cs, w_ref[0]) + b_ref[0]


def _modulation(cond, w_mod, b_mod):
    depth, d, n = w_mod.shape
    rows = cond.shape[0]
    tn = 1536 if n % 1536 == 0 else n
    return pl.pallas_call(
        _mod_kernel,
        grid=(depth, n // tn),
        in_specs=[
            pl.BlockSpec((rows, d), lambda l, j: (0, 0)),
            pl.BlockSpec((1, d, tn), lambda l, j: (l, 0, j)),
            pl.BlockSpec((1, 1, tn), lambda l, j: (l, 0, j)),
        ],
        out_specs=pl.BlockSpec((1, rows, tn), lambda l, j: (l, 0, j)),
        out_shape=jax.ShapeDtypeStruct((depth, rows, n), F32),
        compiler_params=_cparams("parallel", "parallel"),
        name="modulation",
    )(cond, w_mod, b_mod.reshape(depth, 1, n))


N_A = 3 * A_HEADS * A_V
N_B = 4 * B_HEADS * B_DK
N_S = 2 * LANES
N_C = 5 * C_HEADS * C_DK
N_Z = N_A + N_B + N_S + N_C


N_QK = A_HEADS * 2 * A_QK
Q_SCALE = (A_QK ** -0.5) * math.log2(math.e)


def _in_proj_kernel(x_ref, mod_ref, wab_ref, ws_ref, wc_ref, *rest, d_model, rope):
    if rope:
        rc_ref, rs1_ref, rs2_ref, za_ref, zb_ref, zs_ref, zc_ref, zsr_ref, w_ref = rest
    else:
        _, _, zq_ref, ko_ref, vo_ref, zb_ref, zs_ref, zc_ref, zsr_ref, w_ref = rest

    @pl.when(pl.program_id(0) == 0)
    def _():
        w_ref[...] = wab_ref[...].astype(BF16)

    m = mod_ref[0]
    shift = m[:, 0:d_model]
    scale = m[:, d_model:2 * d_model]
    h = (x_ref[...] * (1.0 + scale) + shift).astype(BF16)
    if rope:
        wide = 2 * LANES
        rc, rs1, rs2 = (jnp.concatenate([t[...], t[...]], axis=1) for t in (rc_ref, rs1_ref, rs2_ref))
        for g in range(N_A // wide):
            blk = _dot(h, w_ref[:, g * wide:(g + 1) * wide])
            if g < 2 * N_QK // wide:
                up = pltpu.roll(blk, wide - 16, axis=1)
                dn = pltpu.roll(blk, 16, axis=1)
                blk = blk * rc + up * rs1 + dn * rs2
            if g < N_QK // wide:
                blk = blk * Q_SCALE
            za_ref[:, g * wide:(g + 1) * wide] = blk.astype(za_ref.dtype)
    else:
        zq_ref[...] = _dot(h, w_ref[:, 0:N_QK])
        kk = _dot(h, w_ref[:, N_QK:2 * N_QK])
        vv = _dot(h, w_ref[:, 2 * N_QK:N_A])
        rows = kk.shape[0]
        for hd in range(A_HEADS):
            ko_ref[0, 0, pl.ds(hd, rows, stride=A_HEADS), :] = kk[:, hd * LANES:(hd + 1) * LANES]
            vo_ref[0, 0, pl.ds(hd, rows, stride=A_HEADS), :] = vv[:, hd * LANES:(hd + 1) * LANES]
    zb_ref[...] = _dot(h, w_ref[:, N_A:N_A + N_B])
    zs = _dot(h, ws_ref[...])
    zs_ref[...] = zs
    for g in range(zs.shape[0] // TOK):
        for p in range(N_S // LANES):
            zsr_ref[p, g] = zs[g * TOK:(g + 1) * TOK, p * LANES:(p + 1) * LANES].T[0:8, :]
    zc_ref[...] = _dot(h, wc_ref[...])


def _in_proj(x, mod, w, seq_len, rope_tabs=None, kv_out=None, layer=0):
    n, d = x.shape
    rope = rope_tabs is not None
    tm = min(512, seq_len) if rope else seq_len
    tiles_per_seq = seq_len // tm
    rows_per_mod = n // mod.shape[0]
    in_specs = [
        pl.BlockSpec((tm, d), lambda i: (i, 0)),
        pl.BlockSpec((1, 1, mod.shape[2]), lambda i: ((i * tm) // rows_per_mod, 0, 0)),
        pl.BlockSpec((None, d, N_A + N_B), lambda i: (layer, 0, 0), pipeline_mode=pl.Buffered(1)),
        pl.BlockSpec((None, d, N_S), lambda i: (layer, 0, 0)),
        pl.BlockSpec((None, d, N_C), lambda i: (layer, 0, 0)),
    ]
    args = [x, mod] + list(w)
    row = lambda w_: pl.BlockSpec((tm, w_), lambda i: (i, 0))
    tail_w = [N_B, N_S, N_C]
    zsr_spec = pl.BlockSpec((N_S // LANES, tm // TOK, 8, TOK), lambda i: (0, i, 0, 0))
    zsr_shape = jax.ShapeDtypeStruct((N_S // LANES, n // TOK, 8, TOK), F32)
    if rope:
        in_specs += [pl.BlockSpec((tm, LANES), lambda i: (i % tiles_per_seq, 0))] * 3
        args += list(rope_tabs)
        out_specs = [row(N_A)] + [row(w_) for w_ in tail_w] + [zsr_spec]
        out_shape = ([jax.ShapeDtypeStruct((n, N_A), BF16)] + [jax.ShapeDtypeStruct((n, w_), F32) for w_ in tail_w]
                     + [zsr_shape])
        aliases = {}
    else:
        k_all, v_all = kv_out
        in_specs += [pl.BlockSpec(memory_space=pl.ANY)] * 2
        args += [k_all, v_all]
        kv_spec = lambda a: pl.BlockSpec((1, 1, tm * A_HEADS, a.shape[3]), lambda i: (i, layer, 0, 0))
        out_specs = [row(N_QK), kv_spec(k_all), kv_spec(v_all)] + [row(w_) for w_ in tail_w] + [zsr_spec]
        out_shape = ([jax.ShapeDtypeStruct((n, N_QK), F32), jax.ShapeDtypeStruct(k_all.shape, F32),
                      jax.ShapeDtypeStruct(v_all.shape, F32)]
                     + [jax.ShapeDtypeStruct((n, w_), F32) for w_ in tail_w] + [zsr_shape])
        aliases = {5: 1, 6: 2}
    return pl.pallas_call(
        functools.partial(_in_proj_kernel, d_model=d, rope=rope),
        grid=(n // tm,),
        in_specs=in_specs,
        out_specs=out_specs,
        out_shape=out_shape,
        input_output_aliases=aliases,
        scratch_shapes=[pltpu.VMEM((d, N_A + N_B), BF16)],
        compiler_params=_cparams("arbitrary"),
        name="in_proj_rope" if rope else "in_proj",
    )(*args)


def _rope_tables(seq_len):
    half = A_QK // 2
    nf = half // 2
    pos = jnp.arange(seq_len)
    row = (pos // GRID_W).astype(F32)
    col = (pos % GRID_W).astype(F32)
    inv_freq = ROPE_BASE ** (-jnp.arange(nf, dtype=F32) / nf)
    ang_r = row[:, None] * inv_freq
    ang_c = col[:, None] * inv_freq
    cos64 = jnp.concatenate([jnp.cos(ang_r), jnp.cos(ang_r), jnp.cos(ang_c), jnp.cos(ang_c)], -1)
    zero = jnp.zeros_like(ang_r)
    s1_64 = jnp.concatenate([-jnp.sin(ang_r), zero, -jnp.sin(ang_c), zero], -1)
    s2_64 = jnp.concatenate([zero, jnp.sin(ang_r), zero, jnp.sin(ang_c)], -1)
    rep = LANES // A_QK
    return tuple(jnp.tile(t, (1, rep)) for t in (cos64, s1_64, s2_64))


def _attn_kernel(q_ref, k_ref, v_ref, *rest, lam_init, latent, sb, tq, seq_len):
    if latent:
        ck_ref, cv_ref, dl_ref, nrm_ref, o_ref, s_scr, vx_scr = rest
        past = cv_ref.shape[2]

        @pl.when(pl.program_id(1) == 0)
        def _():
            ones = jnp.ones((seq_len, LANES), BF16)
            for h in range(A_HEADS):
                hs = slice(h * LANES, (h + 1) * LANES)
                vx_scr[0:seq_len, 2 * h * LANES:(2 * h + 1) * LANES] = v_ref[:, hs]
                vx_scr[0:seq_len, (2 * h + 1) * LANES:(2 * h + 2) * LANES] = ones
                vx_scr[seq_len:, 2 * h * LANES:(2 * h + 1) * LANES] = cv_ref[0, 0, :, hs].astype(BF16)
                vx_scr[seq_len:, (2 * h + 1) * LANES:(2 * h + 2) * LANES] = ones[0:past]
    else:
        dl_ref, nrm_ref, o_ref, s_scr = rest
    dl = dl_ref[0]
    lam = (jnp.exp(jnp.sum(dl[0:1] * dl[1:2], axis=1, keepdims=True))
           - jnp.exp(jnp.sum(dl[2:3] * dl[3:4], axis=1, keepdims=True)) + lam_init)
    lo = _lane_iota((tq, LANES)) < A_QK
    units = [(sq, h) for sq in range(sb) for h in range(A_HEADS)]

    def scores(unit, slot):
        sq, h = unit
        hs = slice(h * LANES, (h + 1) * LANES)
        q = q_ref[sq * tq:(sq + 1) * tq, hs]
        if latent:
            k = k_ref[:, hs]
            zero = jnp.zeros_like(q)
        else:
            q = q * Q_SCALE
            k = k_ref[sq, 0, pl.ds(h, seq_len, stride=A_HEADS), :].astype(BF16)
            zero = 0.0
        qq = jnp.concatenate([jnp.where(lo, q, zero), jnp.where(lo, zero, q)], axis=0).astype(BF16)
        s_scr[slot, :, 0:seq_len] = _dot_nt(qq, k)
        if latent:
            s_scr[slot, :, seq_len:] = _dot_nt(qq, ck_ref[0, 0, :, hs].astype(BF16))

    def finish(unit, slot):
        sq, h = unit
        hs = slice(h * LANES, (h + 1) * LANES)
        s = s_scr[slot]
        if latent:
            e = jnp.exp2((s - jnp.max(s, axis=-1, keepdims=True)).astype(BF16))
            o2 = _dot(e, vx_scr[:, 2 * h * LANES:(2 * h + 2) * LANES])
            r = 1.0 / o2[:, LANES:LANES + 1]
            o = o2[:tq, 0:LANES] * r[:tq] - o2[tq:, 0:LANES] * (lam * r[tq:])
        else:
            e = jnp.exp2(s - jnp.max(s, axis=-1, keepdims=True))
            r = 1.0 / jnp.sum(e, axis=-1, keepdims=True)
            p = (e[:tq] * r[:tq] - e[tq:] * (lam * r[tq:])).astype(BF16)
            o = _dot(p, v_ref[sq, 0, pl.ds(h, seq_len, stride=A_HEADS), :].astype(BF16))
        o = o * lax.rsqrt(jnp.mean(o * o, axis=-1, keepdims=True) + RMS_EPS) * nrm_ref[0]
        o_ref[sq * tq:(sq + 1) * tq, hs] = o * (1.0 - lam_init)

    scores(units[0], 0)
    for i, unit in enumerate(units):
        if i + 1 < len(units):
            scores(units[i + 1], (i + 1) % 2)
        finish(unit, i % 2)


def _attention(q_arr, seq_len, layer, diff_lambda, diff_norm, kv_all=None, ctx_kv=None):
    n = q_arr.shape[0]
    n_seq = n // seq_len
    latent = ctx_kv is not None
    lam_init = 0.8 - 0.6 * math.exp(-0.3 * layer)
    if latent:
        tq = min(128, seq_len)
        sb = min(4, seq_len // tq)
        nq = seq_len // (sb * tq)
        col = lambda c: pl.BlockSpec((seq_len, N_QK), lambda s, i: (s, c))
        ck, cv = ctx_kv
        cspec = lambda a: pl.BlockSpec((1, 1) + a.shape[2:], lambda s, i: (s, layer, 0, 0))
        in_specs = [pl.BlockSpec((sb * tq, N_QK), lambda s, i: (s * nq + i, 0)), col(1), col(2), cspec(ck), cspec(cv)]
        args = [q_arr, q_arr, q_arr, ck, cv]
    else:
        sb, tq, nq = min(4, n_seq), seq_len, 1
        k_all, v_all = kv_all
        kspec = lambda a: pl.BlockSpec((sb, 1) + a.shape[2:], lambda s, i: (s, layer, 0, 0))
        in_specs = [pl.BlockSpec((sb * tq, N_QK), lambda s, i: (s, 0)), kspec(k_all), kspec(v_all)]
        args = [q_arr, k_all, v_all]
    in_specs += [
        pl.BlockSpec((1, 4, A_QK), lambda s, i: (layer, 0, 0)),
        pl.BlockSpec((1, 1, A_V), lambda s, i: (layer, 0, 0)),
    ]
    args += [diff_lambda, diff_norm.reshape(diff_norm.shape[0], 1, A_V)]
    keys = seq_len + (ctx_kv[0].shape[2] if latent else 0)
    return pl.pallas_call(
        functools.partial(_attn_kernel, lam_init=lam_init, latent=latent, sb=sb, tq=tq, seq_len=seq_len),
        grid=(n_seq // (1 if latent else sb), nq),
        in_specs=in_specs,
        out_specs=pl.BlockSpec((sb * tq, A_HEADS * A_V), lambda s, i: (s * nq + i, 0)),
        out_shape=jax.ShapeDtypeStruct((n, A_HEADS * A_V), F32),
        scratch_shapes=[pltpu.VMEM((2, 2 * tq, keys), F32)]
        + ([pltpu.VMEM((keys, 2 * A_HEADS * A_V), BF16)] if latent else []),
        compiler_params=_cparams("parallel", "arbitrary"),
        name="diff_attn_latent" if latent else "diff_attn",
    )(*args)


def _head_sum(x):
    lane = _lane_iota(x.shape)
    lo = lane < 64
    s0 = jnp.sum(jnp.where(lo, x, 0.0), axis=-1, keepdims=True)
    s1 = jnp.sum(jnp.where(lo, 0.0, x), axis=-1, keepdims=True)
    return jnp.where(lo, s0, s1)


def _pair_blockdiag(a, b):
    z = jnp.zeros_like(a)
    return jnp.concatenate([jnp.concatenate([a, z], axis=1), jnp.concatenate([z, b], axis=1)], axis=0)


def _gated_head_norm(o, norm_row, gate):
    parts = []
    for hp in range(o.shape[1] // LANES):
        blk = o[:, hp * LANES:(hp + 1) * LANES]
        parts.append(blk * lax.rsqrt(_head_sum(blk * blk) * (1.0 / 64.0) + RMS_EPS))
    return jnp.concatenate(parts, axis=1) * norm_row * _silu(gate)


DELTA_MERGES = (4, 8, 16, 32, 64)
M_CAUSAL, M_STRICT, M_PAIR, M_MERGE0 = 0, 1, 2, 3
M_EYE = M_MERGE0 + len(DELTA_MERGES)
M_HEADS = M_EYE + 1


def _delta_consts():
    t = np.arange(TOK)[:, None]
    s = np.arange(TOK)[None, :]

    def same(n):
        return (t // n) == (s // n)

    masks, cums = [], []
    for d in range(2):
        before = (s <= t) if d == 0 else (s >= t)
        strict = (s < t) if d == 0 else (s > t)
        causal = same(DELTA_CHUNK) & before
        st = same(DELTA_CHUNK) & strict
        rows = [causal, st, st & same(2)]
        rows += [st & same(n) & ~same(n // 2) for n in DELTA_MERGES]
        rows += [t == s, same(64)]
        masks.append(np.stack(rows))
        cums.append(np.concatenate([causal, same(DELTA_CHUNK) & ~before, same(DELTA_CHUNK)], 0))
    cums = np.stack(cums)
    return (jnp.asarray(np.stack(masks), F32), jnp.asarray(np.tile(cums, (1, 1, 3)), BF16),
            jnp.asarray(cums[:, 0:TOK], BF16))


def _delta_prep_kernel(z_ref, zp_ref, zn_ref, sc_ref, sr_ref, cw_ref, prow_ref, pcol_ref, msk_ref, cum_ref, cumr_ref,
                       u_ref, w_ref, qd_ref, qkm_ref, kdt_ref, gl_ref, *, blocks_per_seq, g_tiles):
    jloc = pl.program_id(0) % blocks_per_seq
    rb = g_tiles * TOK
    nqk = B_HEADS * B_DK
    pairs = B_HEADS // 2
    x = z_ref[...]
    cw = cw_ref[0]
    prev = jnp.where(jloc > 0, zp_ref[7:8, :], 0.0)
    nxt = jnp.where(jloc < blocks_per_seq - 1, zn_ref[0:1, :], 0.0)
    row = _row_iota(x.shape)
    dn = jnp.where(row == 0, prev, pltpu.roll(x, 1, axis=0))
    up = jnp.where(row == rb - 1, nxt, pltpu.roll(x, rb - 1, axis=0))
    y = _silu(cw[0:1] * dn + cw[1:2] * x + cw[2:3] * up)

    lo = _lane_iota((TOK, LANES)) < 64
    hi = jnp.logical_not(lo)
    units = [(g, p) for g in range(g_tiles) for p in range(pairs)]
    chains = [(g, p, d, hh) for (g, p) in units for d in range(2) for hh in range(2)]

    q, k, v, kk, qk, gcol, grow = {}, {}, {}, {}, {}, {}, {}
    for (g, p) in units:
        rs = slice(g * TOK, (g + 1) * TOK)
        qs = y[rs, p * LANES:(p + 1) * LANES]
        ks = y[rs, nqk + p * LANES:nqk + (p + 1) * LANES]
        v[g, p] = y[rs, 2 * nqk + p * LANES:2 * nqk + (p + 1) * LANES]
        q[g, p] = qs * lax.rsqrt(_head_sum(qs * qs) + L2_EPS) * (B_DK ** -0.5)
        k[g, p] = ks * lax.rsqrt(_head_sum(ks * ks) + L2_EPS)
    for (g, p) in units:
        k16 = k[g, p].astype(BF16)
        for hh, sel in enumerate((lo, hi)):
            kk[g, p, hh] = _dot_nt(jnp.where(sel, k[g, p], 0.0).astype(BF16), k16)
            qk[g, p, hh] = _dot_nt(jnp.where(sel, q[g, p], 0.0).astype(BF16), k16)
    for (g, p) in units:
        prow = prow_ref[0, p]
        pcol = pcol_ref[0, p]
        xs = sc_ref[g * TOK:(g + 1) * TOK, p * LANES:(p + 1) * LANES]
        lane = _lane_iota(xs.shape)
        gcol[g, p] = jnp.where(lane < 4, _sigmoid(xs), -jnp.exp(prow[0:1]) * _softplus(xs + prow[1:2]))
        xr = sr_ref[p, g]
        rowi = _row_iota(xr.shape)
        grow[g, p] = jnp.where(rowi < 4, _sigmoid(xr), -jnp.exp(pcol[:, 0:1]) * _softplus(xr + pcol[:, 1:2]))
    cs, br = {}, {}
    for g in range(g_tiles):
        gc = jnp.concatenate([gcol[g, p] for p in range(pairs)], axis=1)
        gr = jnp.concatenate([grow[g, p] for p in range(pairs)], axis=0)
        for d in range(2):
            csd = _dot01(cum_ref[d], gc)
            brd = _dot01_nt(gr, cumr_ref[d])
            for p in range(pairs):
                cs[g, p, d] = csd[:, p * LANES:(p + 1) * LANES]
                br[g, p, d] = brd[p * 8:(p + 1) * 8]
    beta, bcol, dec, m, x = {}, {}, {}, {}, {}
    for ch in chains:
        g, p, d, hh = ch
        cb, cg = d * 2 + hh, 4 + d * 2 + hh
        beta[ch] = gcol[g, p][:, cb:cb + 1]
        bcol[ch] = cs[g, p, d][0:TOK, cg:cg + 1]
        dec[ch] = jnp.exp(jnp.minimum(bcol[ch] - br[g, p, d][cg:cg + 1, :], 0.0)) * msk_ref[d, M_CAUSAL]
        m[ch] = kk[g, p, hh] * beta[ch] * dec[ch] * msk_ref[d, M_STRICT]
        x[ch] = msk_ref[d, M_EYE] - m[ch] * msk_ref[d, M_PAIR]
    for lvl in range(len(DELTA_MERGES)):
        yv = {ch: _bdot(m[ch] * msk_ref[ch[2], M_MERGE0 + lvl], x[ch]) for ch in chains}
        x = {ch: x[ch] - _bdot(x[ch], yv[ch]) for ch in chains}
    for (g, p) in units:
        rs = slice(g * TOK, (g + 1) * TOK)
        for d in range(2):
            sol = None
            eq, ek, gl, qkm = [], [], [], []
            for hh, sel in enumerate((lo, hi)):
                ch = (g, p, d, hh)
                cg = 4 + d * 2 + hh
                eb = jnp.exp(bcol[ch])
                rhs = jnp.concatenate([jnp.where(sel, v[g, p] * beta[ch], 0.0),
                                       jnp.where(sel, k[g, p] * (beta[ch] * eb), 0.0)], axis=1)
                part = _bdot(x[ch], rhs)
                sol = part if sol is None else sol + part
                qkm.append((qk[g, p, hh] * dec[ch]).astype(BF16))
                eq.append(eb)
                ek.append(jnp.exp(cs[g, p, d][TOK:2 * TOK, cg:cg + 1]))
                gl.append(jnp.exp(cs[g, p, d][2 * TOK:3 * TOK, cg:cg + 1]))
            u_ref[d, p, rs, :] = sol[:, 0:LANES]
            w_ref[d, p, rs, :] = sol[:, LANES:2 * LANES].astype(BF16)
            qd_ref[d, p, rs, :] = (q[g, p] * jnp.where(lo, eq[0], eq[1])).astype(BF16)
            qkm_ref[d, p, rs, :] = jnp.concatenate(qkm, axis=1)
            kd = k[g, p] * jnp.where(lo, ek[0], ek[1])
            kdt_ref[d, p, rs, :] = kd.T.astype(BF16)
            glf = jnp.where(lo, gl[0], gl[1])
            gl_ref[d, p, g * 16:(g + 1) * 16, :] = jnp.concatenate([glf[0:8], glf[64:72]], axis=0)


def _delta_prep(zb, zs, zs_rows, seq_len, layer, conv_w, prow, pcol, consts):
    n = zb.shape[0]
    pairs = B_HEADS // 2
    g_tiles = 2
    rb = g_tiles * TOK
    blocks_per_seq = seq_len // rb
    n_blocks = n // rb
    nconv = conv_w.shape[2]
    msk, cum, cum_rows = consts
    last8 = n // 8 - 1
    out_w = [LANES, LANES, LANES, 2 * LANES, LANES]
    out_t = [F32, BF16, BF16, BF16, BF16]
    return pl.pallas_call(
        functools.partial(_delta_prep_kernel, blocks_per_seq=blocks_per_seq, g_tiles=g_tiles),
        grid=(n_blocks,),
        in_specs=[
            pl.BlockSpec((rb, nconv), lambda i: (i, 0)),
            pl.BlockSpec((8, nconv), lambda i: (jnp.maximum(i * (rb // 8) - 1, 0), 0)),
            pl.BlockSpec((8, nconv), lambda i: (jnp.minimum((i + 1) * (rb // 8), last8), 0)),
            pl.BlockSpec((rb, pairs * LANES), lambda i: (i, 0)),
            pl.BlockSpec((pairs, g_tiles, 8, TOK), lambda i: (0, i, 0, 0)),
            pl.BlockSpec((1, 3, nconv), lambda i: (layer, 0, 0)),
            pl.BlockSpec((1, pairs, 8, LANES), lambda i: (layer, 0, 0, 0)),
            pl.BlockSpec((1, pairs, 8, LANES), lambda i: (layer, 0, 0, 0)),
            pl.BlockSpec(msk.shape, lambda i: (0, 0, 0, 0)),
            pl.BlockSpec(cum.shape, lambda i: (0, 0, 0)),
            pl.BlockSpec(cum_rows.shape, lambda i: (0, 0, 0)),
        ],
        out_specs=[pl.BlockSpec((2, pairs, rb, w_), lambda i: (0, 0, i, 0)) for w_ in out_w]
        + [pl.BlockSpec((2, pairs, g_tiles * 16, LANES), lambda i: (0, 0, i, 0))],
        out_shape=[jax.ShapeDtypeStruct((2, pairs, n, w_), t_) for w_, t_ in zip(out_w, out_t)]
        + [jax.ShapeDtypeStruct((2, pairs, n // TOK * 16, LANES), F32)],
        compiler_params=_cparams("parallel"),
        name="delta_prep",
    )(zb, zb, zb, zs, zs_rows, conv_w, prow, pcol, msk, cum, cum_rows)


def _delta_scan_kernel(*refs, sb, tb, has_s0):
    fwd, bwd = refs[0:6], refs[6:12]
    hm_ref = refs[12]
    if has_s0:
        s0_ref, of_ref, ob_ref, s_s = refs[13:]
    else:
        _, of_ref, ob_ref, sout_ref, s_s = refs[13:]
    t = pl.program_id(1)
    n_t = pl.num_programs(1)
    n_g = tb // TOK
    pairs = B_HEADS // 2
    chains = [(d, sq, p) for d in range(2) for sq in range(sb) for p in range(pairs)]

    @pl.when(t == 0)
    def _():
        for (d, sq, p) in chains:
            if has_s0:
                s_s[d, sq, p] = _pair_blockdiag(s0_ref[sq, 0, d, 2 * p], s0_ref[sq, 0, d, 2 * p + 1])
            else:
                s_s[d, sq, p] = jnp.zeros((LANES, LANES), F32)

    lane64 = _lane_iota((DELTA_CHUNK, LANES)) < 64
    zpad = jnp.zeros((DELTA_CHUNK, LANES), BF16)

    def body(gi, carry):
        for step in range(2):
            tmp = {}
            for ch in chains:
                d, sq, p = ch
                u_r, w_r = (fwd, bwd)[d][0:2]
                g = gi if d == 0 else n_g - 1 - gi
                c = step if d == 0 else 1 - step
                rc = pl.multiple_of(g * TOK + c * DELTA_CHUNK, DELTA_CHUNK)
                s = s_s[d, sq, p]
                s16 = s.astype(BF16)
                vn = u_r[0, p, sq, pl.ds(rc, DELTA_CHUNK), :] - _dot(w_r[0, p, sq, pl.ds(rc, DELTA_CHUNK), :], s16)
                tmp[ch] = (s, s16, vn, g, c, rc)
            for ch in chains:
                d, sq, p = ch
                _, _, qd_r, qkm_r, kdt_r, gl_r = (fwd, bwd)[d]
                o_r = (of_ref, ob_ref)[d]
                s, s16, vn, g, c, rc = tmp[ch]
                r0 = pl.multiple_of(g * TOK, TOK)
                v0 = jnp.where(lane64, vn, 0.0).astype(BF16)
                v1 = jnp.where(lane64, 0.0, vn).astype(BF16)
                vnb = vn.astype(BF16)
                if c == 0:
                    vext = jnp.concatenate([vnb, zpad], axis=0)
                    v2 = jnp.concatenate([v0, zpad, v1, zpad], axis=0)
                else:
                    vext = jnp.concatenate([zpad, vnb], axis=0)
                    v2 = jnp.concatenate([zpad, v0, zpad, v1], axis=0)
                o = (_dot(qd_r[0, p, sq, pl.ds(rc, DELTA_CHUNK), :], s16)
                     + _dot(qkm_r[0, p, sq, pl.ds(rc, DELTA_CHUNK), :], v2))
                o_r[sq, pl.ds(rc, DELTA_CHUNK), p * LANES:(p + 1) * LANES] = o
                glr = gl_r[0, p, sq, pl.ds(pl.multiple_of(g * 16 + c * 8, 8), 8), :][0:1]
                s_s[d, sq, p] = s * glr + _dot(kdt_r[0, p, sq, pl.ds(r0, TOK), :], vext) * hm_ref[...]
        return carry

    lax.fori_loop(0, n_g, body, 0)

    if not has_s0:
        @pl.when(t == n_t - 1)
        def _():
            for (d, sq, p) in chains:
                s = s_s[d, sq, p]
                sout_ref[sq, 0, d, 2 * p] = s[0:64, 0:64]
                sout_ref[sq, 0, d, 2 * p + 1] = s[64:128, 64:128]


def _delta_scan(prep, n_seq, seq_len, layer, head_mask, s0=None, s_all=None):
    pairs = B_HEADS // 2
    has_s0 = s0 is not None
    sb = n_seq if n_seq <= 2 else 8
    tb = min(seq_len, 512)
    n_t = seq_len // tb
    arrs = [a.reshape(2, pairs, n_seq, a.shape[2] // n_seq, a.shape[3]) for a in prep]

    def specs(d):
        tmap = (lambda s, t: (d, 0, s, t, 0)) if d == 0 else (lambda s, t: (d, 0, s, n_t - 1 - t, 0))
        return [pl.BlockSpec((1, pairs, sb, a.shape[3] // n_t, a.shape[4]), tmap) for a in arrs]

    in_specs = specs(0) + specs(1) + [pl.BlockSpec((TOK, LANES), lambda s, t: (0, 0))]
    args = arrs + arrs + [head_mask]
    st_spec = pl.BlockSpec((sb, 1, 2, B_HEADS, B_DK, B_DV), lambda s, t: (s, layer, 0, 0, 0, 0))
    width = B_HEADS * B_DV
    out_specs = [
        pl.BlockSpec((sb, tb, width), lambda s, t: (s, t, 0)),
        pl.BlockSpec((sb, tb, width), lambda s, t: (s, n_t - 1 - t, 0)),
    ]
    out_shape = [jax.ShapeDtypeStruct((n_seq, seq_len, width), F32)] * 2
    if has_s0:
        in_specs.append(st_spec)
        args.append(s0)
        aliases = {}
    else:
        in_specs.append(pl.BlockSpec(memory_space=pl.ANY))
        args.append(s_all)
        out_specs.append(st_spec)
        out_shape.append(jax.ShapeDtypeStruct(s_all.shape, F32))
        aliases = {len(args) - 1: 2}
    outs = pl.pallas_call(
        functools.partial(_delta_scan_kernel, sb=sb, tb=tb, has_s0=has_s0),
        grid=(n_seq // sb, n_t),
        in_specs=in_specs,
        out_specs=out_specs,
        out_shape=out_shape,
        input_output_aliases=aliases,
        scratch_shapes=[pltpu.VMEM((2, sb, pairs, LANES, LANES), F32)],
        compiler_params=_cparams("parallel", "arbitrary"),
        name="delta_scan_s0" if has_s0 else "delta_scan",
    )(*args)
    n = n_seq * seq_len
    return (outs[0].reshape(n, width), outs[1].reshape(n, width)) + tuple(outs[2:])


HGRN_LEVELS = 7
HGRN_WROWS = (HGRN_LEVELS + 2) * TOK + 8
HGRN_SPAN_TERMS = 2


def _hgrn_consts():
    t = np.arange(TOK)
    masks = [t[:, None] == t[None, :]]
    for lvl in range(1, HGRN_LEVELS + 1):
        n = 1 << lvl
        masks.append((t[:, None] // n) == (t[None, :] // n))
    ws = []
    for d in range(2):
        tau = t if d == 0 else TOK - 1 - t
        tt, ti = tau[:, None], tau[None, :]
        blocks = []
        for lvl in range(1, HGRN_LEVELS + 1):
            n = 1 << lvl
            piv = (tau - tau % n + n // 2 - 1)[:, None]
            upper = ((tau % n) >= n // 2)[:, None]
            blocks.append(np.where(upper, (ti > piv) & (ti <= tt), (ti > tt) & (ti <= piv)))
        blocks.append(ti <= tt)
        blocks.append(ti > tt)
        blocks.append(np.ones((8, TOK), bool))
        ws.append(np.concatenate(blocks, 0))
    return jnp.asarray(np.stack(masks), F32), jnp.asarray(np.tile(np.stack(ws), (1, 1, HGRN_SPAN_TERMS)), BF16)


def _hgrn_prep_kernel(z_ref, lb_ref, msk_ref, w_ref, intra_ref, qd_ref, kv_ref, gl_ref, *, layer, g_tiles):
    pairs = C_HEADS // 2
    nk = C_HEADS * C_DK
    lo = _lane_iota((TOK, LANES)) < 64
    row = _row_iota((TOK, LANES))
    units = [(g, p) for g in range(g_tiles) for p in range(pairs)]
    chains = [(g, p, d) for (g, p) in units for d in range(2)]

    lbs = {}
    for p in range(pairs):
        for d in range(2):
            x = lb_ref[p, d]
            e = jnp.exp(x - jnp.max(x, axis=0, keepdims=True))
            sm = e / jnp.sum(e, axis=0, keepdims=True)
            if layer > 0:
                lbs[p, d] = jnp.sum(sm[1:layer + 1], axis=0, keepdims=True)
            else:
                lbs[p, d] = jnp.zeros((1, LANES), F32)

    def stack_heads(x):
        return jnp.concatenate([jnp.where(lo, x, 0.0), jnp.where(lo, 0.0, x)], axis=0).astype(BF16)

    def both_heads(mask):
        return jnp.concatenate([mask, mask], axis=1)

    q, vt, key, ex, acc = {}, {}, {}, {}, {}
    for (g, p) in units:
        rs = slice(g * TOK, (g + 1) * TOK)
        q[g, p] = _silu(z_ref[rs, p * LANES:(p + 1) * LANES])
        vt[g, p] = z_ref[rs, 3 * nk + p * LANES:3 * nk + (p + 1) * LANES].T
    for g in range(g_tiles):
        for d in range(2):
            lg = []
            for p in range(pairs):
                f = z_ref[g * TOK:(g + 1) * TOK, (1 + d) * nk + p * LANES:(1 + d) * nk + (p + 1) * LANES]
                forget = lbs[p, d] + (1.0 - lbs[p, d]) * _sigmoid(f)
                key[g, p, d] = 1.0 - forget
                lg.append(jnp.log(forget))
            spans = _dot01(w_ref[d], jnp.concatenate(lg, axis=1), terms=HGRN_SPAN_TERMS)
            for p in range(pairs):
                ex[g, p, d] = spans[:, p * LANES:(p + 1) * LANES]
    for ch in chains:
        acc[ch] = _dot_nt(key[ch].astype(BF16), stack_heads(q[ch[0], ch[1]])) * both_heads(msk_ref[0])
    for lvl in range(1, HGRN_LEVELS + 1):
        for ch in chains:
            g, p, d = ch
            tau = row if d == 0 else TOK - 1 - row
            e = jnp.exp(ex[ch][(lvl - 1) * TOK:lvl * TOK])
            up = (tau & (1 << (lvl - 1))) != 0
            qt = jnp.where(up, q[g, p] * e, 0.0)
            kt = jnp.where(up, 0.0, key[ch] * e)
            a = _dot_nt(kt.astype(BF16), stack_heads(qt))
            if lvl < HGRN_LEVELS:
                a = a * both_heads(msk_ref[lvl])
            acc[ch] = acc[ch] + a
    base = HGRN_LEVELS * TOK
    for ch in chains:
        g, p, d = ch
        rs = slice(g * TOK, (g + 1) * TOK)
        vv = vt[g, p]
        intra_t = (_dot(jnp.where(row < 64, vv, 0.0).astype(BF16), acc[ch][:, 0:LANES].astype(BF16))
                   + _dot(jnp.where(row < 64, 0.0, vv).astype(BF16), acc[ch][:, LANES:2 * LANES].astype(BF16)))
        intra_ref[d, p, rs, :] = intra_t.T
        qd_ref[d, p, rs, :] = (q[g, p] * jnp.exp(ex[ch][base:base + TOK])).astype(BF16)
        kd = (key[ch] * jnp.exp(ex[ch][base + TOK:base + 2 * TOK])).astype(BF16)
        kv_ref[d, p, rs, :] = _dot(vv.astype(BF16), kd) * msk_ref[6]
        gl_ref[d, p, g * 8:(g + 1) * 8, :] = jnp.exp(ex[ch][base + 2 * TOK:base + 2 * TOK + 8])


def _hgrn_prep(zc, layer, lb, consts):
    n = zc.shape[0]
    pairs = C_HEADS // 2
    g_tiles = 2
    rb = g_tiles * TOK
    msk, wst = consts
    n_in = 4 * C_HEADS * C_DK
    return pl.pallas_call(
        functools.partial(_hgrn_prep_kernel, layer=layer, g_tiles=g_tiles),
        grid=(n // rb,),
        in_specs=[
            pl.BlockSpec((rb, n_in), lambda i: (i, 0)),
            pl.BlockSpec(lb.shape, lambda i: (0, 0, 0, 0)),
            pl.BlockSpec(msk.shape, lambda i: (0, 0, 0)),
            pl.BlockSpec(wst.shape, lambda i: (0, 0, 0)),
        ],
        out_specs=[pl.BlockSpec((2, pairs, rb, LANES), lambda i: (0, 0, i, 0))] * 3
        + [pl.BlockSpec((2, pairs, g_tiles * 8, LANES), lambda i: (0, 0, i, 0))],
        out_shape=[jax.ShapeDtypeStruct((2, pairs, n, LANES), t_) for t_ in (F32, BF16, F32)]
        + [jax.ShapeDtypeStruct((2, pairs, n // TOK * 8, LANES), F32)],
        compiler_params=_cparams("parallel"),
        name="hgrn_prep",
    )(zc, lb, msk, wst)


def _hgrn_scan_kernel(*refs, sb, tb, has_s0):
    fwd, bwd = refs[0:4], refs[4:8]
    if has_s0:
        s0_ref, of_ref, ob_ref, s_s = refs[8:]
    else:
        _, of_ref, ob_ref, sout_ref, s_s = refs[8:]
    t = pl.program_id(1)
    n_t = pl.num_programs(1)
    n_g = tb // TOK
    pairs = C_HEADS // 2
    chains = [(d, sq, p) for d in range(2) for sq in range(sb) for p in range(pairs)]

    @pl.when(t == 0)
    def _():
        for (d, sq, p) in chains:
            if has_s0:
                s_s[d, sq, p] = _pair_blockdiag(s0_ref[sq, 0, d, 2 * p], s0_ref[sq, 0, d, 2 * p + 1])
            else:
                s_s[d, sq, p] = jnp.zeros((LANES, LANES), F32)

    def body(gi, carry):
        for (d, sq, p) in chains:
            intra_r, qd_r, kv_r, gl_r = (fwd, bwd)[d]
            o_r = (of_ref, ob_ref)[d]
            g = gi if d == 0 else n_g - 1 - gi
            r0 = pl.multiple_of(g * TOK, TOK)
            st = s_s[d, sq, p]
            o = intra_r[0, p, sq, pl.ds(r0, TOK), :] + _dot_nt(qd_r[0, p, sq, pl.ds(r0, TOK), :], st.astype(BF16))
            o_r[sq, pl.ds(r0, TOK), p * LANES:(p + 1) * LANES] = o
            glr = gl_r[0, p, sq, pl.ds(pl.multiple_of(g * 8, 8), 8), :][0:1]
            s_s[d, sq, p] = st * glr + kv_r[0, p, sq, pl.ds(r0, TOK), :]
        return carry

    lax.fori_loop(0, n_g, body, 0)

    if not has_s0:
        @pl.when(t == n_t - 1)
        def _():
            for (d, sq, p) in chains:
                s = s_s[d, sq, p].T
                sout_ref[sq, 0, d, 2 * p] = s[0:64, 0:64]
                sout_ref[sq, 0, d, 2 * p + 1] = s[64:128, 64:128]


def _hgrn_scan(prep, n_seq, seq_len, layer, s0t=None, s_all=None):
    pairs = C_HEADS // 2
    has_s0 = s0t is not None
    sb = n_seq if n_seq <= 2 else 8
    tb = min(seq_len, 512)
    n_t = seq_len // tb
    arrs = [a.reshape(2, pairs, n_seq, a.shape[2] // n_seq, a.shape[3]) for a in prep]

    def specs(d):
        tmap = (lambda s, t: (d, 0, s, t, 0)) if d == 0 else (lambda s, t: (d, 0, s, n_t - 1 - t, 0))
        return [pl.BlockSpec((1, pairs, sb, a.shape[3] // n_t, a.shape[4]), tmap) for a in arrs]

    in_specs = specs(0) + specs(1)
    args = arrs + arrs
    st_spec = pl.BlockSpec((sb, 1, 2, C_HEADS, C_DK, C_DV), lambda s, t: (s, layer, 0, 0, 0, 0))
    width = C_HEADS * C_DV
    out_specs = [
        pl.BlockSpec((sb, tb, width), lambda s, t: (s, t, 0)),
        pl.BlockSpec((sb, tb, width), lambda s, t: (s, n_t - 1 - t, 0)),
    ]
    out_shape = [jax.ShapeDtypeStruct((n_seq, seq_len, width), F32)] * 2
    if has_s0:
        in_specs.append(st_spec)
        args.append(s0t)
        aliases = {}
    else:
        in_specs.append(pl.BlockSpec(memory_space=pl.ANY))
        args.append(s_all)
        out_specs.append(st_spec)
        out_shape.append(jax.ShapeDtypeStruct(s_all.shape, F32))
        aliases = {len(args) - 1: 2}
    outs = pl.pallas_call(
        functools.partial(_hgrn_scan_kernel, sb=sb, tb=tb, has_s0=has_s0),
        grid=(n_seq // sb, n_t),
        in_specs=in_specs,
        out_specs=out_specs,
        out_shape=out_shape,
        input_output_aliases=aliases,
        scratch_shapes=[pltpu.VMEM((2, sb, pairs, LANES, LANES), F32)],
        compiler_params=_cparams("parallel", "arbitrary"),
        name="hgrn_scan_s0" if has_s0 else "hgrn_scan",
    )(*args)
    n = n_seq * seq_len
    return (outs[0].reshape(n, width), outs[1].reshape(n, width)) + tuple(outs[2:])


FFN_SPLIT = 2
ROW_SPLIT = 2


def _tail_kernel(x_ref, oa_ref, obf_ref, obb_ref, bg_ref, ocf_ref, ocb_ref, cg_ref, mod_ref, bn_ref, cn_ref,
                 wo_ref, lng_ref, lnb_ref, wi_ref, wd_ref, o_ref, *, d_model, d_ff, alpha):
    m = mod_ref[0]
    na, nb = oa_ref.shape[1], obf_ref.shape[1]
    tm = x_ref.shape[0]
    parts = [slice(i * tm // ROW_SPLIT, (i + 1) * tm // ROW_SPLIT) for i in range(ROW_SPLIT)]
    ob = [_gated_head_norm(obf_ref[r, :] + obb_ref[r, :], bn_ref[0], bg_ref[r, :]) for r in parts]
    oc = [_gated_head_norm(ocf_ref[r, :] + ocb_ref[r, :], cn_ref[0], cg_ref[r, :]) for r in parts]
    y = [_bdot(oa_ref[r, :], wo_ref[0:na]) + _bdot(ob[i], wo_ref[na:na + nb]) + _bdot(oc[i], wo_ref[na + nb:])
         for i, r in enumerate(parts)]
    x1 = [_layer_norm(alpha * x_ref[r, :] + m[:, 2 * d_model:3 * d_model] * y[i], lng_ref[0, 0:1], lnb_ref[0, 0:1])
          for i, r in enumerate(parts)]
    h = [(x1[i] * (1.0 + m[:, 4 * d_model:5 * d_model]) + m[:, 3 * d_model:4 * d_model]).astype(BF16)
         for i in range(ROW_SPLIT)]
    ck = d_ff // FFN_SPLIT
    acc = [None] * ROW_SPLIT
    for c0 in range(0, d_ff, ck):
        for i in range(ROW_SPLIT):
            gt = _dot(h[i], wi_ref[:, c0:c0 + ck])
            up = _dot(h[i], wi_ref[:, d_ff + c0:d_ff + c0 + ck])
            part = _dot((_silu(gt) * up).astype(BF16), wd_ref[c0:c0 + ck, :])
            acc[i] = part if acc[i] is None else acc[i] + part
    for i, r in enumerate(parts):
        o_ref[r, :] = _layer_norm(alpha * x1[i] + m[:, 5 * d_model:6 * d_model] * acc[i],
                                  lng_ref[0, 1:2], lnb_ref[0, 1:2])


def _tail(x, oa, obf, obb, zb, ocf, ocb, zc, mod, layer, bnorm, cnorm, w_out, ln_g, ln_b, w_ffn_in, w_ffn_out,
          alpha):
    n, d = x.shape
    tm = 512
    d_ff = w_ffn_out.shape[1]
    rows_per_mod = n // mod.shape[0]
    nb, nc = obf.shape[1], ocf.shape[1]
    row = lambda w_: pl.BlockSpec((tm, w_), lambda i: (i, 0))
    last_cols = lambda a, w_: pl.BlockSpec((tm, w_), lambda i: (i, a.shape[1] // w_ - 1))
    per_layer = lambda w_: pl.BlockSpec((1, 1, w_), lambda i: (layer, 0, 0))
    resident = lambda a: pl.BlockSpec((None,) + a.shape[1:], lambda i: (layer, 0, 0), pipeline_mode=pl.Buffered(1))
    return pl.pallas_call(
        functools.partial(_tail_kernel, d_model=d, d_ff=d_ff, alpha=alpha),
        grid=(n // tm,),
        in_specs=[
            row(d), row(oa.shape[1]),
            row(nb), row(nb), last_cols(zb, nb),
            row(nc), row(nc), last_cols(zc, nc),
            pl.BlockSpec((1, 1, mod.shape[2]), lambda i: ((i * tm) // rows_per_mod, 0, 0)),
            per_layer(nb), per_layer(nc),
            resident(w_out),
            pl.BlockSpec((1, 2, d), lambda i: (layer, 0, 0)),
            pl.BlockSpec((1, 2, d), lambda i: (layer, 0, 0)),
            resident(w_ffn_in), resident(w_ffn_out),
        ],
        out_specs=pl.BlockSpec((tm, d), lambda i: (i, 0)),
        out_shape=jax.ShapeDtypeStruct((n, d), F32),
        compiler_params=_cparams("parallel"),
        name="out_proj_ffn",
    )(x, oa, obf, obb, zb, ocf, ocb, zc, mod, bnorm, cnorm, w_out, ln_g, ln_b, w_ffn_in, w_ffn_out)


def _pair_scalar_order(p):
    return [d * B_HEADS + 2 * p + hh for d in range(2) for hh in range(2)]


N_SC = 4 * B_HEADS


def _w_in_split_kernel(w_ref, sel_ref, ws_ref, wc_ref):
    x = w_ref[0]
    b0 = N_A + N_B
    slab = x[:, b0:b0 + LANES].astype(BF16)
    ws_ref[0] = _dot(slab, sel_ref[...]).astype(BF16)
    wc_ref[0] = x[:, b0 + N_SC:].astype(BF16)


def _layout_w_in(w_in):
    depth, d, d_in = w_in.shape
    sel = np.zeros((LANES, N_S), np.float32)
    for p in range(B_HEADS // 2):
        order = _pair_scalar_order(p)
        for j, col in enumerate(order + [2 * B_HEADS + i for i in order]):
            sel[col, p * LANES + j] = 1.0
    tr = 256
    ws, wc = pl.pallas_call(
        _w_in_split_kernel,
        grid=(depth, d // tr),
        in_specs=[pl.BlockSpec((1, tr, d_in), lambda l, r: (l, r, 0)),
                  pl.BlockSpec((LANES, N_S), lambda l, r: (0, 0))],
        out_specs=[pl.BlockSpec((1, tr, N_S), lambda l, r: (l, r, 0)),
                   pl.BlockSpec((1, tr, N_C), lambda l, r: (l, r, 0))],
        out_shape=[jax.ShapeDtypeStruct((depth, d, N_S), BF16), jax.ShapeDtypeStruct((depth, d, N_C), BF16)],
        compiler_params=_cparams("parallel", "parallel"),
        name="w_in_split",
    )(w_in, jnp.asarray(sel, BF16))
    return w_in, ws, wc


def _delta_params(delta_a_log, delta_dt_bias):
    depth = delta_a_log.shape[0]
    pairs = B_HEADS // 2
    al = delta_a_log.reshape(depth, 2 * B_HEADS)
    dt = delta_dt_bias.reshape(depth, 2 * B_HEADS)
    prow = jnp.zeros((depth, pairs, 8, LANES), F32)
    pcol = jnp.zeros((depth, pairs, 8, LANES), F32)
    for p in range(pairs):
        order = np.asarray(_pair_scalar_order(p))
        prow = prow.at[:, p, 0, 4:8].set(al[:, order]).at[:, p, 1, 4:8].set(dt[:, order])
        pcol = pcol.at[:, p, 4:8, 0].set(al[:, order]).at[:, p, 4:8, 1].set(dt[:, order])
    return prow, pcol


def kernel(x_prompt, x_sample, cache_attn_k, cache_attn_v, state_delta, state_hgrn, c, c_ctx, w_mod, b_mod, w_in, conv_w, delta_a_log, delta_dt_bias, delta_norm, hgrn_lb, hgrn_norm, diff_lambda, diff_norm, w_out, ln_g, ln_b, w_ffn_in, w_ffn_out):
    depth, d, _ = w_in.shape
    bp, lp, _ = x_prompt.shape
    bs, ls, _ = x_sample.shape
    alpha = (2 * depth) ** 0.25

    cond_rows = 8 * ((1 + bs + 7) // 8)
    cond = jnp.concatenate([c_ctx[None], c, jnp.zeros((cond_rows - 1 - bs, d), F32)], axis=0)
    mod = _modulation(cond, w_mod, b_mod)

    w_in_l = _layout_w_in(w_in)
    w_out_b = w_out.astype(BF16)
    w_ffn_in_b = w_ffn_in.astype(BF16)
    w_ffn_out_b = w_ffn_out.astype(BF16)
    prow, pcol = _delta_params(delta_a_log, delta_dt_bias)
    bnorm = jnp.tile(delta_norm, (1, B_HEADS)).reshape(depth, 1, B_HEADS * B_DV)
    cnorm = jnp.tile(hgrn_norm, (1, C_HEADS)).reshape(depth, 1, C_HEADS * C_DV)
    lb = hgrn_lb.reshape(2, depth, C_HEADS // 2, LANES).transpose(2, 0, 1, 3)
    rope = _rope_tables(ls)
    dconsts = _delta_consts()
    head_mask = dconsts[0][0, M_HEADS]
    hconsts = _hgrn_consts()
    past = cache_attn_k.shape[2]
    ck = cache_attn_k.reshape(bs, depth, past, A_HEADS * 2 * A_QK)
    cv = cache_attn_v.reshape(bs, depth, past, A_HEADS * A_V)
    sh0 = jnp.swapaxes(state_hgrn, -1, -2)
    tails = (bnorm, cnorm, w_out_b, ln_g, ln_b, w_ffn_in_b, w_ffn_out_b, alpha)

    xp = x_prompt.reshape(bp * lp, d)
    xs = x_sample.reshape(bs * ls, d)
    k_all = jnp.zeros((bp, depth, lp * A_HEADS, 2 * A_QK), F32)
    v_all = jnp.zeros((bp, depth, lp * A_HEADS, A_V), F32)
    sd_all = jnp.zeros((bp, depth, 2, B_HEADS, B_DK, B_DV), F32)
    sh_all = jnp.zeros((bp, depth, 2, C_HEADS, C_DK, C_DV), F32)
    for l in range(depth):
        mod_p = mod[l, 0:1][:, None, :]
        mod_s = mod[l, 1:1 + bs][:, None, :]

        zq, k_all, v_all, zb, zs, zc, zsr = _in_proj(xp, mod_p, w_in_l, lp, kv_out=(k_all, v_all), layer=l)
        oa = _attention(zq, lp, l, diff_lambda, diff_norm, kv_all=(k_all, v_all))
        prep = _delta_prep(zb, zs, zsr, lp, l, conv_w, prow, pcol, dconsts)
        obf, obb, sd_all = _delta_scan(prep, bp, lp, l, head_mask, s_all=sd_all)
        ocf, ocb, sh_all = _hgrn_scan(_hgrn_prep(zc, l, lb, hconsts), bp, lp, l, s_all=sh_all)
        xp = _tail(xp, oa, obf, obb, zb, ocf, ocb, zc, mod_p, l, *tails)

        za, zb, zs, zc, zsr = _in_proj(xs, mod_s, w_in_l, ls, rope_tabs=rope, layer=l)
        oa = _attention(za, ls, l, diff_lambda, diff_norm, ctx_kv=(ck, cv))
        prep = _delta_prep(zb, zs, zsr, ls, l, conv_w, prow, pcol, dconsts)
        obf, obb = _delta_scan(prep, bs, ls, l, head_mask, s0=state_delta)
        ocf, ocb = _hgrn_scan(_hgrn_prep(zc, l, lb, hconsts), bs, ls, l, s0t=sh0)
        xs = _tail(xs, oa, obf, obb, zb, ocf, ocb, zc, mod_s, l, *tails)

    return (xp.reshape(bp, lp, d), xs.reshape(bs, ls, d),
            k_all.reshape(bp, depth, lp, A_HEADS, 2 * A_QK), v_all.reshape(bp, depth, lp, A_HEADS, A_V),
            sd_all, sh_all)
```

```python
import functools
import math

import numpy as np
import jax
import jax.numpy as jnp
from jax import lax
from jax.experimental import pallas as pl
from jax.experimental.pallas import tpu as pltpu

F32 = jnp.float32
BF16 = jnp.bfloat16

A_HEADS = 4
A_QK = 64
A_V = 2 * A_QK
B_HEADS = 4
B_DK = 64
B_DV = 64
C_HEADS = 4
C_DK = 64
C_DV = 64
GRID_W = 64
ROPE_BASE = 10000.0
LN_EPS = 1e-5
RMS_EPS = 1e-6
L2_EPS = 1e-6

LANES = 128
TOK = 128
DELTA_CHUNK = 64
VMEM_LIMIT = 60 * 1024 * 1024


def _cparams(*sem):
    return pltpu.CompilerParams(dimension_semantics=sem, vmem_limit_bytes=VMEM_LIMIT)


def _dot(a, b):
    return jnp.dot(a, b, preferred_element_type=F32)


def _dot_nt(a, b):
    return lax.dot_general(a, b, (((1,), (1,)), ((), ())), preferred_element_type=F32)


def _bdot(a, b):
    return _dot(a.astype(BF16), b.astype(BF16))


def _split3(x):
    hi = x.astype(BF16)
    r = x - hi.astype(F32)
    mid = r.astype(BF16)
    lo = (r - mid.astype(F32)).astype(BF16)
    return hi, mid, lo


def _dot01(w01_rep, x, terms=3):
    pieces = _split3(x)[:terms]
    return _dot(w01_rep, jnp.concatenate(pieces, axis=0))


def _dot01_nt(x, w01):
    hi, mid, lo = _split3(x)
    return (_dot_nt(lo, w01) + _dot_nt(mid, w01)) + _dot_nt(hi, w01)


def _sigmoid(x):
    return 1.0 / (1.0 + jnp.exp(-x))


def _silu(x):
    return x * _sigmoid(x)


def _softplus(x):
    return jnp.maximum(x, 0.0) + jnp.log(1.0 + jnp.exp(-jnp.abs(x)))


def _lane_iota(shape):
    return lax.broadcasted_iota(jnp.int32, shape, len(shape) - 1)


def _row_iota(shape):
    return lax.broadcasted_iota(jnp.int32, shape, len(shape) - 2)


def _layer_norm(x, g, b):
    mu = jnp.mean(x, axis=-1, keepdims=True)
    xc = x - mu
    var = jnp.mean(xc * xc, axis=-1, keepdims=True)
    return xc * lax.rsqrt(var + LN_EPS) * g + b


def _mod_kernel(c_ref, w_ref, b_ref, o_ref):
    cs = _silu(c_ref[...])
    o_ref[0] = _bdot(cs, w_ref[0]) + b_ref[0]


def _modulation(cond, w_mod, b_mod):
    depth, d, n = w_mod.shape
    rows = cond.shape[0]
    tn = 1536 if n % 1536 == 0 else n
    return pl.pallas_call(
        _mod_kernel,
        grid=(depth, n // tn),
        in_specs=[
            pl.BlockSpec((rows, d), lambda l, j: (0, 0)),
            pl.BlockSpec((1, d, tn), lambda l, j: (l, 0, j)),
            pl.BlockSpec((1, 1, tn), lambda l, j: (l, 0, j)),
        ],
        out_specs=pl.BlockSpec((1, rows, tn), lambda l, j: (l, 0, j)),
        out_shape=jax.ShapeDtypeStruct((depth, rows, n), F32),
        compiler_params=_cparams("parallel", "parallel"),
        name="modulation",
    )(cond, w_mod, b_mod.reshape(depth, 1, n))


N_A = 3 * A_HEADS * A_V
N_B = 4 * B_HEADS * B_DK
N_S = 2 * LANES
N_C = 5 * C_HEADS * C_DK
N_Z = N_A + N_B + N_S + N_C


N_QK = A_HEADS * 2 * A_QK
Q_SCALE = (A_QK ** -0.5) * math.log2(math.e)


def _in_proj_kernel(x_ref, mod_ref, wab_ref, ws_ref, wc_ref, *rest, d_model, rope):
    if rope:
        rc_ref, rs1_ref, rs2_ref, za_ref, zb_ref, zs_ref, zc_ref, zsr_ref, w_ref = rest
    else:
        _, _, zq_ref, ko_ref, vo_ref, zb_ref, zs_ref, zc_ref, zsr_ref, w_ref = rest

    @pl.when(pl.program_id(0) == 0)
    def _():
        w_ref[...] = wab_ref[...].T.astype(BF16)

    m = mod_ref[0]
    shift = m[:, 0:d_model]
    scale = m[:, d_model:2 * d_model]
    h = (x_ref[...] * (1.0 + scale) + shift).astype(BF16)
    if rope:
        wide = 2 * LANES
        rc, rs1, rs2 = (jnp.concatenate([t[...], t[...]], axis=1) for t in (rc_ref, rs1_ref, rs2_ref))
        for g in range(N_A // wide):
            blk = _dot(h, w_ref[:, g * wide:(g + 1) * wide])
            if g < 2 * N_QK // wide:
                up = pltpu.roll(blk, wide - 16, axis=1)
                dn = pltpu.roll(blk, 16, axis=1)
                blk = blk * rc + up * rs1 + dn * rs2
            if g < N_QK // wide:
                blk = blk * Q_SCALE
            za_ref[:, g * wide:(g + 1) * wide] = blk.astype(za_ref.dtype)
    else:
        zq_ref[...] = _dot(h, w_ref[:, 0:N_QK])
        kk = _dot(h, w_ref[:, N_QK:2 * N_QK])
        vv = _dot(h, w_ref[:, 2 * N_QK:N_A])
        rows = kk.shape[0]
        for hd in range(A_HEADS):
            ko_ref[0, 0, pl.ds(hd, rows, stride=A_HEADS), :] = kk[:, hd * LANES:(hd + 1) * LANES]
            vo_ref[0, 0, pl.ds(hd, rows, stride=A_HEADS), :] = vv[:, hd * LANES:(hd + 1) * LANES]
    zb_ref[...] = _dot(h, w_ref[:, N_A:N_A + N_B])
    zs = _dot(h, ws_ref[...])
    zs_ref[...] = zs
    for g in range(zs.shape[0] // TOK):
        for p in range(N_S // LANES):
            zsr_ref[p, g] = zs[g * TOK:(g + 1) * TOK, p * LANES:(p + 1) * LANES].T[0:8, :]
    zc_ref[...] = _dot(h, wc_ref[...])


def _in_proj(x, mod, w, seq_len, rope_tabs=None, kv_out=None, layer=0):
    n, d = x.shape
    rope = rope_tabs is not None
    tm = min(512, seq_len) if rope else seq_len
    tiles_per_seq = seq_len // tm
    rows_per_mod = n // mod.shape[0]
    in_specs = [
        pl.BlockSpec((tm, d), lambda i: (i, 0)),
        pl.BlockSpec((1, 1, mod.shape[2]), lambda i: ((i * tm) // rows_per_mod, 0, 0)),
        pl.BlockSpec((None, N_A + N_B, d), lambda i: (layer, 0, 0), pipeline_mode=pl.Buffered(1)),
        pl.BlockSpec((None, d, N_S), lambda i: (layer, 0, 0)),
        pl.BlockSpec((None, d, N_C), lambda i: (layer, 0, 0)),
    ]
    args = [x, mod] + list(w)
    row = lambda w_: pl.BlockSpec((tm, w_), lambda i: (i, 0))
    tail_w = [N_B, N_S, N_C]
    zsr_spec = pl.BlockSpec((N_S // LANES, tm // TOK, 8, TOK), lambda i: (0, i, 0, 0))
    zsr_shape = jax.ShapeDtypeStruct((N_S // LANES, n // TOK, 8, TOK), F32)
    if rope:
        in_specs += [pl.BlockSpec((tm, LANES), lambda i: (i % tiles_per_seq, 0))] * 3
        args += list(rope_tabs)
        out_specs = [row(N_A)] + [row(w_) for w_ in tail_w] + [zsr_spec]
        out_shape = ([jax.ShapeDtypeStruct((n, N_A), BF16)] + [jax.ShapeDtypeStruct((n, w_), F32) for w_ in tail_w]
                     + [zsr_shape])
        aliases = {}
    else:
        k_all, v_all = kv_out
        in_specs += [pl.BlockSpec(memory_space=pl.ANY)] * 2
        args += [k_all, v_all]
        kv_spec = lambda a: pl.BlockSpec((1, 1, tm * A_HEADS, a.shape[3]), lambda i: (i, layer, 0, 0))
        out_specs = [row(N_QK), kv_spec(k_all), kv_spec(v_all)] + [row(w_) for w_ in tail_w] + [zsr_spec]
        out_shape = ([jax.ShapeDtypeStruct((n, N_QK), F32), jax.ShapeDtypeStruct(k_all.shape, F32),
                      jax.ShapeDtypeStruct(v_all.shape, F32)]
                     + [jax.ShapeDtypeStruct((n, w_), F32) for w_ in tail_w] + [zsr_shape])
        aliases = {5: 1, 6: 2}
    return pl.pallas_call(
        functools.partial(_in_proj_kernel, d_model=d, rope=rope),
        grid=(n // tm,),
        in_specs=in_specs,
        out_specs=out_specs,
        out_shape=out_shape,
        input_output_aliases=aliases,
        scratch_shapes=[pltpu.VMEM((d, N_A + N_B), BF16)],
        compiler_params=_cparams("arbitrary"),
        name="in_proj_rope" if rope else "in_proj",
    )(*args)


def _rope_tables(seq_len):
    half = A_QK // 2
    nf = half // 2
    pos = jnp.arange(seq_len)
    row = (pos // GRID_W).astype(F32)
    col = (pos % GRID_W).astype(F32)
    inv_freq = ROPE_BASE ** (-jnp.arange(nf, dtype=F32) / nf)
    ang_r = row[:, None] * inv_freq
    ang_c = col[:, None] * inv_freq
    cos64 = jnp.concatenate([jnp.cos(ang_r), jnp.cos(ang_r), jnp.cos(ang_c), jnp.cos(ang_c)], -1)
    zero = jnp.zeros_like(ang_r)
    s1_64 = jnp.concatenate([-jnp.sin(ang_r), zero, -jnp.sin(ang_c), zero], -1)
    s2_64 = jnp.concatenate([zero, jnp.sin(ang_r), zero, jnp.sin(ang_c)], -1)
    rep = LANES // A_QK
    return tuple(jnp.tile(t, (1, rep)) for t in (cos64, s1_64, s2_64))


def _attn_kernel(q_ref, k_ref, v_ref, *rest, lam_init, latent, sb, tq, seq_len):
    if latent:
        ck_ref, cv_ref, dl_ref, nrm_ref, o_ref, s_scr, vx_scr = rest
        past = cv_ref.shape[2]

        @pl.when(pl.program_id(1) == 0)
        def _():
            ones = jnp.ones((seq_len, LANES), BF16)
            for h in range(A_HEADS):
                hs = slice(h * LANES, (h + 1) * LANES)
                vx_scr[0:seq_len, 2 * h * LANES:(2 * h + 1) * LANES] = v_ref[:, hs]
                vx_scr[0:seq_len, (2 * h + 1) * LANES:(2 * h + 2) * LANES] = ones
                vx_scr[seq_len:, 2 * h * LANES:(2 * h + 1) * LANES] = cv_ref[0, 0, :, hs].astype(BF16)
                vx_scr[seq_len:, (2 * h + 1) * LANES:(2 * h + 2) * LANES] = ones[0:past]
    else:
        dl_ref, nrm_ref, o_ref, s_scr = rest
    dl = dl_ref[0]
    lam = (jnp.exp(jnp.sum(dl[0:1] * dl[1:2], axis=1, keepdims=True))
           - jnp.exp(jnp.sum(dl[2:3] * dl[3:4], axis=1, keepdims=True)) + lam_init)
    lo = _lane_iota((tq, LANES)) < A_QK
    units = [(sq, h) for sq in range(sb) for h in range(A_HEADS)]

    def scores(unit, slot):
        sq, h = unit
        hs = slice(h * LANES, (h + 1) * LANES)
        q = q_ref[sq * tq:(sq + 1) * tq, hs]
        if latent:
            k = k_ref[:, hs]
            zero = jnp.zeros_like(q)
        else:
            q = q * Q_SCALE
            k = k_ref[sq, 0, pl.ds(h, seq_len, stride=A_HEADS), :].astype(BF16)
            zero = 0.0
        qq = jnp.concatenate([jnp.where(lo, q, zero), jnp.where(lo, zero, q)], axis=0).astype(BF16)
        s_scr[slot, :, 0:seq_len] = _dot_nt(qq, k)
        if latent:
            s_scr[slot, :, seq_len:] = _dot_nt(qq, ck_ref[0, 0, :, hs].astype(BF16))

    def finish(unit, slot):
        sq, h = unit
        hs = slice(h * LANES, (h + 1) * LANES)
        s = s_scr[slot]
        if latent:
            e = jnp.exp2((s - jnp.max(s, axis=-1, keepdims=True)).astype(BF16))
            o2 = _dot(e, vx_scr[:, 2 * h * LANES:(2 * h + 2) * LANES])
            r = 1.0 / o2[:, LANES:LANES + 1]
            o = o2[:tq, 0:LANES] * r[:tq] - o2[tq:, 0:LANES] * (lam * r[tq:])
        else:
            e = jnp.exp2(s - jnp.max(s, axis=-1, keepdims=True))
            r = 1.0 / jnp.sum(e, axis=-1, keepdims=True)
            p = (e[:tq] * r[:tq] - e[tq:] * (lam * r[tq:])).astype(BF16)
            o = _dot(p, v_ref[sq, 0, pl.ds(h, seq_len, stride=A_HEADS), :].astype(BF16))
        o = o * lax.rsqrt(jnp.mean(o * o, axis=-1, keepdims=True) + RMS_EPS) * nrm_ref[0]
        o_ref[sq * tq:(sq + 1) * tq, hs] = o * (1.0 - lam_init)

    scores(units[0], 0)
    for i, unit in enumerate(units):
        if i + 1 < len(units):
            scores(units[i + 1], (i + 1) % 2)
        finish(unit, i % 2)


def _attention(q_arr, seq_len, layer, diff_lambda, diff_norm, kv_all=None, ctx_kv=None):
    n = q_arr.shape[0]
    n_seq = n // seq_len
    latent = ctx_kv is not None
    lam_init = 0.8 - 0.6 * math.exp(-0.3 * layer)
    if latent:
        tq = min(128, seq_len)
        sb = min(4, seq_len // tq)
        nq = seq_len // (sb * tq)
        col = lambda c: pl.BlockSpec((seq_len, N_QK), lambda s, i: (s, c))
        ck, cv = ctx_kv
        cspec = lambda a: pl.BlockSpec((1, 1) + a.shape[2:], lambda s, i: (s, layer, 0, 0))
        in_specs = [pl.BlockSpec((sb * tq, N_QK), lambda s, i: (s * nq + i, 0)), col(1), col(2), cspec(ck), cspec(cv)]
        args = [q_arr, q_arr, q_arr, ck, cv]
    else:
        sb, tq, nq = min(4, n_seq), seq_len, 1
        k_all, v_all = kv_all
        kspec = lambda a: pl.BlockSpec((sb, 1) + a.shape[2:], lambda s, i: (s, layer, 0, 0))
        in_specs = [pl.BlockSpec((sb * tq, N_QK), lambda s, i: (s, 0)), kspec(k_all), kspec(v_all)]
        args = [q_arr, k_all, v_all]
    in_specs += [
        pl.BlockSpec((1, 4, A_QK), lambda s, i: (layer, 0, 0)),
        pl.BlockSpec((1, 1, A_V), lambda s, i: (layer, 0, 0)),
    ]
    args += [diff_lambda, diff_norm.reshape(diff_norm.shape[0], 1, A_V)]
    keys = seq_len + (ctx_kv[0].shape[2] if latent else 0)
    return pl.pallas_call(
        functools.partial(_attn_kernel, lam_init=lam_init, latent=latent, sb=sb, tq=tq, seq_len=seq_len),
        grid=(n_seq // (1 if latent else sb), nq),
        in_specs=in_specs,
        out_specs=pl.BlockSpec((sb * tq, A_HEADS * A_V), lambda s, i: (s * nq + i, 0)),
        out_shape=jax.ShapeDtypeStruct((n, A_HEADS * A_V), F32),
        scratch_shapes=[pltpu.VMEM((2, 2 * tq, keys), F32)]
        + ([pltpu.VMEM((keys, 2 * A_HEADS * A_V), BF16)] if latent else []),
        compiler_params=_cparams("parallel", "arbitrary"),
        name="diff_attn_latent" if latent else "diff_attn",
    )(*args)


def _head_sum(x):
    lane = _lane_iota(x.shape)
    lo = lane < 64
    s0 = jnp.sum(jnp.where(lo, x, 0.0), axis=-1, keepdims=True)
    s1 = jnp.sum(jnp.where(lo, 0.0, x), axis=-1, keepdims=True)
    return jnp.where(lo, s0, s1)


def _pair_blockdiag(a, b):
    z = jnp.zeros_like(a)
    return jnp.concatenate([jnp.concatenate([a, z], axis=1), jnp.concatenate([z, b], axis=1)], axis=0)


def _gated_head_norm(o, norm_row, gate):
    parts = []
    for hp in range(o.shape[1] // LANES):
        blk = o[:, hp * LANES:(hp + 1) * LANES]
        parts.append(blk * lax.rsqrt(_head_sum(blk * blk) * (1.0 / 64.0) + RMS_EPS))
    return jnp.concatenate(parts, axis=1) * norm_row * _silu(gate)


DELTA_MERGES = (4, 8, 16, 32, 64)
M_CAUSAL, M_STRICT, M_PAIR, M_MERGE0 = 0, 1, 2, 3
M_EYE = M_MERGE0 + len(DELTA_MERGES)
M_HEADS = M_EYE + 1


def _delta_consts():
    t = np.arange(TOK)[:, None]
    s = np.arange(TOK)[None, :]

    def same(n):
        return (t // n) == (s // n)

    masks, cums = [], []
    for d in range(2):
        before = (s <= t) if d == 0 else (s >= t)
        strict = (s < t) if d == 0 else (s > t)
        causal = same(DELTA_CHUNK) & before
        st = same(DELTA_CHUNK) & strict
        rows = [causal, st, st & same(2)]
        rows += [st & same(n) & ~same(n // 2) for n in DELTA_MERGES]
        rows += [t == s, same(64)]
        masks.append(np.stack(rows))
        cums.append(np.concatenate([causal, same(DELTA_CHUNK) & ~before, same(DELTA_CHUNK)], 0))
    cums = np.stack(cums)
    return (jnp.asarray(np.stack(masks), F32), jnp.asarray(np.tile(cums, (1, 1, 3)), BF16),
            jnp.asarray(cums[:, 0:TOK], BF16))


def _delta_prep_kernel(z_ref, zp_ref, zn_ref, sc_ref, sr_ref, cw_ref, prow_ref, pcol_ref, msk_ref, cum_ref, cumr_ref,
                       u_ref, w_ref, qd_ref, qkm_ref, kdt_ref, gl_ref, *, blocks_per_seq, g_tiles):
    jloc = pl.program_id(0) % blocks_per_seq
    rb = g_tiles * TOK
    nqk = B_HEADS * B_DK
    pairs = B_HEADS // 2
    x = z_ref[...]
    cw = cw_ref[0]
    prev = jnp.where(jloc > 0, zp_ref[7:8, :], 0.0)
    nxt = jnp.where(jloc < blocks_per_seq - 1, zn_ref[0:1, :], 0.0)
    row = _row_iota(x.shape)
    dn = jnp.where(row == 0, prev, pltpu.roll(x, 1, axis=0))
    up = jnp.where(row == rb - 1, nxt, pltpu.roll(x, rb - 1, axis=0))
    y = _silu(cw[0:1] * dn + cw[1:2] * x + cw[2:3] * up)

    lo = _lane_iota((TOK, LANES)) < 64
    hi = jnp.logical_not(lo)
    units = [(g, p) for g in range(g_tiles) for p in range(pairs)]
    chains = [(g, p, d, hh) for (g, p) in units for d in range(2) for hh in range(2)]

    q, k, v, kk, qk, gcol, grow = {}, {}, {}, {}, {}, {}, {}
    for (g, p) in units:
        rs = slice(g * TOK, (g + 1) * TOK)
        qs = y[rs, p * LANES:(p + 1) * LANES]
        ks = y[rs, nqk + p * LANES:nqk + (p + 1) * LANES]
        v[g, p] = y[rs, 2 * nqk + p * LANES:2 * nqk + (p + 1) * LANES]
        q[g, p] = qs * lax.rsqrt(_head_sum(qs * qs) + L2_EPS) * (B_DK ** -0.5)
        k[g, p] = ks * lax.rsqrt(_head_sum(ks * ks) + L2_EPS)
    for (g, p) in units:
        k16 = k[g, p].astype(BF16)
        for hh, sel in enumerate((lo, hi)):
            kk[g, p, hh] = _dot_nt(jnp.where(sel, k[g, p], 0.0).astype(BF16), k16)
            qk[g, p, hh] = _dot_nt(jnp.where(sel, q[g, p], 0.0).astype(BF16), k16)
    for (g, p) in units:
        prow = prow_ref[0, p]
        pcol = pcol_ref[0, p]
        xs = sc_ref[g * TOK:(g + 1) * TOK, p * LANES:(p + 1) * LANES]
        lane = _lane_iota(xs.shape)
        gcol[g, p] = jnp.where(lane < 4, _sigmoid(xs), -jnp.exp(prow[0:1]) * _softplus(xs + prow[1:2]))
        xr = sr_ref[p, g]
        rowi = _row_iota(xr.shape)
        grow[g, p] = jnp.where(rowi < 4, _sigmoid(xr), -jnp.exp(pcol[:, 0:1]) * _softplus(xr + pcol[:, 1:2]))
    cs, br = {}, {}
    for g in range(g_tiles):
        gc = jnp.concatenate([gcol[g, p] for p in range(pairs)], axis=1)
        gr = jnp.concatenate([grow[g, p] for p in range(pairs)], axis=0)
        for d in range(2):
            csd = _dot01(cum_ref[d], gc)
            brd = _dot01_nt(gr, cumr_ref[d])
            for p in range(pairs):
                cs[g, p, d] = csd[:, p * LANES:(p + 1) * LANES]
                br[g, p, d] = brd[p * 8:(p + 1) * 8]
    beta, bcol, dec, m, x = {}, {}, {}, {}, {}
    for ch in chains:
        g, p, d, hh = ch
        cb, cg = d * 2 + hh, 4 + d * 2 + hh
        beta[ch] = gcol[g, p][:, cb:cb + 1]
        bcol[ch] = cs[g, p, d][0:TOK, cg:cg + 1]
        dec[ch] = jnp.exp(jnp.minimum(bcol[ch] - br[g, p, d][cg:cg + 1, :], 0.0)) * msk_ref[d, M_CAUSAL]
        m[ch] = kk[g, p, hh] * beta[ch] * dec[ch] * msk_ref[d, M_STRICT]
        x[ch] = msk_ref[d, M_EYE] - m[ch] * msk_ref[d, M_PAIR]
    for lvl in range(len(DELTA_MERGES)):
        yv = {ch: _bdot(m[ch] * msk_ref[ch[2], M_MERGE0 + lvl], x[ch]) for ch in chains}
        x = {ch: x[ch] - _bdot(x[ch], yv[ch]) for ch in chains}
    for (g, p) in units:
        rs = slice(g * TOK, (g + 1) * TOK)
        for d in range(2):
            sol = None
            eq, ek, gl, qkm = [], [], [], []
            for hh, sel in enumerate((lo, hi)):
                ch = (g, p, d, hh)
                cg = 4 + d * 2 + hh
                eb = jnp.exp(bcol[ch])
                rhs = jnp.concatenate([jnp.where(sel, v[g, p] * beta[ch], 0.0),
                                       jnp.where(sel, k[g, p] * (beta[ch] * eb), 0.0)], axis=1)
                part = _bdot(x[ch], rhs)
                sol = part if sol is None else sol + part
                qkm.append((qk[g, p, hh] * dec[ch]).astype(BF16))
                eq.append(eb)
                ek.append(jnp.exp(cs[g, p, d][TOK:2 * TOK, cg:cg + 1]))
                gl.append(jnp.exp(cs[g, p, d][2 * TOK:3 * TOK, cg:cg + 1]))
            u_ref[d, p, rs, :] = sol[:, 0:LANES]
            w_ref[d, p, rs, :] = sol[:, LANES:2 * LANES].astype(BF16)
            qd_ref[d, p, rs, :] = (q[g, p] * jnp.where(lo, eq[0], eq[1])).astype(BF16)
            qkm_ref[d, p, rs, :] = jnp.concatenate(qkm, axis=1)
            kd = k[g, p] * jnp.where(lo, ek[0], ek[1])
            kdt_ref[d, p, rs, :] = kd.T.astype(BF16)
            glf = jnp.where(lo, gl[0], gl[1])
            gl_ref[d, p, g * 16:(g + 1) * 16, :] = jnp.concatenate([glf[0:8], glf[64:72]], axis=0)


def _delta_prep(zb, zs, zs_rows, seq_len, layer, conv_w, prow, pcol, consts):
    n = zb.shape[0]
    pairs = B_HEADS // 2
    g_tiles = 2
    rb = g_tiles * TOK
    blocks_per_seq = seq_len // rb
    n_blocks = n // rb
    nconv = conv_w.shape[2]
    msk, cum, cum_rows = consts
    last8 = n // 8 - 1
    out_w = [LANES, LANES, LANES, 2 * LANES, LANES]
    out_t = [F32, BF16, BF16, BF16, BF16]
    return pl.pallas_call(
        functools.partial(_delta_prep_kernel, blocks_per_seq=blocks_per_seq, g_tiles=g_tiles),
        grid=(n_blocks,),
        in_specs=[
            pl.BlockSpec((rb, nconv), lambda i: (i, 0)),
            pl.BlockSpec((8, nconv), lambda i: (jnp.maximum(i * (rb // 8) - 1, 0), 0)),
            pl.BlockSpec((8, nconv), lambda i: (jnp.minimum((i + 1) * (rb // 8), last8), 0)),
            pl.BlockSpec((rb, pairs * LANES), lambda i: (i, 0)),
            pl.BlockSpec((pairs, g_tiles, 8, TOK), lambda i: (0, i, 0, 0)),
            pl.BlockSpec((1, 3, nconv), lambda i: (layer, 0, 0)),
            pl.BlockSpec((1, pairs, 8, LANES), lambda i: (layer, 0, 0, 0)),
            pl.BlockSpec((1, pairs, 8, LANES), lambda i: (layer, 0, 0, 0)),
            pl.BlockSpec(msk.shape, lambda i: (0, 0, 0, 0)),
            pl.BlockSpec(cum.shape, lambda i: (0, 0, 0)),
            pl.BlockSpec(cum_rows.shape, lambda i: (0, 0, 0)),
        ],
        out_specs=[pl.BlockSpec((2, pairs, rb, w_), lambda i: (0, 0, i, 0)) for w_ in out_w]
        + [pl.BlockSpec((2, pairs, g_tiles * 16, LANES), lambda i: (0, 0, i, 0))],
        out_shape=[jax.ShapeDtypeStruct((2, pairs, n, w_), t_) for w_, t_ in zip(out_w, out_t)]
        + [jax.ShapeDtypeStruct((2, pairs, n // TOK * 16, LANES), F32)],
        compiler_params=_cparams("parallel"),
        name="delta_prep",
    )(zb, zb, zb, zs, zs_rows, conv_w, prow, pcol, msk, cum, cum_rows)


def _delta_scan_kernel(*refs, sb, tb, has_s0):
    fwd, bwd = refs[0:6], refs[6:12]
    hm_ref = refs[12]
    if has_s0:
        s0_ref, of_ref, ob_ref, s_s = refs[13:]
    else:
        _, of_ref, ob_ref, sout_ref, s_s = refs[13:]
    t = pl.program_id(1)
    n_t = pl.num_programs(1)
    n_g = tb // TOK
    pairs = B_HEADS // 2
    chains = [(d, sq, p) for d in range(2) for sq in range(sb) for p in range(pairs)]

    @pl.when(t == 0)
    def _():
        for (d, sq, p) in chains:
            if has_s0:
                s_s[d, sq, p] = _pair_blockdiag(s0_ref[sq, 0, d, 2 * p], s0_ref[sq, 0, d, 2 * p + 1])
            else:
                s_s[d, sq, p] = jnp.zeros((LANES, LANES), F32)

    lane64 = _lane_iota((DELTA_CHUNK, LANES)) < 64
    zpad = jnp.zeros((DELTA_CHUNK, LANES), BF16)

    def body(gi, carry):
        for step in range(2):
            tmp = {}
            for ch in chains:
                d, sq, p = ch
                u_r, w_r = (fwd, bwd)[d][0:2]
                g = gi if d == 0 else n_g - 1 - gi
                c = step if d == 0 else 1 - step
                rc = pl.multiple_of(g * TOK + c * DELTA_CHUNK, DELTA_CHUNK)
                s = s_s[d, sq, p]
                s16 = s.astype(BF16)
                vn = u_r[0, p, sq, pl.ds(rc, DELTA_CHUNK), :] - _dot(w_r[0, p, sq, pl.ds(rc, DELTA_CHUNK), :], s16)
                tmp[ch] = (s, s16, vn, g, c, rc)
            for ch in chains:
                d, sq, p = ch
                _, _, qd_r, qkm_r, kdt_r, gl_r = (fwd, bwd)[d]
                o_r = (of_ref, ob_ref)[d]
                s, s16, vn, g, c, rc = tmp[ch]
                r0 = pl.multiple_of(g * TOK, TOK)
                v0 = jnp.where(lane64, vn, 0.0).astype(BF16)
                v1 = jnp.where(lane64, 0.0, vn).astype(BF16)
                vnb = vn.astype(BF16)
                if c == 0:
                    vext = jnp.concatenate([vnb, zpad], axis=0)
                    v2 = jnp.concatenate([v0, zpad, v1, zpad], axis=0)
                else:
                    vext = jnp.concatenate([zpad, vnb], axis=0)
                    v2 = jnp.concatenate([zpad, v0, zpad, v1], axis=0)
                o = (_dot(qd_r[0, p, sq, pl.ds(rc, DELTA_CHUNK), :], s16)
                     + _dot(qkm_r[0, p, sq, pl.ds(rc, DELTA_CHUNK), :], v2))
                o_r[sq, pl.ds(rc, DELTA_CHUNK), p * LANES:(p + 1) * LANES] = o
                glr = gl_r[0, p, sq, pl.ds(pl.multiple_of(g * 16 + c * 8, 8), 8), :][0:1]
                s_s[d, sq, p] = s * glr + _dot(kdt_r[0, p, sq, pl.ds(r0, TOK), :], vext) * hm_ref[...]
        return carry

    lax.fori_loop(0, n_g, body, 0)

    if not has_s0:
        @pl.when(t == n_t - 1)
        def _():
            for (d, sq, p) in chains:
                s = s_s[d, sq, p]
                sout_ref[sq, 0, d, 2 * p] = s[0:64, 0:64]
                sout_ref[sq, 0, d, 2 * p + 1] = s[64:128, 64:128]


def _delta_scan(prep, n_seq, seq_len, layer, head_mask, s0=None, s_all=None):
    pairs = B_HEADS // 2
    has_s0 = s0 is not None
    sb = n_seq if n_seq <= 2 else 8
    tb = min(seq_len, 512)
    n_t = seq_len // tb
    arrs = [a.reshape(2, pairs, n_seq, a.shape[2] // n_seq, a.shape[3]) for a in prep]

    def specs(d):
        tmap = (lambda s, t: (d, 0, s, t, 0)) if d == 0 else (lambda s, t: (d, 0, s, n_t - 1 - t, 0))
        return [pl.BlockSpec((1, pairs, sb, a.shape[3] // n_t, a.shape[4]), tmap) for a in arrs]

    in_specs = specs(0) + specs(1) + [pl.BlockSpec((TOK, LANES), lambda s, t: (0, 0))]
    args = arrs + arrs + [head_mask]
    st_spec = pl.BlockSpec((sb, 1, 2, B_HEADS, B_DK, B_DV), lambda s, t: (s, layer, 0, 0, 0, 0))
    width = B_HEADS * B_DV
    out_specs = [
        pl.BlockSpec((sb, tb, width), lambda s, t: (s, t, 0)),
        pl.BlockSpec((sb, tb, width), lambda s, t: (s, n_t - 1 - t, 0)),
    ]
    out_shape = [jax.ShapeDtypeStruct((n_seq, seq_len, width), F32)] * 2
    if has_s0:
        in_specs.append(st_spec)
        args.append(s0)
        aliases = {}
    else:
        in_specs.append(pl.BlockSpec(memory_space=pl.ANY))
        args.append(s_all)
        out_specs.append(st_spec)
        out_shape.append(jax.ShapeDtypeStruct(s_all.shape, F32))
        aliases = {len(args) - 1: 2}
    outs = pl.pallas_call(
        functools.partial(_delta_scan_kernel, sb=sb, tb=tb, has_s0=has_s0),
        grid=(n_seq // sb, n_t),
        in_specs=in_specs,
        out_specs=out_specs,
        out_shape=out_shape,
        input_output_aliases=aliases,
        scratch_shapes=[pltpu.VMEM((2, sb, pairs, LANES, LANES), F32)],
        compiler_params=_cparams("parallel", "arbitrary"),
        name="delta_scan_s0" if has_s0 else "delta_scan",
    )(*args)
    n = n_seq * seq_len
    return (outs[0].reshape(n, width), outs[1].reshape(n, width)) + tuple(outs[2:])


HGRN_LEVELS = 7
HGRN_WROWS = (HGRN_LEVELS + 2) * TOK + 8
HGRN_SPAN_TERMS = 2


def _hgrn_consts():
    t = np.arange(TOK)
    masks = [t[:, None] == t[None, :]]
    for lvl in range(1, HGRN_LEVELS + 1):
        n = 1 << lvl
        masks.append((t[:, None] // n) == (t[None, :] // n))
    ws = []
    for d in range(2):
        tau = t if d == 0 else TOK - 1 - t
        tt, ti = tau[:, None], tau[None, :]
        blocks = []
        for lvl in range(1, HGRN_LEVELS + 1):
            n = 1 << lvl
            piv = (tau - tau % n + n // 2 - 1)[:, None]
            upper = ((tau % n) >= n // 2)[:, None]
            blocks.append(np.where(upper, (ti > piv) & (ti <= tt), (ti > tt) & (ti <= piv)))
        blocks.append(ti <= tt)
        blocks.append(ti > tt)
        blocks.append(np.ones((8, TOK), bool))
        ws.append(np.concatenate(blocks, 0))
    return jnp.asarray(np.stack(masks), F32), jnp.asarray(np.tile(np.stack(ws), (1, 1, HGRN_SPAN_TERMS)), BF16)


def _hgrn_prep_kernel(z_ref, lb_ref, msk_ref, w_ref, intra_ref, qd_ref, kv_ref, gl_ref, *, layer, g_tiles):
    pairs = C_HEADS // 2
    nk = C_HEADS * C_DK
    lo = _lane_iota((TOK, LANES)) < 64
    row = _row_iota((TOK, LANES))
    units = [(g, p) for g in range(g_tiles) for p in range(pairs)]
    chains = [(g, p, d) for (g, p) in units for d in range(2)]

    lbs = {}
    for p in range(pairs):
        for d in range(2):
            x = lb_ref[p, d]
            e = jnp.exp(x - jnp.max(x, axis=0, keepdims=True))
            sm = e / jnp.sum(e, axis=0, keepdims=True)
            if layer > 0:
                lbs[p, d] = jnp.sum(sm[1:layer + 1], axis=0, keepdims=True)
            else:
                lbs[p, d] = jnp.zeros((1, LANES), F32)

    def stack_heads(x):
        return jnp.concatenate([jnp.where(lo, x, 0.0), jnp.where(lo, 0.0, x)], axis=0).astype(BF16)

    def both_heads(mask):
        return jnp.concatenate([mask, mask], axis=1)

    q, vt, key, ex, acc = {}, {}, {}, {}, {}
    for (g, p) in units:
        rs = slice(g * TOK, (g + 1) * TOK)
        q[g, p] = _silu(z_ref[rs, p * LANES:(p + 1) * LANES])
        vt[g, p] = z_ref[rs, 3 * nk + p * LANES:3 * nk + (p + 1) * LANES].T
    for g in range(g_tiles):
        for d in range(2):
            lg = []
            for p in range(pairs):
                f = z_ref[g * TOK:(g + 1) * TOK, (1 + d) * nk + p * LANES:(1 + d) * nk + (p + 1) * LANES]
                forget = lbs[p, d] + (1.0 - lbs[p, d]) * _sigmoid(f)
                key[g, p, d] = 1.0 - forget
                lg.append(jnp.log(forget))
            spans = _dot01(w_ref[d], jnp.concatenate(lg, axis=1), terms=HGRN_SPAN_TERMS)
            for p in range(pairs):
                ex[g, p, d] = spans[:, p * LANES:(p + 1) * LANES]
    for ch in chains:
        acc[ch] = _dot_nt(key[ch].astype(BF16), stack_heads(q[ch[0], ch[1]])) * both_heads(msk_ref[0])
    for lvl in range(1, HGRN_LEVELS + 1):
        for ch in chains:
            g, p, d = ch
            tau = row if d == 0 else TOK - 1 - row
            e = jnp.exp(ex[ch][(lvl - 1) * TOK:lvl * TOK])
            up = (tau & (1 << (lvl - 1))) != 0
            qt = jnp.where(up, q[g, p] * e, 0.0)
            kt = jnp.where(up, 0.0, key[ch] * e)
            a = _dot_nt(kt.astype(BF16), stack_heads(qt))
            if lvl < HGRN_LEVELS:
                a = a * both_heads(msk_ref[lvl])
            acc[ch] = acc[ch] + a
    base = HGRN_LEVELS * TOK
    for ch in chains:
        g, p, d = ch
        rs = slice(g * TOK, (g + 1) * TOK)
        vv = vt[g, p]
        intra_t = (_dot(jnp.where(row < 64, vv, 0.0).astype(BF16), acc[ch][:, 0:LANES].astype(BF16))
                   + _dot(jnp.where(row < 64, 0.0, vv).astype(BF16), acc[ch][:, LANES:2 * LANES].astype(BF16)))
        intra_ref[d, p, rs, :] = intra_t.T
        qd_ref[d, p, rs, :] = (q[g, p] * jnp.exp(ex[ch][base:base + TOK])).astype(BF16)
        kd = (key[ch] * jnp.exp(ex[ch][base + TOK:base + 2 * TOK])).astype(BF16)
        kv_ref[d, p, rs, :] = _dot(vv.astype(BF16), kd) * msk_ref[6]
        gl_ref[d, p, g * 8:(g + 1) * 8, :] = jnp.exp(ex[ch][base + 2 * TOK:base + 2 * TOK + 8])


def _hgrn_prep(zc, layer, lb, consts):
    n = zc.shape[0]
    pairs = C_HEADS // 2
    g_tiles = 2
    rb = g_tiles * TOK
    msk, wst = consts
    n_in = 4 * C_HEADS * C_DK
    return pl.pallas_call(
        functools.partial(_hgrn_prep_kernel, layer=layer, g_tiles=g_tiles),
        grid=(n // rb,),
        in_specs=[
            pl.BlockSpec((rb, n_in), lambda i: (i, 0)),
            pl.BlockSpec(lb.shape, lambda i: (0, 0, 0, 0)),
            pl.BlockSpec(msk.shape, lambda i: (0, 0, 0)),
            pl.BlockSpec(wst.shape, lambda i: (0, 0, 0)),
        ],
        out_specs=[pl.BlockSpec((2, pairs, rb, LANES), lambda i: (0, 0, i, 0))] * 3
        + [pl.BlockSpec((2, pairs, g_tiles * 8, LANES), lambda i: (0, 0, i, 0))],
        out_shape=[jax.ShapeDtypeStruct((2, pairs, n, LANES), t_) for t_ in (F32, BF16, F32)]
        + [jax.ShapeDtypeStruct((2, pairs, n // TOK * 8, LANES), F32)],
        compiler_params=_cparams("parallel"),
        name="hgrn_prep",
    )(zc, lb, msk, wst)


def _hgrn_scan_kernel(*refs, sb, tb, has_s0):
    fwd, bwd = refs[0:4], refs[4:8]
    if has_s0:
        s0_ref, of_ref, ob_ref, s_s = refs[8:]
    else:
        _, of_ref, ob_ref, sout_ref, s_s = refs[8:]
    t = pl.program_id(1)
    n_t = pl.num_programs(1)
    n_g = tb // TOK
    pairs = C_HEADS // 2
    chains = [(d, sq, p) for d in range(2) for sq in range(sb) for p in range(pairs)]

    @pl.when(t == 0)
    def _():
        for (d, sq, p) in chains:
            if has_s0:
                s_s[d, sq, p] = _pair_blockdiag(s0_ref[sq, 0, d, 2 * p], s0_ref[sq, 0, d, 2 * p + 1])
            else:
                s_s[d, sq, p] = jnp.zeros((LANES, LANES), F32)

    def body(gi, carry):
        for (d, sq, p) in chains:
            intra_r, qd_r, kv_r, gl_r = (fwd, bwd)[d]
            o_r = (of_ref, ob_ref)[d]
            g = gi if d == 0 else n_g - 1 - gi
            r0 = pl.multiple_of(g * TOK, TOK)
            st = s_s[d, sq, p]
            o = intra_r[0, p, sq, pl.ds(r0, TOK), :] + _dot_nt(qd_r[0, p, sq, pl.ds(r0, TOK), :], st.astype(BF16))
            o_r[sq, pl.ds(r0, TOK), p * LANES:(p + 1) * LANES] = o
            glr = gl_r[0, p, sq, pl.ds(pl.multiple_of(g * 8, 8), 8), :][0:1]
            s_s[d, sq, p] = st * glr + kv_r[0, p, sq, pl.ds(r0, TOK), :]
        return carry

    lax.fori_loop(0, n_g, body, 0)

    if not has_s0:
        @pl.when(t == n_t - 1)
        def _():
            for (d, sq, p) in chains:
                s = s_s[d, sq, p].T
                sout_ref[sq, 0, d, 2 * p] = s[0:64, 0:64]
                sout_ref[sq, 0, d, 2 * p + 1] = s[64:128, 64:128]


def _hgrn_scan(prep, n_seq, seq_len, layer, s0t=None, s_all=None):
    pairs = C_HEADS // 2
    has_s0 = s0t is not None
    sb = n_seq if n_seq <= 2 else 8
    tb = min(seq_len, 512)
    n_t = seq_len // tb
    arrs = [a.reshape(2, pairs, n_seq, a.shape[2] // n_seq, a.shape[3]) for a in prep]

    def specs(d):
        tmap = (lambda s, t: (d, 0, s, t, 0)) if d == 0 else (lambda s, t: (d, 0, s, n_t - 1 - t, 0))
        return [pl.BlockSpec((1, pairs, sb, a.shape[3] // n_t, a.shape[4]), tmap) for a in arrs]

    in_specs = specs(0) + specs(1)
    args = arrs + arrs
    st_spec = pl.BlockSpec((sb, 1, 2, C_HEADS, C_DK, C_DV), lambda s, t: (s, layer, 0, 0, 0, 0))
    width = C_HEADS * C_DV
    out_specs = [
        pl.BlockSpec((sb, tb, width), lambda s, t: (s, t, 0)),
        pl.BlockSpec((sb, tb, width), lambda s, t: (s, n_t - 1 - t, 0)),
    ]
    out_shape = [jax.ShapeDtypeStruct((n_seq, seq_len, width), F32)] * 2
    if has_s0:
        in_specs.append(st_spec)
        args.append(s0t)
        aliases = {}
    else:
        in_specs.append(pl.BlockSpec(memory_space=pl.ANY))
        args.append(s_all)
        out_specs.append(st_spec)
        out_shape.append(jax.ShapeDtypeStruct(s_all.shape, F32))
        aliases = {len(args) - 1: 2}
    outs = pl.pallas_call(
        functools.partial(_hgrn_scan_kernel, sb=sb, tb=tb, has_s0=has_s0),
        grid=(n_seq // sb, n_t),
        in_specs=in_specs,
        out_specs=out_specs,
        out_shape=out_shape,
        input_output_aliases=aliases,
        scratch_shapes=[pltpu.VMEM((2, sb, pairs, LANES, LANES), F32)],
        compiler_params=_cparams("parallel", "arbitrary"),
        name="hgrn_scan_s0" if has_s0 else "hgrn_scan",
    )(*args)
    n = n_seq * seq_len
    return (outs[0].reshape(n, width), outs[1].reshape(n, width)) + tuple(outs[2:])


FFN_SPLIT = 2
ROW_SPLIT = 2


def _tail_kernel(x_ref, oa_ref, obf_ref, obb_ref, bg_ref, ocf_ref, ocb_ref, cg_ref, mod_ref, bn_ref, cn_ref,
                 wo_ref, lng_ref, lnb_ref, wi_ref, wd_ref, o_ref, *, d_model, d_ff, alpha):
    m = mod_ref[0]
    na, nb = oa_ref.shape[1], obf_ref.shape[1]
    tm = x_ref.shape[0]
    parts = [slice(i * tm // ROW_SPLIT, (i + 1) * tm // ROW_SPLIT) for i in range(ROW_SPLIT)]
    ob = [_gated_head_norm(obf_ref[r, :] + obb_ref[r, :], bn_ref[0], bg_ref[r, :]) for r in parts]
    oc = [_gated_head_norm(ocf_ref[r, :] + ocb_ref[r, :], cn_ref[0], cg_ref[r, :]) for r in parts]
    y = [_bdot(oa_ref[r, :], wo_ref[0:na]) + _bdot(ob[i], wo_ref[na:na + nb]) + _bdot(oc[i], wo_ref[na + nb:])
         for i, r in enumerate(parts)]
    x1 = [_layer_norm(alpha * x_ref[r, :] + m[:, 2 * d_model:3 * d_model] * y[i], lng_ref[0, 0:1], lnb_ref[0, 0:1])
          for i, r in enumerate(parts)]
    h = [(x1[i] * (1.0 + m[:, 4 * d_model:5 * d_model]) + m[:, 3 * d_model:4 * d_model]).astype(BF16)
         for i in range(ROW_SPLIT)]
    ck = d_ff // FFN_SPLIT
    acc = [None] * ROW_SPLIT
    for c0 in range(0, d_ff, ck):
        for i in range(ROW_SPLIT):
            gt = _dot(h[i], wi_ref[:, c0:c0 + ck])
            up = _dot(h[i], wi_ref[:, d_ff + c0:d_ff + c0 + ck])
            part = _dot((_silu(gt) * up).astype(BF16), wd_ref[c0:c0 + ck, :])
            acc[i] = part if acc[i] is None else acc[i] + part
    for i, r in enumerate(parts):
        o_ref[r, :] = _layer_norm(alpha * x1[i] + m[:, 5 * d_model:6 * d_model] * acc[i],
                                  lng_ref[0, 1:2], lnb_ref[0, 1:2])


def _tail(x, oa, obf, obb, zb, ocf, ocb, zc, mod, layer, bnorm, cnorm, w_out, ln_g, ln_b, w_ffn_in, w_ffn_out,
          alpha):
    n, d = x.shape
    tm = 512
    d_ff = w_ffn_out.shape[1]
    rows_per_mod = n // mod.shape[0]
    nb, nc = obf.shape[1], ocf.shape[1]
    row = lambda w_: pl.BlockSpec((tm, w_), lambda i: (i, 0))
    last_cols = lambda a, w_: pl.BlockSpec((tm, w_), lambda i: (i, a.shape[1] // w_ - 1))
    per_layer = lambda w_: pl.BlockSpec((1, 1, w_), lambda i: (layer, 0, 0))
    resident = lambda a: pl.BlockSpec((None,) + a.shape[1:], lambda i: (layer, 0, 0), pipeline_mode=pl.Buffered(1))
    return pl.pallas_call(
        functools.partial(_tail_kernel, d_model=d, d_ff=d_ff, alpha=alpha),
        grid=(n // tm,),
        in_specs=[
            row(d), row(oa.shape[1]),
            row(nb), row(nb), last_cols(zb, nb),
            row(nc), row(nc), last_cols(zc, nc),
            pl.BlockSpec((1, 1, mod.shape[2]), lambda i: ((i * tm) // rows_per_mod, 0, 0)),
            per_layer(nb), per_layer(nc),
            resident(w_out),
            pl.BlockSpec((1, 2, d), lambda i: (layer, 0, 0)),
            pl.BlockSpec((1, 2, d), lambda i: (layer, 0, 0)),
            resident(w_ffn_in), resident(w_ffn_out),
        ],
        out_specs=pl.BlockSpec((tm, d), lambda i: (i, 0)),
        out_shape=jax.ShapeDtypeStruct((n, d), F32),
        compiler_params=_cparams("parallel"),
        name="out_proj_ffn",
    )(x, oa, obf, obb, zb, ocf, ocb, zc, mod, bnorm, cnorm, w_out, ln_g, ln_b, w_ffn_in, w_ffn_out)


def _pair_scalar_order(p):
    return [d * B_HEADS + 2 * p + hh for d in range(2) for hh in range(2)]


N_SC = 4 * B_HEADS


def _w_in_split_kernel(w_ref, sel_ref, ws_ref, wc_ref):
    x = w_ref[0]
    b0 = N_A + N_B
    scal = _dot(sel_ref[...], x[b0:b0 + N_SC, :].astype(BF16))
    ws_ref[0] = scal.T.astype(BF16)
    wc_ref[0] = x[b0 + N_SC:, :].T.astype(BF16)


def _layout_w_in(w_in):
    depth, d, d_in = w_in.shape
    w_t = jnp.swapaxes(w_in, 1, 2)
    sel = np.zeros((N_S, N_SC), np.float32)
    for p in range(B_HEADS // 2):
        order = _pair_scalar_order(p)
        for j, col in enumerate(order + [2 * B_HEADS + i for i in order]):
            sel[p * LANES + j, col] = 1.0
    ws, wc = pl.pallas_call(
        _w_in_split_kernel,
        grid=(depth,),
        in_specs=[pl.BlockSpec((1, d_in, d), lambda l: (l, 0, 0)),
                  pl.BlockSpec((N_S, N_SC), lambda l: (0, 0))],
        out_specs=[pl.BlockSpec((1, d, N_S), lambda l: (l, 0, 0)),
                   pl.BlockSpec((1, d, N_C), lambda l: (l, 0, 0))],
        out_shape=[jax.ShapeDtypeStruct((depth, d, N_S), BF16), jax.ShapeDtypeStruct((depth, d, N_C), BF16)],
        compiler_params=_cparams("parallel"),
        name="w_in_split",
    )(w_t, jnp.asarray(sel, BF16))
    return w_t, ws, wc


def _delta_params(delta_a_log, delta_dt_bias):
    depth = delta_a_log.shape[0]
    pairs = B_HEADS // 2
    al = delta_a_log.reshape(depth, 2 * B_HEADS)
    dt = delta_dt_bias.reshape(depth, 2 * B_HEADS)
    prow = jnp.zeros((depth, pairs, 8, LANES), F32)
    pcol = jnp.zeros((depth, pairs, 8, LANES), F32)
    for p in range(pairs):
        order = np.asarray(_pair_scalar_order(p))
        prow = prow.at[:, p, 0, 4:8].set(al[:, order]).at[:, p, 1, 4:8].set(dt[:, order])
        pcol = pcol.at[:, p, 4:8, 0].set(al[:, order]).at[:, p, 4:8, 1].set(dt[:, order])
    return prow, pcol


def kernel(x_prompt, x_sample, cache_attn_k, cache_attn_v, state_delta, state_hgrn, c, c_ctx, w_mod, b_mod, w_in, conv_w, delta_a_log, delta_dt_bias, delta_norm, hgrn_lb, hgrn_norm, diff_lambda, diff_norm, w_out, ln_g, ln_b, w_ffn_in, w_ffn_out):
    depth, d, _ = w_in.shape
    bp, lp, _ = x_prompt.shape
    bs, ls, _ = x_sample.shape
    alpha = (2 * depth) ** 0.25

    cond_rows = 8 * ((1 + bs + 7) // 8)
    cond = jnp.concatenate([c_ctx[None], c, jnp.zeros((cond_rows - 1 - bs, d), F32)], axis=0)
    mod = _modulation(cond, w_mod, b_mod)

    w_in_l = _layout_w_in(w_in)
    w_out_b = w_out.astype(BF16)
    w_ffn_in_b = w_ffn_in.astype(BF16)
    w_ffn_out_b = w_ffn_out.astype(BF16)
    prow, pcol = _delta_params(delta_a_log, delta_dt_bias)
    bnorm = jnp.tile(delta_norm, (1, B_HEADS)).reshape(depth, 1, B_HEADS * B_DV)
    cnorm = jnp.tile(hgrn_norm, (1, C_HEADS)).reshape(depth, 1, C_HEADS * C_DV)
    lb = hgrn_lb.reshape(2, depth, C_HEADS // 2, LANES).transpose(2, 0, 1, 3)
    rope = _rope_tables(ls)
    dconsts = _delta_consts()
    head_mask = dconsts[0][0, M_HEADS]
    hconsts = _hgrn_consts()
    past = cache_attn_k.shape[2]
    ck = cache_attn_k.reshape(bs, depth, past, A_HEADS * 2 * A_QK)
    cv = cache_attn_v.reshape(bs, depth, past, A_HEADS * A_V)
    sh0 = jnp.swapaxes(state_hgrn, -1, -2)
    tails = (bnorm, cnorm, w_out_b, ln_g, ln_b, w_ffn_in_b, w_ffn_out_b, alpha)

    xp = x_prompt.reshape(bp * lp, d)
    xs = x_sample.reshape(bs * ls, d)
    k_all = jnp.zeros((bp, depth, lp * A_HEADS, 2 * A_QK), F32)
    v_all = jnp.zeros((bp, depth, lp * A_HEADS, A_V), F32)
    sd_all = jnp.zeros((bp, depth, 2, B_HEADS, B_DK, B_DV), F32)
    sh_all = jnp.zeros((bp, depth, 2, C_HEADS, C_DK, C_DV), F32)
    for l in range(depth):
        mod_p = mod[l, 0:1][:, None, :]
        mod_s = mod[l, 1:1 + bs][:, None, :]

        zq, k_all, v_all, zb, zs, zc, zsr = _in_proj(xp, mod_p, w_in_l, lp, kv_out=(k_all, v_all), layer=l)
        oa = _attention(zq, lp, l, diff_lambda, diff_norm, kv_all=(k_all, v_all))
        prep = _delta_prep(zb, zs, zsr, lp, l, conv_w, prow, pcol, dconsts)
        obf, obb, sd_all = _delta_scan(prep, bp, lp, l, head_mask, s_all=sd_all)
        ocf, ocb, sh_all = _hgrn_scan(_hgrn_prep(zc, l, lb, hconsts), bp, lp, l, s_all=sh_all)
        xp = _tail(xp, oa, obf, obb, zb, ocf, ocb, zc, mod_p, l, *tails)

        za, zb, zs, zc, zsr = _in_proj(xs, mod_s, w_in_l, ls, rope_tabs=rope, layer=l)
        oa = _attention(za, ls, l, diff_lambda, diff_norm, ctx_kv=(ck, cv))
        prep = _delta_prep(zb, zs, zsr, ls, l, conv_w, prow, pcol, dconsts)
        obf, obb = _delta_scan(prep, bs, ls, l, head_mask, s0=state_delta)
        ocf, ocb = _hgrn_scan(_hgrn_prep(zc, l, lb, hconsts), bs, ls, l, s0t=sh0)
        xs = _tail(xs, oa, obf, obb, zb, ocf, ocb, zc, mod_s, l, *tails)

    return (xp.reshape(bp, lp, d), xs.reshape(bs, ls, d),
            k_all.reshape(bp, depth, lp, A_HEADS, 2 * A_QK), v_all.reshape(bp, depth, lp, A_HEADS, A_V),
            sd_all, sh_all)
```
